```python
import math
import jax
import jax.numpy as jnp
from jax import lax
import numpy as np

D_MODEL = 1024
BATCH = 1
SEQ = 16384
DEPTH = 4
DEC_BATCH = 16
DEC_SEQ = 4096
PAST_LEN = 128

D_MIX = D_MODEL
N_FOURIER_GROUPS = 4
FOURIER_GROUP_DIM = D_MIX // 16
FOURIER_DIM = N_FOURIER_GROUPS * FOURIER_GROUP_DIM
HEAD_DIM = 128
N_GDN_HEADS = (D_MIX - FOURIER_DIM) // HEAD_DIM
GDN_DIM = N_GDN_HEADS * HEAD_DIM
QKV_DIM = 3 * GDN_DIM
OFF_F = FOURIER_DIM
OFF_QKV = OFF_F + QKV_DIM
OFF_Z = OFF_QKV + GDN_DIM
OFF_B = OFF_Z + 2 * N_GDN_HEADS
D_IN = OFF_B + 2 * N_GDN_HEADS
CONV_K = 5
CHUNK = 64
D_FF = 2816
N_EXPERTS = 8
TOP_K = 2
D_EXPERT = 3584
N_DENSE = (DEPTH + 1) // 2
N_MOE = DEPTH // 2
MOE_BLOCK_MAX = 1024
NORM_EPS = 1e-6

kernel_name = "hymba_fnet_gdn_moe_encoder"


def rmsnorm(x, w):
    x32 = x.astype(jnp.float32)
    y = x32 * lax.rsqrt(jnp.mean(x32 * x32, axis=-1, keepdims=True) + NORM_EPS)
    return (y * w.astype(jnp.float32)).astype(x.dtype)


def l2norm(x):
    return x * lax.rsqrt(jnp.sum(x * x, axis=-1, keepdims=True) + 1e-6)


def centred_depthwise_conv(x, w):
    c = x.shape[-1]
    return lax.conv_general_dilated(
        x, w[:, None, :].astype(x.dtype), window_strides=(1,),
        padding=[(CONV_K // 2, CONV_K // 2)],
        dimension_numbers=("NWC", "WIO", "NWC"), feature_group_count=c)


def fourier_mix(f):
    b, s, _ = f.shape
    f = f.astype(jnp.float32).reshape(b, s, N_FOURIER_GROUPS, FOURIER_GROUP_DIM)
    out = jnp.fft.fftn(f, axes=(1, 3), norm="ortho").real
    return out.reshape(b, s, FOURIER_DIM)


def gated_delta_chunked(q, k, v, g, beta):
    b, s, h, dk = q.shape
    dv = v.shape[-1]
    n = s // CHUNK

    def to_chunks(t):
        return t.reshape(b, n, CHUNK, h, *t.shape[3:]).swapaxes(2, 3)

    q, k, v, g, beta = map(to_chunks, (q, k, v, g, beta))
    g_cum = jnp.cumsum(g, axis=-1)
    tril_incl = jnp.tril(jnp.ones((CHUNK, CHUNK), dtype=bool))
    tril_strict = jnp.tril(jnp.ones((CHUNK, CHUNK), dtype=bool), k=-1)
    decay = jnp.exp(jnp.where(tril_incl, g_cum[..., :, None] - g_cum[..., None, :], -jnp.inf))
    k_beta = k * beta[..., None]
    v_beta = v * beta[..., None]
    a_mat = jnp.where(tril_strict, jnp.einsum("bnhid,bnhjd->bnhij", k_beta, k) * decay, 0.0)
    eye = jnp.eye(CHUNK, dtype=jnp.float32)
    rhs = jnp.concatenate([v_beta, k_beta * jnp.exp(g_cum)[..., None]], axis=-1)
    sol = lax.linalg.triangular_solve(a_mat + eye, rhs, left_side=True, lower=True,
                                      unit_diagonal=True)
    u = sol[..., :dv]
    w = sol[..., dv:]
    attn_intra = jnp.where(tril_incl, jnp.einsum("bnhid,bnhjd->bnhij", q, k) * decay, 0.0)
    q_dec = q * jnp.exp(g_cum)[..., None]
    g_last = g_cum[..., -1]
    k_dec = k * jnp.exp(g_last[..., None] - g_cum)[..., None]

    def step(state, inp):
        qg, w_c, u_c, a_c, kd, gl = inp
        v_new = u_c - jnp.einsum("bhcd,bhde->bhce", w_c, state)
        o = jnp.einsum("bhcd,bhde->bhce", qg, state) + jnp.einsum("bhij,bhje->bhie", a_c, v_new)
        state = state * jnp.exp(gl)[..., None, None] + jnp.einsum("bhcd,bhce->bhde", kd, v_new)
        return state, o

    xs = tuple(jnp.moveaxis(t, 1, 0) for t in (q_dec, w, u, attn_intra, k_dec, g_last))
    state0 = jnp.zeros((b, h, dk, dv), jnp.float32)
    _, o = lax.scan(step, state0, xs)
    return o.transpose(1, 0, 3, 2, 4).reshape(b, s, h, dv)


def mixer(h, w_in, conv_w, a_log, dt_bias, gdn_norm, w_out):
    b, s, _ = h.shape
    proj = jnp.einsum("bsd,de->bse", h, w_in)
    f, qkv, z, bb, aa = jnp.split(proj, [OFF_F, OFF_QKV, OFF_Z, OFF_B], axis=-1)
    qkv = jax.nn.silu(centred_depthwise_conv(qkv, conv_w))
    q, k, v = jnp.split(qkv, 3, axis=-1)

    def heads(t):
        return t.reshape(b, s, N_GDN_HEADS, HEAD_DIM).astype(jnp.float32)

    q = l2norm(heads(q)) * (HEAD_DIM ** -0.5)
    k = l2norm(heads(k))
    v = heads(v)
    beta = jax.nn.sigmoid(bb.astype(jnp.float32)).reshape(b, s, 2, N_GDN_HEADS)
    g = -jnp.exp(a_log.astype(jnp.float32)) * jax.nn.softplus(
        aa.astype(jnp.float32).reshape(b, s, 2, N_GDN_HEADS) + dt_bias.astype(jnp.float32))
    o_fwd = gated_delta_chunked(q, k, v, g[:, :, 0], beta[:, :, 0])

    def rev(t):
        return jnp.flip(t, axis=1)

    o_bwd = rev(gated_delta_chunked(rev(q), rev(k), rev(v), rev(g[:, :, 1]), rev(beta[:, :, 1])))
    o = rmsnorm(o_fwd + o_bwd, gdn_norm) * jax.nn.silu(heads(z))
    mixed = jnp.concatenate([fourier_mix(f), o.reshape(b, s, GDN_DIM)], axis=-1).astype(h.dtype)
    return jnp.einsum("bse,ed->bsd", mixed, w_out)


def swiglu(h, w_gate, w_up, w_down):
    hid = jax.nn.silu(jnp.einsum("bsd,df->bsf", h, w_gate)) * jnp.einsum("bsd,df->bsf", h, w_up)
    return jnp.einsum("bsf,fd->bsd", hid, w_down)


def moe_ffn(h, w_router, w_gate, w_up, w_down):
    b, s, d = h.shape
    t = b * s
    xt = h.reshape(t, d)
    logits = jnp.einsum("td,de->te", xt, w_router).astype(jnp.float32)
    probs = jax.nn.softmax(logits, axis=-1)
    top_p, top_i = lax.top_k(probs, TOP_K)
    top_p = top_p / jnp.sum(top_p, axis=-1, keepdims=True)
    combine = jnp.sum(jax.nn.one_hot(top_i, N_EXPERTS, dtype=jnp.float32) * top_p[..., None], axis=1)
    blk = math.gcd(t, MOE_BLOCK_MAX)

    def block(args):
        xb, cb = args
        hid = jax.nn.silu(jnp.einsum("td,edf->tef", xb, w_gate)) * jnp.einsum("td,edf->tef", xb, w_up)
        hid = hid * cb[..., None].astype(hid.dtype)
        return jnp.einsum("tef,efd->td", hid, w_down)

    y = lax.map(block, (xt.reshape(t // blk, blk, d), combine.reshape(t // blk, blk, N_EXPERTS)))
    return y.reshape(b, s, d)


def encoder(x, c, w_ada, b_ada, norm_mix, norm_ffn, w_in, conv_w, a_log, dt_bias, gdn_norm, w_out,
            w_ffn_gate, w_ffn_up, w_ffn_down, w_router, w_exp_gate, w_exp_up, w_exp_down, norm_final):
    c_act = jax.nn.silu(c)
    for l in range(DEPTH):
        ada = jnp.einsum("bd,de->be", c_act, w_ada[l]) + b_ada[l]
        sh1, sc1, g1, sh2, sc2, g2 = [t[:, None, :] for t in jnp.split(ada, 6, axis=-1)]
        h = rmsnorm(x, norm_mix[l]) * (1 + sc1) + sh1
        x = x + g1 * mixer(h, w_in[l], conv_w[l], a_log[l], dt_bias[l], gdn_norm[l], w_out[l])
        h = rmsnorm(x, norm_ffn[l]) * (1 + sc2) + sh2
        if l % 2 == 0:
            f = swiglu(h, w_ffn_gate[l // 2], w_ffn_up[l // 2], w_ffn_down[l // 2])
        else:
            f = moe_ffn(h, w_router[l // 2], w_exp_gate[l // 2], w_exp_up[l // 2], w_exp_down[l // 2])
        x = x + g2 * f
    return rmsnorm(x, norm_final)


def setup_inputs(seed: int = 0) -> dict:
    key = jax.random.key(seed)
    ks = jax.random.split(key, 24)

    def nrm(k, shape, std):
        return jax.random.normal(k, shape, jnp.float32) * std

    d = D_MODEL
    dt = jnp.exp(jax.random.uniform(ks[10], (DEPTH, 2, N_GDN_HEADS), jnp.float32,
                                    math.log(1e-3), math.log(1e-1)))
    return {
        "x_prompt": nrm(ks[0], (BATCH, SEQ, d), 1.0),
        "x_sample": nrm(ks[1], (DEC_BATCH, DEC_SEQ, d), 1.0),
        "c_prompt": nrm(ks[2], (BATCH, d), 1.0),
        "c_sample": nrm(ks[3], (DEC_BATCH, d), 1.0),
        "w_ada": nrm(ks[4], (DEPTH, d, 6 * d), 0.5 * d ** -0.5),
        "b_ada": nrm(ks[5], (DEPTH, 6 * d), 0.02),
        "norm_mix": 1.0 + nrm(ks[6], (DEPTH, d), 0.02),
        "norm_ffn": 1.0 + nrm(ks[7], (DEPTH, d), 0.02),
        "w_in": nrm(ks[8], (DEPTH, d, D_IN), d ** -0.5),
        "conv_w": nrm(ks[9], (DEPTH, CONV_K, QKV_DIM), CONV_K ** -0.5),
        "a_log": jnp.log(jax.random.uniform(ks[11], (DEPTH, 2, N_GDN_HEADS), jnp.float32, 1.0, 16.0)),
        "dt_bias": dt + jnp.log(-jnp.expm1(-dt)),
        "gdn_norm": 1.0 + nrm(ks[12], (DEPTH, HEAD_DIM), 0.02),
        "w_out": nrm(ks[13], (DEPTH, D_MIX, d), D_MIX ** -0.5),
        "w_ffn_gate": nrm(ks[14], (N_DENSE, d, D_FF), d ** -0.5),
        "w_ffn_up": nrm(ks[15], (N_DENSE, d, D_FF), d ** -0.5),
        "w_ffn_down": nrm(ks[16], (N_DENSE, D_FF, d), D_FF ** -0.5),
        "w_router": nrm(ks[17], (N_MOE, d, N_EXPERTS), d ** -0.5),
        "w_exp_gate": nrm(ks[18], (N_MOE, N_EXPERTS, d, D_EXPERT), d ** -0.5),
        "w_exp_up": nrm(ks[19], (N_MOE, N_EXPERTS, d, D_EXPERT), d ** -0.5),
        "w_exp_down": nrm(ks[20], (N_MOE, N_EXPERTS, D_EXPERT, d), D_EXPERT ** -0.5),
        "norm_final": 1.0 + nrm(ks[21], (d,), 0.02),
    }


def reference(x_prompt, x_sample, c_prompt, c_sample, w_ada, b_ada, norm_mix, norm_ffn, w_in, conv_w,
              a_log, dt_bias, gdn_norm, w_out, w_ffn_gate, w_ffn_up, w_ffn_down, w_router,
              w_exp_gate, w_exp_up, w_exp_down, norm_final):
    y_prompt = encoder(x_prompt, c_prompt, w_ada, b_ada, norm_mix, norm_ffn, w_in, conv_w, a_log,
                       dt_bias, gdn_norm, w_out, w_ffn_gate, w_ffn_up, w_ffn_down, w_router,
                       w_exp_gate, w_exp_up, w_exp_down, norm_final)
    y_sample = encoder(x_sample, c_sample, w_ada, b_ada, norm_mix, norm_ffn, w_in, conv_w, a_log,
                       dt_bias, gdn_norm, w_out, w_ffn_gate, w_ffn_up, w_ffn_down, w_router,
                       w_exp_gate, w_exp_up, w_exp_down, norm_final)
    return (y_prompt, y_sample)
```

```python
import functools
import math

import numpy as np
import jax
import jax.numpy as jnp
from jax import lax
from jax.experimental import pallas as pl
from jax.experimental.pallas import tpu as pltpu

F32 = jnp.float32
BF16 = jnp.bfloat16

D_MODEL = 1024
DEPTH = 4
N_GROUPS = 4
GROUP_DIM = 64
F_DIM = N_GROUPS * GROUP_DIM
HD = 128
NH = 6
G_DIM = NH * HD
QKV_DIM = 3 * G_DIM
CONV_K = 5
CHUNK = 64
D_FF = 2816
N_EXPERTS = 8
D_EXPERT = 3584
NORM_EPS = 1e-6

LANES = 128
SUBLANES = 8
VMEM_LIMIT = 56 * 1024 * 1024
DFT1 = 128

GL_GAMMA, GL_BETA, GL_EG, GL_EGR = 0, 16, 32, 48
NEG_BIG = -1e30


def _cparams(sem):
    return pltpu.CompilerParams(dimension_semantics=sem, vmem_limit_bytes=VMEM_LIMIT)


def _mod_norm(x, nw, sc, sh):
    ms = jnp.mean(x * x, axis=-1, keepdims=True)
    y = x * lax.rsqrt(ms + NORM_EPS)
    return (y * nw) * (1.0 + sc) + sh


def _silu(x):
    return x * (1.0 / (1.0 + jnp.exp(-x)))


def _ada_kernel(c_ref, w_ref, b_ref, o_ref):
    c = _silu(c_ref[...])
    o_ref[0] = jnp.dot(c, w_ref[0], precision=lax.Precision.HIGHEST,
                       preferred_element_type=F32) + b_ref[0]


def ada_call(c_all, w_ada, b_ada):
    nrow = c_all.shape[0]
    depth, d, d6 = w_ada.shape
    tn = 1024
    return pl.pallas_call(
        _ada_kernel,
        grid=(depth, d6 // tn),
        in_specs=[
            pl.BlockSpec((nrow, d), lambda l, j: (0, 0)),
            pl.BlockSpec((1, d, tn), lambda l, j: (l, 0, j)),
            pl.BlockSpec((1, 1, tn), lambda l, j: (l, 0, j)),
        ],
        out_specs=pl.BlockSpec((1, nrow, tn), lambda l, j: (l, 0, j)),
        out_shape=jax.ShapeDtypeStruct((depth, nrow, d6), F32),
        compiler_params=_cparams(("arbitrary", "arbitrary")),
        name="ada",
    )(c_all, w_ada, b_ada.reshape(depth, 1, d6))


def _inproj_kernel(x_ref, mod_ref, nw_ref, wf_ref, wqkv_ref, wz_ref, wba_ref,
                   f_ref, qkv_ref, z_ref, ba_ref):
    h = _mod_norm(x_ref[...], nw_ref[...], mod_ref[0, 1:2, :], mod_ref[0, 0:1, :]).astype(BF16)
    f_ref[...] = jnp.dot(h, wf_ref[...], preferred_element_type=F32)
    qkv_ref[...] = jnp.dot(h, wqkv_ref[...], preferred_element_type=F32)
    z_ref[...] = jnp.dot(h, wz_ref[...], preferred_element_type=F32)
    ba_ref[...] = jnp.dot(h, wba_ref[...], preferred_element_type=F32)


def inproj_call(x, mod, nw, wf, wqkv, wz, wba, *, seg, tm):
    t, d = x.shape
    const = lambda i: (0, 0)
    row = lambda i: (i, 0)
    return pl.pallas_call(
        _inproj_kernel,
        grid=(t // tm,),
        in_specs=[
            pl.BlockSpec((tm, d), row),
            pl.BlockSpec((1, SUBLANES, d), lambda i: (i * tm // seg, 0, 0)),
            pl.BlockSpec((1, d), const),
            pl.BlockSpec(wf.shape, const),
            pl.BlockSpec(wqkv.shape, const),
            pl.BlockSpec(wz.shape, const),
            pl.BlockSpec(wba.shape, const),
        ],
        out_specs=[
            pl.BlockSpec((tm, F_DIM), row),
            pl.BlockSpec((tm, QKV_DIM), row),
            pl.BlockSpec((tm, G_DIM), row),
            pl.BlockSpec((tm, LANES), row),
        ],
        out_shape=[
            jax.ShapeDtypeStruct((t, F_DIM), F32),
            jax.ShapeDtypeStruct((t, QKV_DIM), F32),
            jax.ShapeDtypeStruct((t, G_DIM), F32),
            jax.ShapeDtypeStruct((t, LANES), F32),
        ],
        compiler_params=_cparams(("arbitrary",)),
        name="inproj",
    )(x, mod, nw, wf, wqkv, wz, wba)


def _conv_kernel(first_ref, last_ref,
                 qkv_ref, prev_ref, next_ref, cw_ref, ba_ref, gp_ref,
                 q_ref, k_ref, v_ref, gcol_ref, grow_ref, gend_ref,
                 xe_ref):
    i = pl.program_id(0)
    tm = qkv_ref.shape[0]
    nc = tm // CHUNK
    halo = SUBLANES
    pm = jnp.where(first_ref[i] == 1, 0.0, 1.0)
    nm = jnp.where(last_ref[i] == 1, 0.0, 1.0)
    xe_ref[0:halo, :] = prev_ref[...] * pm
    xe_ref[halo:halo + tm, :] = qkv_ref[...]
    xe_ref[halo + tm:, :] = next_ref[...] * nm

    outs = (q_ref, k_ref, v_ref)
    for s in range(3 * NH):
        lo = s * HD
        acc = None
        for j in range(CONV_K):
            r0 = halo - CONV_K // 2 + j
            term = xe_ref[r0:r0 + tm, lo:lo + HD] * cw_ref[j:j + 1, lo:lo + HD]
            acc = term if acc is None else acc + term
        y = _silu(acc)
        which, h = divmod(s, NH)
        if which < 2:
            y = y * lax.rsqrt(jnp.sum(y * y, axis=-1, keepdims=True) + 1e-6)
        if which == 0:
            y = y * (HD ** -0.5)
        outs[which][:, h * HD:(h + 1) * HD] = y

    ba = ba_ref[...]
    lane = lax.broadcasted_iota(jnp.int32, (tm, LANES), 1)
    beta = 1.0 / (1.0 + jnp.exp(-ba))
    xs = ba + gp_ref[0:1, :]
    softplus = jnp.maximum(xs, 0.0) + jnp.log(1.0 + jnp.exp(-jnp.abs(xs)))
    g = -jnp.exp(gp_ref[1:2, :]) * softplus
    g = jnp.where((lane >= 2 * NH) & (lane < 4 * NH), g, 0.0)
    g = pltpu.roll(g, LANES - 2 * NH, axis=1)

    rowc = lax.broadcasted_iota(jnp.int32, (tm, LANES), 0) % CHUNK
    p = g
    sh = 1
    while sh < CHUNK:
        p = p + jnp.where(rowc >= sh, pltpu.roll(p, sh, axis=0), 0.0)
        sh *= 2
    g3 = g.reshape(nc, CHUNK, LANES)
    tot = jnp.broadcast_to(jnp.sum(g3, axis=1, keepdims=True), (nc, CHUNK, LANES)).reshape(tm, LANES)
    is_bwd = (lane >= NH) & (lane < 2 * NH)
    gamma = jnp.where(is_bwd, tot - p + g, p)
    eg = jnp.exp(gamma)
    egr = jnp.exp(tot - gamma)
    m12 = lane < 2 * NH
    gcol = (jnp.where(m12, gamma, 0.0)
            + pltpu.roll(jnp.where(m12, beta, 0.0), GL_BETA, axis=1)
            + pltpu.roll(jnp.where(m12, eg, 0.0), GL_EG, axis=1)
            + pltpu.roll(jnp.where(m12, egr, 0.0), GL_EGR, axis=1))
    gcol_ref[...] = gcol
    gam_t = jnp.where(m12, gamma, 0.0).T[0:2 * SUBLANES, :]
    g_t = g.T[0:2 * SUBLANES, :]
    for c in range(nc):
        grow_ref[c] = gam_t[:, c * CHUNK:(c + 1) * CHUNK]
        tc = jnp.sum(g_t[:, c * CHUNK:(c + 1) * CHUNK], axis=-1, keepdims=True)
        gend_ref[c] = jnp.exp(jnp.broadcast_to(tc, (2 * SUBLANES, LANES)))


def conv_call(first, last, qkv, cw, ba, gp, *, tm):
    t = qkv.shape[0]
    nblk8 = t // SUBLANES
    r8 = tm // SUBLANES
    nc = tm // CHUNK
    grid_spec = pltpu.PrefetchScalarGridSpec(
        num_scalar_prefetch=2,
        grid=(t // tm,),
        in_specs=[
            pl.BlockSpec((tm, QKV_DIM), lambda i, f, l: (i, 0)),
            pl.BlockSpec((SUBLANES, QKV_DIM), lambda i, f, l: (jnp.maximum(i * r8 - 1, 0), 0)),
            pl.BlockSpec((SUBLANES, QKV_DIM), lambda i, f, l: (jnp.minimum((i + 1) * r8, nblk8 - 1), 0)),
            pl.BlockSpec((SUBLANES, QKV_DIM), lambda i, f, l: (0, 0)),
            pl.BlockSpec((tm, LANES), lambda i, f, l: (i, 0)),
            pl.BlockSpec((SUBLANES, LANES), lambda i, f, l: (0, 0)),
        ],
        out_specs=[
            pl.BlockSpec((tm, G_DIM), lambda i, f, l: (i, 0)),
            pl.BlockSpec((tm, G_DIM), lambda i, f, l: (i, 0)),
            pl.BlockSpec((tm, G_DIM), lambda i, f, l: (i, 0)),
            pl.BlockSpec((tm, LANES), lambda i, f, l: (i, 0)),
            pl.BlockSpec((nc, 2 * SUBLANES, CHUNK), lambda i, f, l: (i, 0, 0)),
            pl.BlockSpec((nc, 2 * SUBLANES, LANES), lambda i, f, l: (i, 0, 0)),
        ],
        scratch_shapes=[pltpu.VMEM((tm + 2 * SUBLANES, QKV_DIM), F32)],
    )
    return pl.pallas_call(
        _conv_kernel,
        grid_spec=grid_spec,
        out_shape=[
            jax.ShapeDtypeStruct((t, G_DIM), F32),
            jax.ShapeDtypeStruct((t, G_DIM), F32),
            jax.ShapeDtypeStruct((t, G_DIM), F32),
            jax.ShapeDtypeStruct((t, LANES), F32),
            jax.ShapeDtypeStruct((t // CHUNK, 2 * SUBLANES, CHUNK), F32),
            jax.ShapeDtypeStruct((t // CHUNK, 2 * SUBLANES, LANES), F32),
        ],
        compiler_params=_cparams(("arbitrary",)),
        name="conv_gates",
    )(first, last, qkv, qkv, qkv, cw, ba, gp)


def _gdn_kernel(rf_ref, rb_ref,
                qf_ref, kf_ref, vf_ref, gcf_ref, grf_ref, gef_ref,
                qb_ref, kb_ref, vb_ref, gcb_ref, grb_ref, geb_ref,
                of_ref, ob_ref, s_ref):
    i = pl.program_id(0)
    tb = qf_ref.shape[0]
    nc = tb // CHUNK

    @pl.when(rf_ref[i] == 1)
    def _():
        s_ref[0:NH] = jnp.zeros((NH, HD, HD), F32)

    @pl.when(rb_ref[i] == 1)
    def _():
        s_ref[NH:2 * NH] = jnp.zeros((NH, HD, HD), F32)

    row = lax.broadcasted_iota(jnp.int32, (CHUNK, CHUNK), 0)
    col = lax.broadcasted_iota(jnp.int32, (CHUNK, CHUNK), 1)
    eye = jnp.where(row == col, 1.0, 0.0).astype(F32)
    dirs = (
        (qf_ref, kf_ref, vf_ref, gcf_ref, grf_ref, gef_ref, of_ref, row >= col, row > col),
        (qb_ref, kb_ref, vb_ref, gcb_ref, grb_ref, geb_ref, ob_ref, row <= col, row < col),
    )

    def chunk_step(c, carry):
        for d, (q_ref, k_ref, v_ref, gc_ref, gr_ref, ge_ref, o_ref, m_incl, m_strict) in enumerate(dirs):
            cc = c if d == 0 else nc - 1 - c
            r0 = pl.multiple_of(cc * CHUNK, CHUNK)
            gc = gc_ref[pl.ds(r0, CHUNK), :]
            gr = gr_ref[cc]
            ge = ge_ref[cc]
            for h in range(NH):
                hd = d * NH + h
                lo = h * HD
                qh = q_ref[pl.ds(r0, CHUNK), lo:lo + HD]
                kh = k_ref[pl.ds(r0, CHUNK), lo:lo + HD]
                vh = v_ref[pl.ds(r0, CHUNK), lo:lo + HD]
                gam_c = gc[:, GL_GAMMA + hd:GL_GAMMA + hd + 1]
                beta_c = gc[:, GL_BETA + hd:GL_BETA + hd + 1]
                eg_c = gc[:, GL_EG + hd:GL_EG + hd + 1]
                egr_c = gc[:, GL_EGR + hd:GL_EGR + hd + 1]
                gam_r = gr[hd:hd + 1, :]
                dec = jnp.exp(jnp.where(m_incl, gam_c - gam_r, NEG_BIG))
                kq = lax.dot_general(jnp.concatenate([kh, qh], axis=0), kh,
                                     (((1,), (1,)), ((), ())), preferred_element_type=F32)
                n_mat = jnp.where(m_strict, kq[0:CHUNK] * (-beta_c) * dec, 0.0)
                attn = kq[CHUNK:] * dec
                t_mat = eye + n_mat
                p_mat = jnp.dot(n_mat, n_mat, preferred_element_type=F32)
                span = 2
                while span < CHUNK:
                    tp = jnp.dot(jnp.concatenate([t_mat, p_mat], axis=0), p_mat,
                                 preferred_element_type=F32)
                    t_mat = t_mat + tp[0:CHUNK]
                    p_mat = tp[CHUNK:]
                    span *= 2
                rhs = jnp.concatenate([vh * beta_c, kh * (beta_c * eg_c)], axis=1)
                uw = jnp.dot(t_mat, rhs, preferred_element_type=F32)
                st = s_ref[hd]
                wq = jnp.dot(jnp.concatenate([uw[:, HD:], qh * eg_c], axis=0), st,
                             preferred_element_type=F32)
                v_new = uw[:, 0:HD] - wq[0:CHUNK]
                o_ref[pl.ds(r0, CHUNK), lo:lo + HD] = wq[CHUNK:] + jnp.dot(
                    attn, v_new, preferred_element_type=F32)
                kd = kh * egr_c
                s_ref[hd] = st * ge[hd:hd + 1, :] + lax.dot_general(
                    kd, v_new, (((0,), (0,)), ((), ())), preferred_element_type=F32)
        return carry

    lax.fori_loop(0, nc, chunk_step, 0)


def gdn_call(reset_f, reset_b, q, k, v, gcol, grow, gend, *, tb):
    t = q.shape[0]
    nb = t // tb
    nc = tb // CHUNK
    fwd2 = lambda i, a, b: (i, 0)
    bwd2 = lambda i, a, b: (nb - 1 - i, 0)
    fwd3 = lambda i, a, b: (i, 0, 0)
    bwd3 = lambda i, a, b: (nb - 1 - i, 0, 0)

    def specs(m2, m3):
        return [
            pl.BlockSpec((tb, G_DIM), m2), pl.BlockSpec((tb, G_DIM), m2), pl.BlockSpec((tb, G_DIM), m2),
            pl.BlockSpec((tb, LANES), m2),
            pl.BlockSpec((nc, 2 * SUBLANES, CHUNK), m3),
            pl.BlockSpec((nc, 2 * SUBLANES, LANES), m3),
        ]

    grid_spec = pltpu.PrefetchScalarGridSpec(
        num_scalar_prefetch=2,
        grid=(nb,),
        in_specs=specs(fwd2, fwd3) + specs(bwd2, bwd3),
        out_specs=[pl.BlockSpec((tb, G_DIM), fwd2), pl.BlockSpec((tb, G_DIM), bwd2)],
        scratch_shapes=[pltpu.VMEM((2 * NH, HD, HD), F32)],
    )
    return pl.pallas_call(
        _gdn_kernel,
        grid_spec=grid_spec,
        out_shape=[jax.ShapeDtypeStruct((t, G_DIM), F32), jax.ShapeDtypeStruct((t, G_DIM), F32)],
        compiler_params=_cparams(("arbitrary",)),
        name="gdn",
    )(reset_f, reset_b, q, k, v, gcol, grow, gend, q, k, v, gcol, grow, gend)


def _fft1_kernel(x_ref, m1_ref, tc_ref, ts_ref, br_ref, bi_ref):
    s1 = x_ref.shape[1]
    a = jnp.dot(m1_ref[...], x_ref[0], preferred_element_type=F32)
    ar, ai = a[0:s1], a[s1:]
    tc, ts = tc_ref[...], ts_ref[...]
    br_ref[0] = ar * tc + ai * ts
    bi_ref[0] = ai * tc - ar * ts


def fft1_call(x3, m1, twc, tws, *, tcol):
    nseq, s1, cols = x3.shape
    blk = pl.BlockSpec((1, s1, tcol), lambda j, b: (b, 0, j))
    tw = pl.BlockSpec((s1, tcol), lambda j, b: (0, j))
    return pl.pallas_call(
        _fft1_kernel,
        grid=(cols // tcol, nseq),
        in_specs=[blk, pl.BlockSpec(m1.shape, lambda j, b: (0, 0)), tw, tw],
        out_specs=[blk, blk],
        out_shape=[jax.ShapeDtypeStruct(x3.shape, F32)] * 2,
        compiler_params=_cparams(("arbitrary", "arbitrary")),
        name="fft_stage1",
    )(x3, m1, twc, tws)


def _fft2_kernel(br_ref, bi_ref, mc_ref, m2_ref, o_ref):
    _, tk, s2, c = br_ref.shape
    nsplit = o_ref.shape[0]
    s2o = s2 // nsplit
    b = jnp.concatenate([br_ref[0].reshape(tk * s2, c), bi_ref[0].reshape(tk * s2, c)], axis=1)
    z = jnp.dot(b, mc_ref[...], preferred_element_type=F32)
    m2 = m2_ref[...]
    for kk in range(tk):
        zk = z[kk * s2:(kk + 1) * s2]
        x = jnp.dot(m2, jnp.concatenate([zk[:, 0:c], zk[:, c:]], axis=0),
                    preferred_element_type=F32)
        for sp in range(nsplit):
            o_ref[sp, kk] = x[sp * s2o:(sp + 1) * s2o]


def fft2_call(br4, bi4, mc, m2, *, tk, nsplit):
    nseq, s1, s2, c = br4.shape
    s2o = s2 // nsplit
    blk = pl.BlockSpec((1, tk, s2, c), lambda b, j: (b, j, 0, 0))
    return pl.pallas_call(
        _fft2_kernel,
        grid=(nseq, s1 // tk),
        in_specs=[blk, blk, pl.BlockSpec(mc.shape, lambda b, j: (0, 0)),
                  pl.BlockSpec(m2.shape, lambda b, j: (0, 0))],
        out_specs=pl.BlockSpec((nsplit, tk, s2o, c), lambda b, j: (b, j, 0, 0)),
        out_shape=jax.ShapeDtypeStruct((nseq * nsplit, s1, s2o, c), F32),
        compiler_params=_cparams(("arbitrary", "arbitrary")),
        name="fft_stage2",
    )(br4, bi4, mc, m2)


def _dft_tables(s):
    s1 = DFT1
    s2 = s // s1
    k = np.arange(s1)
    ang1 = 2.0 * np.pi * ((k[:, None] * k[None, :]) % s1) / s1
    sc = 1.0 / math.sqrt(s)
    m1 = np.concatenate([np.cos(ang1), -np.sin(ang1)], axis=0) * sc
    n2 = np.arange(s2)
    angt = 2.0 * np.pi * ((k[:, None] * n2[None, :]) % s) / s
    twc = np.repeat(np.cos(angt), F_DIM, axis=1)
    tws = np.repeat(np.sin(angt), F_DIM, axis=1)
    ang2 = 2.0 * np.pi * ((n2[:, None] * n2[None, :]) % s2) / s2
    m2 = np.concatenate([np.cos(ang2), np.sin(ang2)], axis=1)
    return (jnp.asarray(m1, F32), jnp.asarray(np.cos(angt), F32), jnp.asarray(np.sin(angt), F32),
            jnp.asarray(m2, F32))


def _channel_dft_matrix():
    g = np.arange(GROUP_DIM)
    ang = 2.0 * np.pi * ((g[:, None] * g[None, :]) % GROUP_DIM) / GROUP_DIM
    cg = np.kron(np.eye(N_GROUPS), np.cos(ang)) / math.sqrt(GROUP_DIM)
    sg = np.kron(np.eye(N_GROUPS), np.sin(ang)) / math.sqrt(GROUP_DIM)
    return jnp.asarray(np.block([[cg, -sg], [sg, cg]]), F32)


def fourier_mix_call(f, *, nseq, s, tcol, tk, nsplit):
    s1 = DFT1
    s2 = s // s1
    m1, tcs, tss, m2 = _dft_tables(s)
    twc = jnp.broadcast_to(tcs[:, :, None], (s1, s2, F_DIM)).reshape(s1, s2 * F_DIM)
    tws = jnp.broadcast_to(tss[:, :, None], (s1, s2, F_DIM)).reshape(s1, s2 * F_DIM)
    x3 = f.reshape(nseq, s1, s2 * F_DIM)
    br, bi = fft1_call(x3, m1, twc, tws, tcol=tcol)
    out = fft2_call(br.reshape(nseq, s1, s2, F_DIM), bi.reshape(nseq, s1, s2, F_DIM),
                    _channel_dft_matrix(), m2, tk=tk, nsplit=nsplit)
    return out.reshape(nseq * nsplit, s1, (s2 // nsplit) * F_DIM)


def _outproj_kernel(x_ref, mod_ref, of_ref, ob_ref, z_ref, fm_ref, gn_ref, w_ref, o_ref):
    tm = x_ref.shape[0]
    o = of_ref[...] + ob_ref[...]
    z = z_ref[...]
    gn = gn_ref[...]
    parts = []
    fm = fm_ref[...]
    parts.append(jnp.concatenate(
        [fm[:, j * F_DIM:(j + 1) * F_DIM] for j in range(tm // DFT1)], axis=0).astype(BF16))
    for h in range(NH):
        oh = o[:, h * HD:(h + 1) * HD]
        ms = jnp.mean(oh * oh, axis=-1, keepdims=True)
        y = (oh * lax.rsqrt(ms + NORM_EPS)) * gn
        parts.append((y * _silu(z[:, h * HD:(h + 1) * HD])).astype(BF16))
    mixed = jnp.concatenate(parts, axis=1)
    proj = jnp.dot(mixed, w_ref[...], preferred_element_type=F32)
    o_ref[...] = x_ref[...] + mod_ref[0, 2:3, :] * proj


def outproj_call(x, mod, o_f, o_b, z, fm2, gn, w, *, seg, tm):
    t, d = x.shape
    row = lambda i: (i, 0)
    const = lambda i: (0, 0)
    per_seg = seg // tm
    return pl.pallas_call(
        _outproj_kernel,
        grid=(t // tm,),
        in_specs=[
            pl.BlockSpec((tm, d), row),
            pl.BlockSpec((1, SUBLANES, d), lambda i: (i // per_seg, 0, 0)),
            pl.BlockSpec((tm, G_DIM), row),
            pl.BlockSpec((tm, G_DIM), row),
            pl.BlockSpec((tm, G_DIM), row),
            pl.BlockSpec((DFT1, (tm // DFT1) * F_DIM), lambda i: (i // per_seg, i % per_seg)),
            pl.BlockSpec((1, HD), const),
            pl.BlockSpec(w.shape, const),
        ],
        out_specs=pl.BlockSpec((tm, d), row),
        out_shape=jax.ShapeDtypeStruct((t, d), F32),
        compiler_params=_cparams(("arbitrary",)),
        name="outproj",
    )(x, mod, o_f, o_b, z, fm2, gn, w)


def _ffn_kernel(x_ref, mod_ref, nw_ref, wg_ref, wu_ref, wd_ref, o_ref, *, nsplit):
    x = x_ref[...]
    h = _mod_norm(x, nw_ref[...], mod_ref[0, 4:5, :], mod_ref[0, 3:4, :]).astype(BF16)
    dff = wg_ref.shape[1]
    cw = dff // nsplit
    acc = None
    for c in range(nsplit):
        g = jnp.dot(h, wg_ref[:, c * cw:(c + 1) * cw], preferred_element_type=F32)
        u = jnp.dot(h, wu_ref[:, c * cw:(c + 1) * cw], preferred_element_type=F32)
        hid = (_silu(g) * u).astype(BF16)
        part = jnp.dot(hid, wd_ref[c * cw:(c + 1) * cw, :], preferred_element_type=F32)
        acc = part if acc is None else acc + part
    o_ref[...] = x + mod_ref[0, 5:6, :] * acc


def ffn_call(x, mod, nw, wg, wu, wd, *, seg, tm):
    t, d = x.shape
    row = lambda i: (i, 0)
    const = lambda i: (0, 0)
    single = pl.Buffered(1)
    return pl.pallas_call(
        functools.partial(_ffn_kernel, nsplit=2),
        grid=(t // tm,),
        in_specs=[
            pl.BlockSpec((tm, d), row),
            pl.BlockSpec((1, SUBLANES, d), lambda i: (i * tm // seg, 0, 0)),
            pl.BlockSpec((1, d), const),
            pl.BlockSpec(wg.shape, const, pipeline_mode=single),
            pl.BlockSpec(wu.shape, const, pipeline_mode=single),
            pl.BlockSpec(wd.shape, const, pipeline_mode=single),
        ],
        out_specs=pl.BlockSpec((tm, d), row),
        out_shape=jax.ShapeDtypeStruct((t, d), F32),
        compiler_params=_cparams(("arbitrary",)),
        name="ffn",
    )(x, mod, nw, wg, wu, wd)


def _router_kernel(x_ref, mod_ref, nw_ref, wr_ref, tri_ref, oi_ref, op_ref, cnt_ref, carry_ref):
    i = pl.program_id(0)
    tm = x_ref.shape[0]

    @pl.when(i == 0)
    def _():
        carry_ref[...] = jnp.zeros_like(carry_ref)

    h = _mod_norm(x_ref[...], nw_ref[...], mod_ref[0, 4:5, :], mod_ref[0, 3:4, :])
    logits = jnp.dot(h, wr_ref[...], precision=lax.Precision.HIGHEST, preferred_element_type=F32)
    lane = lax.broadcasted_iota(jnp.int32, (tm, LANES), 1)
    logits = jnp.where(lane < N_EXPERTS, logits, NEG_BIG)
    l1 = jnp.max(logits, axis=-1, keepdims=True)
    i1 = jnp.min(jnp.where(logits == l1, lane, LANES), axis=-1, keepdims=True)
    rest = jnp.where(lane == i1, NEG_BIG, logits)
    l2 = jnp.max(rest, axis=-1, keepdims=True)
    i2 = jnp.min(jnp.where(rest == l2, lane, LANES), axis=-1, keepdims=True)
    e21 = jnp.exp(l2 - l1)
    p1 = 1.0 / (1.0 + e21)
    p2 = e21 * p1
    oh1 = lane == i1
    oh2 = lane == i2
    oh = jnp.where(oh1 | oh2, 1.0, 0.0).astype(BF16)
    before = jnp.dot(tri_ref[...], oh, preferred_element_type=F32) + carry_ref[0:1, :]
    r1 = jnp.sum(jnp.where(oh1, before, 0.0), axis=-1, keepdims=True).astype(jnp.int32)
    r2 = jnp.sum(jnp.where(oh2, before, 0.0), axis=-1, keepdims=True).astype(jnp.int32)
    oi_ref[...] = jnp.where(lane == 0, i1, jnp.where(lane == 1, i2, jnp.where(lane == 2, r1, r2)))
    op_ref[...] = jnp.where(lane == 0, p1, p2)
    new_carry = carry_ref[0:1, :] + jnp.sum(oh.astype(F32), axis=0, keepdims=True)
    carry_ref[...] = jnp.broadcast_to(new_carry, carry_ref.shape)
    cnt_ref[...] = jnp.broadcast_to(new_carry, cnt_ref.shape)


def router_call(x, mod, nw, wr, tri, *, seg, tm):
    t, d = x.shape
    row = lambda i: (i, 0)
    const = lambda i: (0, 0)
    return pl.pallas_call(
        _router_kernel,
        grid=(t // tm,),
        in_specs=[
            pl.BlockSpec((tm, d), row),
            pl.BlockSpec((1, SUBLANES, d), lambda i: (i * tm // seg, 0, 0)),
            pl.BlockSpec((1, d), const),
            pl.BlockSpec(wr.shape, const),
            pl.BlockSpec(tri.shape, const),
        ],
        out_specs=[pl.BlockSpec((tm, LANES), row), pl.BlockSpec((tm, LANES), row),
                   pl.BlockSpec((SUBLANES, LANES), const)],
        out_shape=[jax.ShapeDtypeStruct((t, LANES), jnp.int32), jax.ShapeDtypeStruct((t, LANES), F32),
                   jax.ShapeDtypeStruct((SUBLANES, LANES), F32)],
        scratch_shapes=[pltpu.VMEM((SUBLANES, LANES), F32)],
        compiler_params=_cparams(("arbitrary",)),
        name="router",
    )(x, mod, nw, wr, tri)


def _dispatch_kernel(pos_ref, x_ref, mod_ref, nw_ref, xs_in_ref, xs_ref, hbuf, sem):
    del xs_in_ref
    tm = x_ref.shape[0]
    hbuf[...] = _mod_norm(x_ref[...], nw_ref[...], mod_ref[0, 4:5, :], mod_ref[0, 3:4, :])

    def row_copy(r, slot):
        return pltpu.make_async_copy(hbuf.at[pl.ds(r, 1), :],
                                     xs_ref.at[pl.ds(pos_ref[0, 0, slot * tm + r], 1), :], sem)

    def start(r, c):
        row_copy(r, 0).start()
        row_copy(r, 1).start()
        return c

    def wait(r, c):
        row_copy(r, 0).wait()
        row_copy(r, 1).wait()
        return c

    lax.fori_loop(0, tm, start, 0)
    lax.fori_loop(0, tm, wait, 0)


def dispatch_call(pos3, x, mod, nw, xs_zero, *, seg, tm):
    t, d = x.shape
    return pl.pallas_call(
        _dispatch_kernel,
        grid=(t // tm,),
        in_specs=[
            pl.BlockSpec((1, 1, 2 * tm), lambda i: (i, 0, 0), memory_space=pltpu.SMEM),
            pl.BlockSpec((tm, d), lambda i: (i, 0)),
            pl.BlockSpec((1, SUBLANES, d), lambda i: (i * tm // seg, 0, 0)),
            pl.BlockSpec((1, d), lambda i: (0, 0)),
            pl.BlockSpec(memory_space=pl.ANY),
        ],
        out_specs=pl.BlockSpec(memory_space=pl.ANY),
        out_shape=jax.ShapeDtypeStruct(xs_zero.shape, F32),
        scratch_shapes=[pltpu.VMEM((tm, d), F32), pltpu.SemaphoreType.DMA(())],
        input_output_aliases={4: 0},
        compiler_params=_cparams(("arbitrary",)),
        name="moe_dispatch",
    )(pos3, x, mod, nw, xs_zero)


def _expert_kernel(te_ref, nu_ref, xs_ref, wg_ref, wu_ref, wd_ref, y_ref, acc_ref):
    j = pl.program_id(0)
    half = pl.program_id(1)

    @pl.when(j < nu_ref[0])
    def _():
        h = xs_ref[...].astype(BF16)
        g = jnp.dot(h, wg_ref[0], preferred_element_type=F32)
        u = jnp.dot(h, wu_ref[0], preferred_element_type=F32)
        hid = (_silu(g) * u).astype(BF16)
        part = jnp.dot(hid, wd_ref[0], preferred_element_type=F32)

        @pl.when(half == 0)
        def _():
            acc_ref[...] = part

        @pl.when(half == 1)
        def _():
            y_ref[...] = acc_ref[...] + part

    @pl.when((j >= nu_ref[0]) & (half == 1))
    def _():
        y_ref[...] = jnp.zeros_like(y_ref)


def expert_call(tile_expert, n_used, xs, wg, wu, wd, *, tme):
    nr, d = xs.shape
    ntiles = nr // tme
    fh = wg.shape[2] // 2

    def jj(j, nu):
        return jnp.minimum(j, nu[0] - 1)

    def hh(j, hf, nu):
        return jnp.where(j < nu[0], hf, 1)

    grid_spec = pltpu.PrefetchScalarGridSpec(
        num_scalar_prefetch=2,
        grid=(ntiles, 2),
        in_specs=[
            pl.BlockSpec((tme, d), lambda j, hf, te, nu: (jj(j, nu), 0)),
            pl.BlockSpec((1, d, fh), lambda j, hf, te, nu: (te[jj(j, nu)], 0, hh(j, hf, nu))),
            pl.BlockSpec((1, d, fh), lambda j, hf, te, nu: (te[jj(j, nu)], 0, hh(j, hf, nu))),
            pl.BlockSpec((1, fh, d), lambda j, hf, te, nu: (te[jj(j, nu)], hh(j, hf, nu), 0)),
        ],
        out_specs=pl.BlockSpec((tme, d), lambda j, hf, te, nu: (j, 0)),
        scratch_shapes=[pltpu.VMEM((tme, d), F32)],
    )
    return pl.pallas_call(
        _expert_kernel,
        grid_spec=grid_spec,
        out_shape=jax.ShapeDtypeStruct((nr, d), F32),
        compiler_params=_cparams(("arbitrary", "arbitrary")),
        name="moe_experts",
    )(tile_expert, n_used, xs, wg, wu, wd)


def _combine_kernel(pos_ref, x_ref, mod_ref, p_ref, y_ref, o_ref, ybuf, sem):
    tm = x_ref.shape[0]

    def row_copy(r, slot):
        return pltpu.make_async_copy(y_ref.at[pl.ds(pos_ref[0, 0, slot * tm + r], 1), :],
                                     ybuf.at[slot, pl.ds(r, 1), :], sem)

    def start(r, c):
        row_copy(r, 0).start()
        row_copy(r, 1).start()
        return c

    def wait(r, c):
        row_copy(r, 0).wait()
        row_copy(r, 1).wait()
        return c

    lax.fori_loop(0, tm, start, 0)
    lax.fori_loop(0, tm, wait, 0)
    p = p_ref[...]
    f = p[:, 0:1] * ybuf[0] + p[:, 1:2] * ybuf[1]
    o_ref[...] = x_ref[...] + mod_ref[0, 5:6, :] * f


def combine_call(pos3, x, mod, p, y, *, seg, tm):
    t, d = x.shape
    return pl.pallas_call(
        _combine_kernel,
        grid=(t // tm,),
        in_specs=[
            pl.BlockSpec((1, 1, 2 * tm), lambda i: (i, 0, 0), memory_space=pltpu.SMEM),
            pl.BlockSpec((tm, d), lambda i: (i, 0)),
            pl.BlockSpec((1, SUBLANES, d), lambda i: (i * tm // seg, 0, 0)),
            pl.BlockSpec((tm, LANES), lambda i: (i, 0)),
            pl.BlockSpec(memory_space=pl.ANY),
        ],
        out_specs=pl.BlockSpec((tm, d), lambda i: (i, 0)),
        out_shape=jax.ShapeDtypeStruct((t, d), F32),
        scratch_shapes=[pltpu.VMEM((2, tm, d), F32), pltpu.SemaphoreType.DMA(())],
        compiler_params=_cparams(("arbitrary",)),
        name="moe_combine",
    )(pos3, x, mod, p, y)


def moe_block(x, mod, nw, wr, wg, wu, wd, *, seg, tm, tmd, tme):
    t, d = x.shape
    wr_pad = jnp.zeros((d, LANES), F32).at[:, :N_EXPERTS].set(wr)
    tri = jnp.asarray(np.tril(np.ones((tm, tm), np.float32), k=-1), BF16)
    oi, op, cnt = router_call(x, mod, nw, wr_pad, tri, seg=seg, tm=tm)
    counts = cnt[0, :N_EXPERTS].astype(jnp.int32)
    padded = ((counts + tme - 1) // tme) * tme
    ends = jnp.cumsum(padded)
    starts = ends - padded
    nr = ((2 * t + N_EXPERTS * (tme - 1)) // tme) * tme
    ntiles = nr // tme
    n_used = (ends[-1] // tme).astype(jnp.int32).reshape(1)
    tile_start = jnp.arange(ntiles, dtype=jnp.int32) * tme
    tile_expert = jnp.minimum(jnp.sum(tile_start[:, None] >= ends[None, :], axis=1),
                              N_EXPERTS - 1).astype(jnp.int32)
    pos1 = starts[oi[:, 0]] + oi[:, 2]
    pos2 = starts[oi[:, 1]] + oi[:, 3]
    pos3 = jnp.concatenate([pos1.reshape(t // tmd, 1, tmd), pos2.reshape(t // tmd, 1, tmd)], axis=2)
    xs = dispatch_call(pos3, x, mod, nw, jnp.zeros((nr, d), F32), seg=seg, tm=tmd)
    y = expert_call(tile_expert, n_used, xs, wg, wu, wd, tme=tme)
    return combine_call(pos3, x, mod, op, y, seg=seg, tm=tmd)


def _final_kernel(x_ref, w_ref, o_ref):
    x = x_ref[...]
    ms = jnp.mean(x * x, axis=-1, keepdims=True)
    o_ref[...] = (x * lax.rsqrt(ms + NORM_EPS)) * w_ref[...]


def final_call(x, w, *, row0, nrows, tm):
    d = x.shape[1]
    off = row0 // tm
    return pl.pallas_call(
        _final_kernel,
        grid=(nrows // tm,),
        in_specs=[pl.BlockSpec((tm, d), lambda i: (i + off, 0)), pl.BlockSpec((1, d), lambda i: (0, 0))],
        out_specs=pl.BlockSpec((tm, d), lambda i: (i, 0)),
        out_shape=jax.ShapeDtypeStruct((nrows, d), F32),
        compiler_params=_cparams(("arbitrary",)),
        name="final_norm",
    )(x, w)


def _segment_flags(prompt_segs, sample_segs, seg, tile):
    per_seg = seg // tile
    nseg = prompt_segs + sample_segs
    first = np.zeros(nseg * per_seg, np.int32)
    last = np.zeros(nseg * per_seg, np.int32)
    first[0] = 1
    last[prompt_segs * per_seg - 1] = 1
    for s in range(prompt_segs, nseg):
        first[s * per_seg] = 1
        last[(s + 1) * per_seg - 1] = 1
    return first, last


def encoder_pair(x_prompt, x_sample, c_prompt, c_sample, w_ada, b_ada, norm_mix, norm_ffn, w_in, conv_w,
                 a_log, dt_bias, gdn_norm, w_out, w_ffn_gate, w_ffn_up, w_ffn_down, w_router,
                 w_exp_gate, w_exp_up, w_exp_down, norm_final, *, tm=512, tb=256, tmd=256, tme=512):
    bp, sp, d = x_prompt.shape
    bs, seg, _ = x_sample.shape
    assert bp == 1 and sp % seg == 0 and seg % tm == 0 and tm % DFT1 == 0
    depth = w_ada.shape[0]
    prompt_segs = sp // seg
    nseg = prompt_segs + bs
    t = nseg * seg
    x = jnp.concatenate([x_prompt.reshape(sp, d), x_sample.reshape(bs * seg, d)], axis=0)

    nrow = -(-(1 + bs) // SUBLANES) * SUBLANES
    c_all = jnp.zeros((nrow, d), F32).at[0:1].set(c_prompt).at[1:1 + bs].set(c_sample)
    ada = ada_call(c_all, w_ada, b_ada).reshape(depth, nrow, 6, d)
    seg_row = np.concatenate([np.zeros(prompt_segs, np.int32), 1 + np.arange(bs, dtype=np.int32)])
    mod_all = jnp.pad(ada[:, seg_row], ((0, 0), (0, 0), (0, SUBLANES - 6), (0, 0)))

    first_c, last_c = _segment_flags(prompt_segs, bs, seg, tm)
    first_g, last_g = _segment_flags(prompt_segs, bs, seg, tb)
    reset_f = jnp.asarray(first_g)
    reset_b = jnp.asarray(last_g[::-1].copy())
    first_c, last_c = jnp.asarray(first_c), jnp.asarray(last_c)

    off_f, off_qkv, off_z = F_DIM, F_DIM + QKV_DIM, F_DIM + QKV_DIM + G_DIM
    off_b = off_z + 2 * NH
    for l in range(depth):
        mod = mod_all[l]
        wl = w_in[l]
        wf = wl[:, :off_f].astype(BF16)
        wqkv = wl[:, off_f:off_qkv].astype(BF16)
        wz = wl[:, off_qkv:off_z].astype(BF16)
        wba = jnp.pad(wl[:, off_z:], ((0, 0), (0, LANES - 4 * NH))).astype(BF16)
        f, qkv, z, ba = inproj_call(x, mod, norm_mix[l].reshape(1, d), wf, wqkv, wz, wba, seg=seg, tm=tm)

        cw = jnp.pad(conv_w[l], ((0, SUBLANES - CONV_K), (0, 0)))
        gp = jnp.zeros((SUBLANES, LANES), F32)
        gp = gp.at[0, 2 * NH:4 * NH].set(dt_bias[l].reshape(-1)).at[1, 2 * NH:4 * NH].set(a_log[l].reshape(-1))
        q, k, v, gcol, grow, gend = conv_call(first_c, last_c, qkv, cw, ba, gp, tm=tm)
        o_f, o_b = gdn_call(reset_f, reset_b, q, k, v, gcol, grow, gend, tb=tb)

        s2p = sp // DFT1
        fm_p = fourier_mix_call(f[:sp], nseq=1, s=sp, tcol=min(2048, s2p * F_DIM),
                                tk=min(16, DFT1), nsplit=prompt_segs)
        s2s = seg // DFT1
        fm_s = fourier_mix_call(f[sp:], nseq=bs, s=seg, tcol=min(2048, s2s * F_DIM),
                                tk=min(64, DFT1), nsplit=1)
        fm2 = jnp.concatenate([fm_p, fm_s], axis=0).reshape(nseg * DFT1, (seg // DFT1) * F_DIM)

        x = outproj_call(x, mod, o_f, o_b, z, fm2, gdn_norm[l].reshape(1, HD), w_out[l].astype(BF16),
                         seg=seg, tm=tm)
        nw = norm_ffn[l].reshape(1, d)
        if l % 2 == 0:
            i = l // 2
            x = ffn_call(x, mod, nw, w_ffn_gate[i].astype(BF16), w_ffn_up[i].astype(BF16),
                         w_ffn_down[i].astype(BF16), seg=seg, tm=tm)
        else:
            i = l // 2
            x = moe_block(x, mod, nw, w_router[i], w_exp_gate[i].astype(BF16), w_exp_up[i].astype(BF16),
                          w_exp_down[i].astype(BF16), seg=seg, tm=tm, tmd=tmd, tme=tme)

    wn = norm_final.reshape(1, d)
    y_p = final_call(x, wn, row0=0, nrows=sp, tm=tm).reshape(bp, sp, d)
    y_s = final_call(x, wn, row0=sp, nrows=bs * seg, tm=tm).reshape(bs, seg, d)
    return y_p, y_s


def kernel(x_prompt, x_sample, c_prompt, c_sample, w_ada, b_ada, norm_mix, norm_ffn, w_in, conv_w, a_log,
           dt_bias, gdn_norm, w_out, w_ffn_gate, w_ffn_up, w_ffn_down, w_router, w_exp_gate, w_exp_up,
           w_exp_down, norm_final):
    return encoder_pair(x_prompt, x_sample, c_prompt, c_sample, w_ada, b_ada, norm_mix, norm_ffn, w_in,
                        conv_w, a_log, dt_bias, gdn_norm, w_out, w_ffn_gate, w_ffn_up, w_ffn_down,
                        w_router, w_exp_gate, w_exp_up, w_exp_down, norm_final)
```

```python
import functools
import math

import numpy as np
import jax
import jax.numpy as jnp
from jax import lax
from jax.experimental import pallas as pl
from jax.experimental.pallas import tpu as pltpu

F32 = jnp.float32
BF16 = jnp.bfloat16

D_MODEL = 1024
DEPTH = 4
N_GROUPS = 4
GROUP_DIM = 64
F_DIM = N_GROUPS * GROUP_DIM
HD = 128
NH = 6
G_DIM = NH * HD
QKV_DIM = 3 * G_DIM
CONV_K = 5
CHUNK = 64
D_FF = 2816
N_EXPERTS = 8
D_EXPERT = 3584
NORM_EPS = 1e-6

LANES = 128
SUBLANES = 8
VMEM_LIMIT = 56 * 1024 * 1024
DFT1 = 128

GL_GAMMA, GL_BETA, GL_EG, GL_EGR = 0, 16, 32, 48
NEG_BIG = -1e30


def _cparams(sem):
    return pltpu.CompilerParams(dimension_semantics=sem, vmem_limit_bytes=VMEM_LIMIT)


def _mod_norm(x, nw, sc, sh):
    ms = jnp.mean(x * x, axis=-1, keepdims=True)
    y = x * lax.rsqrt(ms + NORM_EPS)
    return (y * nw) * (1.0 + sc) + sh


def _silu(x):
    return x * (1.0 / (1.0 + jnp.exp(-x)))


def _ada_kernel(c_ref, w_ref, b_ref, o_ref):
    c = _silu(c_ref[...])
    o_ref[0] = jnp.dot(c, w_ref[0], precision=lax.Precision.HIGHEST,
                       preferred_element_type=F32) + b_ref[0]


def ada_call(c_all, w_ada, b_ada):
    nrow = c_all.shape[0]
    depth, d, d6 = w_ada.shape
    tn = 1024
    return pl.pallas_call(
        _ada_kernel,
        grid=(depth, d6 // tn),
        in_specs=[
            pl.BlockSpec((nrow, d), lambda l, j: (0, 0)),
            pl.BlockSpec((1, d, tn), lambda l, j: (l, 0, j)),
            pl.BlockSpec((1, 1, tn), lambda l, j: (l, 0, j)),
        ],
        out_specs=pl.BlockSpec((1, nrow, tn), lambda l, j: (l, 0, j)),
        out_shape=jax.ShapeDtypeStruct((depth, nrow, d6), F32),
        compiler_params=_cparams(("arbitrary", "arbitrary")),
        name="ada",
    )(c_all, w_ada, b_ada.reshape(depth, 1, d6))


def _inproj_kernel(x_ref, mod_ref, nw_ref, wf_ref, wqkv_ref, wz_ref, wba_ref,
                   f_ref, qkv_ref, z_ref, ba_ref):
    h = _mod_norm(x_ref[...], nw_ref[...], mod_ref[0, 1:2, :], mod_ref[0, 0:1, :]).astype(BF16)
    f_ref[...] = jnp.dot(h, wf_ref[...], preferred_element_type=F32)
    qkv_ref[...] = jnp.dot(h, wqkv_ref[...], preferred_element_type=F32)
    z_ref[...] = jnp.dot(h, wz_ref[...], preferred_element_type=F32)
    ba_ref[...] = jnp.dot(h, wba_ref[...], preferred_element_type=F32)


def inproj_call(x, mod, nw, wf, wqkv, wz, wba, *, seg, tm):
    t, d = x.shape
    const = lambda i: (0, 0)
    row = lambda i: (i, 0)
    return pl.pallas_call(
        _inproj_kernel,
        grid=(t // tm,),
        in_specs=[
            pl.BlockSpec((tm, d), row),
            pl.BlockSpec((1, SUBLANES, d), lambda i: (i * tm // seg, 0, 0)),
            pl.BlockSpec((1, d), const),
            pl.BlockSpec(wf.shape, const),
            pl.BlockSpec(wqkv.shape, const),
            pl.BlockSpec(wz.shape, const),
            pl.BlockSpec(wba.shape, const),
        ],
        out_specs=[
            pl.BlockSpec((tm, F_DIM), row),
            pl.BlockSpec((tm, QKV_DIM), row),
            pl.BlockSpec((tm, G_DIM), row),
            pl.BlockSpec((tm, LANES), row),
        ],
        out_shape=[
            jax.ShapeDtypeStruct((t, F_DIM), F32),
            jax.ShapeDtypeStruct((t, QKV_DIM), F32),
            jax.ShapeDtypeStruct((t, G_DIM), F32),
            jax.ShapeDtypeStruct((t, LANES), F32),
        ],
        compiler_params=_cparams(("arbitrary",)),
        name="inproj",
    )(x, mod, nw, wf, wqkv, wz, wba)


def _conv_kernel(first_ref, last_ref,
                 qkv_ref, prev_ref, next_ref, cw_ref, ba_ref, gp_ref,
                 q_ref, k_ref, v_ref, gcol_ref, grow_ref, gend_ref,
                 xe_ref):
    i = pl.program_id(0)
    tm = qkv_ref.shape[0]
    nc = tm // CHUNK
    halo = SUBLANES
    pm = jnp.where(first_ref[i] == 1, 0.0, 1.0)
    nm = jnp.where(last_ref[i] == 1, 0.0, 1.0)
    xe_ref[0:halo, :] = prev_ref[...] * pm
    xe_ref[halo:halo + tm, :] = qkv_ref[...]
    xe_ref[halo + tm:, :] = next_ref[...] * nm

    outs = (q_ref, k_ref, v_ref)
    for s in range(3 * NH):
        lo = s * HD
        acc = None
        for j in range(CONV_K):
            r0 = halo - CONV_K // 2 + j
            term = xe_ref[r0:r0 + tm, lo:lo + HD] * cw_ref[j:j + 1, lo:lo + HD]
            acc = term if acc is None else acc + term
        y = _silu(acc)
        which, h = divmod(s, NH)
        if which < 2:
            y = y * lax.rsqrt(jnp.sum(y * y, axis=-1, keepdims=True) + 1e-6)
        if which == 0:
            y = y * (HD ** -0.5)
        outs[which][:, h * HD:(h + 1) * HD] = y

    ba = ba_ref[...]
    lane = lax.broadcasted_iota(jnp.int32, (tm, LANES), 1)
    beta = 1.0 / (1.0 + jnp.exp(-ba))
    xs = ba + gp_ref[0:1, :]
    softplus = jnp.maximum(xs, 0.0) + jnp.log(1.0 + jnp.exp(-jnp.abs(xs)))
    g = -jnp.exp(gp_ref[1:2, :]) * softplus
    g = jnp.where((lane >= 2 * NH) & (lane < 4 * NH), g, 0.0)
    g = pltpu.roll(g, LANES - 2 * NH, axis=1)

    rowc = lax.broadcasted_iota(jnp.int32, (tm, LANES), 0) % CHUNK
    p = g
    sh = 1
    while sh < CHUNK:
        p = p + jnp.where(rowc >= sh, pltpu.roll(p, sh, axis=0), 0.0)
        sh *= 2
    g3 = g.reshape(nc, CHUNK, LANES)
    tot = jnp.broadcast_to(jnp.sum(g3, axis=1, keepdims=True), (nc, CHUNK, LANES)).reshape(tm, LANES)
    is_bwd = (lane >= NH) & (lane < 2 * NH)
    gamma = jnp.where(is_bwd, tot - p + g, p)
    eg = jnp.exp(gamma)
    egr = jnp.exp(tot - gamma)
    m12 = lane < 2 * NH
    gcol = (jnp.where(m12, gamma, 0.0)
            + pltpu.roll(jnp.where(m12, beta, 0.0), GL_BETA, axis=1)
            + pltpu.roll(jnp.where(m12, eg, 0.0), GL_EG, axis=1)
            + pltpu.roll(jnp.where(m12, egr, 0.0), GL_EGR, axis=1))
    gcol_ref[...] = gcol
    gam_t = jnp.where(m12, gamma, 0.0).T[0:2 * SUBLANES, :]
    g_t = g.T[0:2 * SUBLANES, :]
    for c in range(nc):
        grow_ref[c] = gam_t[:, c * CHUNK:(c + 1) * CHUNK]
        tc = jnp.sum(g_t[:, c * CHUNK:(c + 1) * CHUNK], axis=-1, keepdims=True)
        gend_ref[c] = jnp.exp(jnp.broadcast_to(tc, (2 * SUBLANES, LANES)))


def conv_call(first, last, qkv, cw, ba, gp, *, tm):
    t = qkv.shape[0]
    nblk8 = t // SUBLANES
    r8 = tm // SUBLANES
    nc = tm // CHUNK
    grid_spec = pltpu.PrefetchScalarGridSpec(
        num_scalar_prefetch=2,
        grid=(t // tm,),
        in_specs=[
            pl.BlockSpec((tm, QKV_DIM), lambda i, f, l: (i, 0)),
            pl.BlockSpec((SUBLANES, QKV_DIM), lambda i, f, l: (jnp.maximum(i * r8 - 1, 0), 0)),
            pl.BlockSpec((SUBLANES, QKV_DIM), lambda i, f, l: (jnp.minimum((i + 1) * r8, nblk8 - 1), 0)),
            pl.BlockSpec((SUBLANES, QKV_DIM), lambda i, f, l: (0, 0)),
            pl.BlockSpec((tm, LANES), lambda i, f, l: (i, 0)),
            pl.BlockSpec((SUBLANES, LANES), lambda i, f, l: (0, 0)),
        ],
        out_specs=[
            pl.BlockSpec((tm, G_DIM), lambda i, f, l: (i, 0)),
            pl.BlockSpec((tm, G_DIM), lambda i, f, l: (i, 0)),
            pl.BlockSpec((tm, G_DIM), lambda i, f, l: (i, 0)),
            pl.BlockSpec((tm, LANES), lambda i, f, l: (i, 0)),
            pl.BlockSpec((nc, 2 * SUBLANES, CHUNK), lambda i, f, l: (i, 0, 0)),
            pl.BlockSpec((nc, 2 * SUBLANES, LANES), lambda i, f, l: (i, 0, 0)),
        ],
        scratch_shapes=[pltpu.VMEM((tm + 2 * SUBLANES, QKV_DIM), F32)],
    )
    return pl.pallas_call(
        _conv_kernel,
        grid_spec=grid_spec,
        out_shape=[
            jax.ShapeDtypeStruct((t, G_DIM), F32),
            jax.ShapeDtypeStruct((t, G_DIM), F32),
            jax.ShapeDtypeStruct((t, G_DIM), F32),
            jax.ShapeDtypeStruct((t, LANES), F32),
            jax.ShapeDtypeStruct((t // CHUNK, 2 * SUBLANES, CHUNK), F32),
            jax.ShapeDtypeStruct((t // CHUNK, 2 * SUBLANES, LANES), F32),
        ],
        compiler_params=_cparams(("arbitrary",)),
        name="conv_gates",
    )(first, last, qkv, qkv, qkv, cw, ba, gp)


def _gdn_kernel(rf_ref, rb_ref,
                qf_ref, kf_ref, vf_ref, gcf_ref, grf_ref, gef_ref,
                qb_ref, kb_ref, vb_ref, gcb_ref, grb_ref, geb_ref,
                of_ref, ob_ref, s_ref, wq_ref, ab_ref, u_ref):
    i = pl.program_id(0)
    tb = qf_ref.shape[0]
    nc = tb // CHUNK

    @pl.when(rf_ref[i] == 1)
    def _():
        s_ref[0:NH] = jnp.zeros((NH, HD, HD), F32)

    @pl.when(rb_ref[i] == 1)
    def _():
        s_ref[NH:2 * NH] = jnp.zeros((NH, HD, HD), F32)

    row = lax.broadcasted_iota(jnp.int32, (CHUNK, CHUNK), 0)
    col = lax.broadcasted_iota(jnp.int32, (CHUNK, CHUNK), 1)
    eye = jnp.where(row == col, 1.0, 0.0).astype(F32)
    dirs = (
        (qf_ref, kf_ref, vf_ref, gcf_ref, grf_ref, gef_ref, of_ref, row >= col, row > col),
        (qb_ref, kb_ref, vb_ref, gcb_ref, grb_ref, geb_ref, ob_ref, row <= col, row < col),
    )

    hds = [(d, h) for d in range(2) for h in range(NH)]
    nhd = len(hds)

    def prep_step(c, carry):
        r0 = pl.multiple_of(c * CHUNK, CHUNK)
        qs, ks, vs, cols, decs = [], [], [], [], []
        for d, h in hds:
            q_ref, k_ref, v_ref, gc_ref, gr_ref = dirs[d][0:5]
            hd = d * NH + h
            lo = h * HD
            gc = gc_ref[pl.ds(r0, CHUNK), :]
            qs.append(q_ref[pl.ds(r0, CHUNK), lo:lo + HD])
            ks.append(k_ref[pl.ds(r0, CHUNK), lo:lo + HD])
            vs.append(v_ref[pl.ds(r0, CHUNK), lo:lo + HD])
            gam_c = gc[:, GL_GAMMA + hd:GL_GAMMA + hd + 1]
            cols.append((gc[:, GL_BETA + hd:GL_BETA + hd + 1], gc[:, GL_EG + hd:GL_EG + hd + 1],
                         gc[:, GL_EGR + hd:GL_EGR + hd + 1]))
            gam_r = gr_ref[c][hd:hd + 1, :]
            decs.append(jnp.exp(jnp.where(dirs[d][7], gam_c - gam_r, NEG_BIG)))
        kqs = [lax.dot_general(jnp.concatenate([ks[n], qs[n]], axis=0), ks[n],
                               (((1,), (1,)), ((), ())), preferred_element_type=F32)
               for n in range(nhd)]
        ns = [jnp.where(dirs[d][8], kqs[n][0:CHUNK] * (-cols[n][0]) * decs[n], 0.0)
              for n, (d, h) in enumerate(hds)]
        for n, (d, h) in enumerate(hds):
            ab_ref[d, c, h, 0:CHUNK, :] = kqs[n][CHUNK:] * decs[n]
            ab_ref[d, c, h, CHUNK:, :] = (ks[n] * cols[n][2]).T
        ts = [eye + ns[n] for n in range(nhd)]
        ps = [jnp.dot(ns[n], ns[n], preferred_element_type=F32) for n in range(nhd)]
        span = 2
        while span < CHUNK:
            tps = [jnp.dot(jnp.concatenate([ts[n], ps[n]], axis=0), ps[n], preferred_element_type=F32)
                   for n in range(nhd)]
            ts = [ts[n] + tps[n][0:CHUNK] for n in range(nhd)]
            ps = [tps[n][CHUNK:] for n in range(nhd)]
            span *= 2
        uws = [jnp.dot(ts[n], jnp.concatenate([vs[n] * cols[n][0], ks[n] * (cols[n][0] * cols[n][1])], axis=1),
                       preferred_element_type=F32) for n in range(nhd)]
        for n, (d, h) in enumerate(hds):
            u_ref[d, c, h] = uws[n][:, 0:HD]
            wq_ref[d, c, h, 0:CHUNK, :] = uws[n][:, HD:]
            wq_ref[d, c, h, CHUNK:, :] = qs[n] * cols[n][1]
        return carry

    lax.fori_loop(0, nc, prep_step, 0)

    def scan_step(c, carry):
        ccs = (c, nc - 1 - c)
        sts = [s_ref[d * NH + h] for d, h in hds]
        wqs = [jnp.dot(wq_ref[d, ccs[d], h], sts[n], preferred_element_type=F32)
               for n, (d, h) in enumerate(hds)]
        vns = [u_ref[d, ccs[d], h] - wqs[n][0:CHUNK] for n, (d, h) in enumerate(hds)]
        avs = [jnp.dot(ab_ref[d, ccs[d], h], vns[n], preferred_element_type=F32)
               for n, (d, h) in enumerate(hds)]
        for n, (d, h) in enumerate(hds):
            r0 = pl.multiple_of(ccs[d] * CHUNK, CHUNK)
            dirs[d][6][pl.ds(r0, CHUNK), h * HD:(h + 1) * HD] = wqs[n][CHUNK:] + avs[n][0:CHUNK]
            ge = dirs[d][5][ccs[d]]
            hd = d * NH + h
            s_ref[hd] = sts[n] * ge[hd:hd + 1, :] + avs[n][CHUNK:]
        return carry

    lax.fori_loop(0, nc, scan_step, 0)


def gdn_call(reset_f, reset_b, q, k, v, gcol, grow, gend, *, tb):
    t = q.shape[0]
    nb = t // tb
    nc = tb // CHUNK
    fwd2 = lambda i, a, b: (i, 0)
    bwd2 = lambda i, a, b: (nb - 1 - i, 0)
    fwd3 = lambda i, a, b: (i, 0, 0)
    bwd3 = lambda i, a, b: (nb - 1 - i, 0, 0)

    def specs(m2, m3):
        return [
            pl.BlockSpec((tb, G_DIM), m2), pl.BlockSpec((tb, G_DIM), m2), pl.BlockSpec((tb, G_DIM), m2),
            pl.BlockSpec((tb, LANES), m2),
            pl.BlockSpec((nc, 2 * SUBLANES, CHUNK), m3),
            pl.BlockSpec((nc, 2 * SUBLANES, LANES), m3),
        ]

    grid_spec = pltpu.PrefetchScalarGridSpec(
        num_scalar_prefetch=2,
        grid=(nb,),
        in_specs=specs(fwd2, fwd3) + specs(bwd2, bwd3),
        out_specs=[pl.BlockSpec((tb, G_DIM), fwd2), pl.BlockSpec((tb, G_DIM), bwd2)],
        scratch_shapes=[pltpu.VMEM((2 * NH, HD, HD), F32),
                        pltpu.VMEM((2, nc, NH, 2 * CHUNK, HD), F32),
                        pltpu.VMEM((2, nc, NH, CHUNK + HD, CHUNK), F32),
                        pltpu.VMEM((2, nc, NH, CHUNK, HD), F32)],
    )
    return pl.pallas_call(
        _gdn_kernel,
        grid_spec=grid_spec,
        out_shape=[jax.ShapeDtypeStruct((t, G_DIM), F32), jax.ShapeDtypeStruct((t, G_DIM), F32)],
        compiler_params=_cparams(("arbitrary",)),
        name="gdn",
    )(reset_f, reset_b, q, k, v, gcol, grow, gend, q, k, v, gcol, grow, gend)


def _fft1_kernel(x_ref, m1_ref, tc_ref, ts_ref, br_ref, bi_ref):
    s1 = x_ref.shape[1]
    a = jnp.dot(m1_ref[...], x_ref[0], preferred_element_type=F32)
    ar, ai = a[0:s1], a[s1:]
    tc, ts = tc_ref[...], ts_ref[...]
    br_ref[0] = ar * tc + ai * ts
    bi_ref[0] = ai * tc - ar * ts


def fft1_call(x3, m1, twc, tws, *, tcol):
    nseq, s1, cols = x3.shape
    blk = pl.BlockSpec((1, s1, tcol), lambda j, b: (b, 0, j))
    tw = pl.BlockSpec((s1, tcol), lambda j, b: (0, j))
    return pl.pallas_call(
        _fft1_kernel,
        grid=(cols // tcol, nseq),
        in_specs=[blk, pl.BlockSpec(m1.shape, lambda j, b: (0, 0)), tw, tw],
        out_specs=[blk, blk],
        out_shape=[jax.ShapeDtypeStruct(x3.shape, F32)] * 2,
        compiler_params=_cparams(("arbitrary", "arbitrary")),
        name="fft_stage1",
    )(x3, m1, twc, tws)


def _fft2_kernel(br_ref, bi_ref, mc_ref, m2_ref, o_ref):
    _, tk, s2, c = br_ref.shape
    nsplit = o_ref.shape[0]
    s2o = s2 // nsplit
    b = jnp.concatenate([br_ref[0].reshape(tk * s2, c), bi_ref[0].reshape(tk * s2, c)], axis=1)
    z = jnp.dot(b, mc_ref[...], preferred_element_type=F32)
    m2 = m2_ref[...]
    for kk in range(tk):
        zk = z[kk * s2:(kk + 1) * s2]
        x = jnp.dot(m2, jnp.concatenate([zk[:, 0:c], zk[:, c:]], axis=0),
                    preferred_element_type=F32)
        for sp in range(nsplit):
            o_ref[sp, kk] = x[sp * s2o:(sp + 1) * s2o]


def fft2_call(br4, bi4, mc, m2, *, tk, nsplit):
    nseq, s1, s2, c = br4.shape
    s2o = s2 // nsplit
    blk = pl.BlockSpec((1, tk, s2, c), lambda b, j: (b, j, 0, 0))
    return pl.pallas_call(
        _fft2_kernel,
        grid=(nseq, s1 // tk),
        in_specs=[blk, blk, pl.BlockSpec(mc.shape, lambda b, j: (0, 0)),
                  pl.BlockSpec(m2.shape, lambda b, j: (0, 0))],
        out_specs=pl.BlockSpec((nsplit, tk, s2o, c), lambda b, j: (b, j, 0, 0)),
        out_shape=jax.ShapeDtypeStruct((nseq * nsplit, s1, s2o, c), F32),
        compiler_params=_cparams(("arbitrary", "arbitrary")),
        name="fft_stage2",
    )(br4, bi4, mc, m2)


def _dft_tables(s):
    s1 = DFT1
    s2 = s // s1
    k = np.arange(s1)
    ang1 = 2.0 * np.pi * ((k[:, None] * k[None, :]) % s1) / s1
    sc = 1.0 / math.sqrt(s)
    m1 = np.concatenate([np.cos(ang1), -np.sin(ang1)], axis=0) * sc
    n2 = np.arange(s2)
    angt = 2.0 * np.pi * ((k[:, None] * n2[None, :]) % s) / s
    twc = np.repeat(np.cos(angt), F_DIM, axis=1)
    tws = np.repeat(np.sin(angt), F_DIM, axis=1)
    ang2 = 2.0 * np.pi * ((n2[:, None] * n2[None, :]) % s2) / s2
    m2 = np.concatenate([np.cos(ang2), np.sin(ang2)], axis=1)
    return (jnp.asarray(m1, F32), jnp.asarray(np.cos(angt), F32), jnp.asarray(np.sin(angt), F32),
            jnp.asarray(m2, F32))


def _channel_dft_matrix():
    g = np.arange(GROUP_DIM)
    ang = 2.0 * np.pi * ((g[:, None] * g[None, :]) % GROUP_DIM) / GROUP_DIM
    cg = np.kron(np.eye(N_GROUPS), np.cos(ang)) / math.sqrt(GROUP_DIM)
    sg = np.kron(np.eye(N_GROUPS), np.sin(ang)) / math.sqrt(GROUP_DIM)
    return jnp.asarray(np.block([[cg, -sg], [sg, cg]]), F32)


def fourier_mix_call(f, *, nseq, s, tcol, tk, nsplit):
    s1 = DFT1
    s2 = s // s1
    m1, tcs, tss, m2 = _dft_tables(s)
    twc = jnp.broadcast_to(tcs[:, :, None], (s1, s2, F_DIM)).reshape(s1, s2 * F_DIM)
    tws = jnp.broadcast_to(tss[:, :, None], (s1, s2, F_DIM)).reshape(s1, s2 * F_DIM)
    x3 = f.reshape(nseq, s1, s2 * F_DIM)
    br, bi = fft1_call(x3, m1, twc, tws, tcol=tcol)
    out = fft2_call(br.reshape(nseq, s1, s2, F_DIM), bi.reshape(nseq, s1, s2, F_DIM),
                    _channel_dft_matrix(), m2, tk=tk, nsplit=nsplit)
    return out.reshape(nseq * nsplit, s1, (s2 // nsplit) * F_DIM)


def _outproj_kernel(x_ref, mod_ref, of_ref, ob_ref, z_ref, fm_ref, gn_ref, w_ref, o_ref):
    tm = x_ref.shape[0]
    o = of_ref[...] + ob_ref[...]
    z = z_ref[...]
    gn = gn_ref[...]
    parts = []
    fm = fm_ref[...]
    parts.append(jnp.concatenate(
        [fm[:, j * F_DIM:(j + 1) * F_DIM] for j in range(tm // DFT1)], axis=0).astype(BF16))
    for h in range(NH):
        oh = o[:, h * HD:(h + 1) * HD]
        ms = jnp.mean(oh * oh, axis=-1, keepdims=True)
        y = (oh * lax.rsqrt(ms + NORM_EPS)) * gn
        parts.append((y * _silu(z[:, h * HD:(h + 1) * HD])).astype(BF16))
    mixed = jnp.concatenate(parts, axis=1)
    proj = jnp.dot(mixed, w_ref[...], preferred_element_type=F32)
    o_ref[...] = x_ref[...] + mod_ref[0, 2:3, :] * proj


def outproj_call(x, mod, o_f, o_b, z, fm2, gn, w, *, seg, tm):
    t, d = x.shape
    row = lambda i: (i, 0)
    const = lambda i: (0, 0)
    per_seg = seg // tm
    return pl.pallas_call(
        _outproj_kernel,
        grid=(t // tm,),
        in_specs=[
            pl.BlockSpec((tm, d), row),
            pl.BlockSpec((1, SUBLANES, d), lambda i: (i // per_seg, 0, 0)),
            pl.BlockSpec((tm, G_DIM), row),
            pl.BlockSpec((tm, G_DIM), row),
            pl.BlockSpec((tm, G_DIM), row),
            pl.BlockSpec((DFT1, (tm // DFT1) * F_DIM), lambda i: (i // per_seg, i % per_seg)),
            pl.BlockSpec((1, HD), const),
            pl.BlockSpec(w.shape, const),
        ],
        out_specs=pl.BlockSpec((tm, d), row),
        out_shape=jax.ShapeDtypeStruct((t, d), F32),
        compiler_params=_cparams(("arbitrary",)),
        name="outproj",
    )(x, mod, o_f, o_b, z, fm2, gn, w)


def _ffn_kernel(x_ref, mod_ref, nw_ref, wg_ref, wu_ref, wd_ref, o_ref, *, nsplit):
    x = x_ref[...]
    h = _mod_norm(x, nw_ref[...], mod_ref[0, 4:5, :], mod_ref[0, 3:4, :]).astype(BF16)
    dff = wg_ref.shape[1]
    cw = dff // nsplit
    acc = None
    for c in range(nsplit):
        g = jnp.dot(h, wg_ref[:, c * cw:(c + 1) * cw], preferred_element_type=F32)
        u = jnp.dot(h, wu_ref[:, c * cw:(c + 1) * cw], preferred_element_type=F32)
        hid = (_silu(g) * u).astype(BF16)
        part = jnp.dot(hid, wd_ref[c * cw:(c + 1) * cw, :], preferred_element_type=F32)
        acc = part if acc is None else acc + part
    o_ref[...] = x + mod_ref[0, 5:6, :] * acc


def ffn_call(x, mod, nw, wg, wu, wd, *, seg, tm):
    t, d = x.shape
    row = lambda i: (i, 0)
    const = lambda i: (0, 0)
    single = pl.Buffered(1)
    return pl.pallas_call(
        functools.partial(_ffn_kernel, nsplit=2),
        grid=(t // tm,),
        in_specs=[
            pl.BlockSpec((tm, d), row),
            pl.BlockSpec((1, SUBLANES, d), lambda i: (i * tm // seg, 0, 0)),
            pl.BlockSpec((1, d), const),
            pl.BlockSpec(wg.shape, const, pipeline_mode=single),
            pl.BlockSpec(wu.shape, const, pipeline_mode=single),
            pl.BlockSpec(wd.shape, const, pipeline_mode=single),
        ],
        out_specs=pl.BlockSpec((tm, d), row),
        out_shape=jax.ShapeDtypeStruct((t, d), F32),
        compiler_params=_cparams(("arbitrary",)),
        name="ffn",
    )(x, mod, nw, wg, wu, wd)


def _router_kernel(x_ref, mod_ref, nw_ref, wr_ref, tri_ref, oi_ref, op_ref, cnt_ref, carry_ref):
    i = pl.program_id(0)
    tm = x_ref.shape[0]

    @pl.when(i == 0)
    def _():
        carry_ref[...] = jnp.zeros_like(carry_ref)

    h = _mod_norm(x_ref[...], nw_ref[...], mod_ref[0, 4:5, :], mod_ref[0, 3:4, :])
    logits = jnp.dot(h, wr_ref[...], precision=lax.Precision.HIGHEST, preferred_element_type=F32)
    lane = lax.broadcasted_iota(jnp.int32, (tm, LANES), 1)
    logits = jnp.where(lane < N_EXPERTS, logits, NEG_BIG)
    l1 = jnp.max(logits, axis=-1, keepdims=True)
    i1 = jnp.min(jnp.where(logits == l1, lane, LANES), axis=-1, keepdims=True)
    rest = jnp.where(lane == i1, NEG_BIG, logits)
    l2 = jnp.max(rest, axis=-1, keepdims=True)
    i2 = jnp.min(jnp.where(rest == l2, lane, LANES), axis=-1, keepdims=True)
    e21 = jnp.exp(l2 - l1)
    p1 = 1.0 / (1.0 + e21)
    p2 = e21 * p1
    oh1 = lane == i1
    oh2 = lane == i2
    oh = jnp.where(oh1 | oh2, 1.0, 0.0).astype(BF16)
    before = jnp.dot(tri_ref[...], oh, preferred_element_type=F32) + carry_ref[0:1, :]
    r1 = jnp.sum(jnp.where(oh1, before, 0.0), axis=-1, keepdims=True).astype(jnp.int32)
    r2 = jnp.sum(jnp.where(oh2, before, 0.0), axis=-1, keepdims=True).astype(jnp.int32)
    oi_ref[...] = jnp.where(lane == 0, i1, jnp.where(lane == 1, i2, jnp.where(lane == 2, r1, r2)))
    op_ref[...] = jnp.where(lane == 0, p1, p2)
    new_carry = carry_ref[0:1, :] + jnp.sum(oh.astype(F32), axis=0, keepdims=True)
    carry_ref[...] = jnp.broadcast_to(new_carry, carry_ref.shape)
    cnt_ref[...] = jnp.broadcast_to(new_carry, cnt_ref.shape)


def router_call(x, mod, nw, wr, tri, *, seg, tm):
    t, d = x.shape
    row = lambda i: (i, 0)
    const = lambda i: (0, 0)
    return pl.pallas_call(
        _router_kernel,
        grid=(t // tm,),
        in_specs=[
            pl.BlockSpec((tm, d), row),
            pl.BlockSpec((1, SUBLANES, d), lambda i: (i * tm // seg, 0, 0)),
            pl.BlockSpec((1, d), const),
            pl.BlockSpec(wr.shape, const),
            pl.BlockSpec(tri.shape, const),
        ],
        out_specs=[pl.BlockSpec((tm, LANES), row), pl.BlockSpec((tm, LANES), row),
                   pl.BlockSpec((SUBLANES, LANES), const)],
        out_shape=[jax.ShapeDtypeStruct((t, LANES), jnp.int32), jax.ShapeDtypeStruct((t, LANES), F32),
                   jax.ShapeDtypeStruct((SUBLANES, LANES), F32)],
        scratch_shapes=[pltpu.VMEM((SUBLANES, LANES), F32)],
        compiler_params=_cparams(("arbitrary",)),
        name="router",
    )(x, mod, nw, wr, tri)


def _dispatch_kernel(pos_ref, x_ref, mod_ref, nw_ref, xs_in_ref, xs_ref, hbuf, sem):
    del xs_in_ref
    tm = x_ref.shape[0]
    hbuf[...] = _mod_norm(x_ref[...], nw_ref[...], mod_ref[0, 4:5, :], mod_ref[0, 3:4, :])

    def row_copy(r, slot):
        return pltpu.make_async_copy(hbuf.at[pl.ds(r, 1), :],
                                     xs_ref.at[pl.ds(pos_ref[0, 0, slot * tm + r], 1), :], sem)

    def start(r, c):
        row_copy(r, 0).start()
        row_copy(r, 1).start()
        return c

    def wait(r, c):
        row_copy(r, 0).wait()
        row_copy(r, 1).wait()
        return c

    lax.fori_loop(0, tm, start, 0)
    lax.fori_loop(0, tm, wait, 0)


def dispatch_call(pos3, x, mod, nw, xs_zero, *, seg, tm):
    t, d = x.shape
    return pl.pallas_call(
        _dispatch_kernel,
        grid=(t // tm,),
        in_specs=[
            pl.BlockSpec((1, 1, 2 * tm), lambda i: (i, 0, 0), memory_space=pltpu.SMEM),
            pl.BlockSpec((tm, d), lambda i: (i, 0)),
            pl.BlockSpec((1, SUBLANES, d), lambda i: (i * tm // seg, 0, 0)),
            pl.BlockSpec((1, d), lambda i: (0, 0)),
            pl.BlockSpec(memory_space=pl.ANY),
        ],
        out_specs=pl.BlockSpec(memory_space=pl.ANY),
        out_shape=jax.ShapeDtypeStruct(xs_zero.shape, F32),
        scratch_shapes=[pltpu.VMEM((tm, d), F32), pltpu.SemaphoreType.DMA(())],
        input_output_aliases={4: 0},
        compiler_params=_cparams(("arbitrary",)),
        name="moe_dispatch",
    )(pos3, x, mod, nw, xs_zero)


def _expert_kernel(te_ref, nu_ref, xs_ref, wg_ref, wu_ref, wd_ref, y_ref, acc_ref):
    j = pl.program_id(0)
    half = pl.program_id(1)

    @pl.when(j < nu_ref[0])
    def _():
        h = xs_ref[...].astype(BF16)
        g = jnp.dot(h, wg_ref[0], preferred_element_type=F32)
        u = jnp.dot(h, wu_ref[0], preferred_element_type=F32)
        hid = (_silu(g) * u).astype(BF16)
        part = jnp.dot(hid, wd_ref[0], preferred_element_type=F32)

        @pl.when(half == 0)
        def _():
            acc_ref[...] = part

        @pl.when(half == 1)
        def _():
            y_ref[...] = acc_ref[...] + part

    @pl.when((j >= nu_ref[0]) & (half == 1))
    def _():
        y_ref[...] = jnp.zeros_like(y_ref)


def expert_call(tile_expert, n_used, xs, wg, wu, wd, *, tme):
    nr, d = xs.shape
    ntiles = nr // tme
    fh = wg.shape[2] // 2

    def jj(j, nu):
        return jnp.minimum(j, nu[0] - 1)

    def hh(j, hf, nu):
        return jnp.where(j < nu[0], hf, 1)

    grid_spec = pltpu.PrefetchScalarGridSpec(
        num_scalar_prefetch=2,
        grid=(ntiles, 2),
        in_specs=[
            pl.BlockSpec((tme, d), lambda j, hf, te, nu: (jj(j, nu), 0)),
            pl.BlockSpec((1, d, fh), lambda j, hf, te, nu: (te[jj(j, nu)], 0, hh(j, hf, nu))),
            pl.BlockSpec((1, d, fh), lambda j, hf, te, nu: (te[jj(j, nu)], 0, hh(j, hf, nu))),
            pl.BlockSpec((1, fh, d), lambda j, hf, te, nu: (te[jj(j, nu)], hh(j, hf, nu), 0)),
        ],
        out_specs=pl.BlockSpec((tme, d), lambda j, hf, te, nu: (j, 0)),
        scratch_shapes=[pltpu.VMEM((tme, d), F32)],
    )
    return pl.pallas_call(
        _expert_kernel,
        grid_spec=grid_spec,
        out_shape=jax.ShapeDtypeStruct((nr, d), F32),
        compiler_params=_cparams(("arbitrary", "arbitrary")),
        name="moe_experts",
    )(tile_expert, n_used, xs, wg, wu, wd)


def _combine_kernel(pos_ref, x_ref, mod_ref, p_ref, y_ref, o_ref, ybuf, sem):
    tm = x_ref.shape[0]

    def row_copy(r, slot):
        return pltpu.make_async_copy(y_ref.at[pl.ds(pos_ref[0, 0, slot * tm + r], 1), :],
                                     ybuf.at[slot, pl.ds(r, 1), :], sem)

    def start(r, c):
        row_copy(r, 0).start()
        row_copy(r, 1).start()
        return c

    def wait(r, c):
        row_copy(r, 0).wait()
        row_copy(r, 1).wait()
        return c

    lax.fori_loop(0, tm, start, 0)
    lax.fori_loop(0, tm, wait, 0)
    p = p_ref[...]
    f = p[:, 0:1] * ybuf[0] + p[:, 1:2] * ybuf[1]
    o_ref[...] = x_ref[...] + mod_ref[0, 5:6, :] * f


def combine_call(pos3, x, mod, p, y, *, seg, tm):
    t, d = x.shape
    return pl.pallas_call(
        _combine_kernel,
        grid=(t // tm,),
        in_specs=[
            pl.BlockSpec((1, 1, 2 * tm), lambda i: (i, 0, 0), memory_space=pltpu.SMEM),
            pl.BlockSpec((tm, d), lambda i: (i, 0)),
            pl.BlockSpec((1, SUBLANES, d), lambda i: (i * tm // seg, 0, 0)),
            pl.BlockSpec((tm, LANES), lambda i: (i, 0)),
            pl.BlockSpec(memory_space=pl.ANY),
        ],
        out_specs=pl.BlockSpec((tm, d), lambda i: (i, 0)),
        out_shape=jax.ShapeDtypeStruct((t, d), F32),
        scratch_shapes=[pltpu.VMEM((2, tm, d), F32), pltpu.SemaphoreType.DMA(())],
        compiler_params=_cparams(("arbitrary",)),
        name="moe_combine",
    )(pos3, x, mod, p, y)


def moe_block(x, mod, nw, wr, wg, wu, wd, *, seg, tm, tmd, tme):
    t, d = x.shape
    wr_pad = jnp.zeros((d, LANES), F32).at[:, :N_EXPERTS].set(wr)
    tri = jnp.asarray(np.tril(np.ones((tm, tm), np.float32), k=-1), BF16)
    oi, op, cnt = router_call(x, mod, nw, wr_pad, tri, seg=seg, tm=tm)
    counts = cnt[0, :N_EXPERTS].astype(jnp.int32)
    padded = ((counts + tme - 1) // tme) * tme
    ends = jnp.cumsum(padded)
    starts = ends - padded
    nr = ((2 * t + N_EXPERTS * (tme - 1)) // tme) * tme
    ntiles = nr // tme
    n_used = (ends[-1] // tme).astype(jnp.int32).reshape(1)
    tile_start = jnp.arange(ntiles, dtype=jnp.int32) * tme
    tile_expert = jnp.minimum(jnp.sum(tile_start[:, None] >= ends[None, :], axis=1),
                              N_EXPERTS - 1).astype(jnp.int32)
    pos1 = starts[oi[:, 0]] + oi[:, 2]
    pos2 = starts[oi[:, 1]] + oi[:, 3]
    pos3 = jnp.concatenate([pos1.reshape(t // tmd, 1, tmd), pos2.reshape(t // tmd, 1, tmd)], axis=2)
    xs = dispatch_call(pos3, x, mod, nw, jnp.zeros((nr, d), F32), seg=seg, tm=tmd)
    y = expert_call(tile_expert, n_used, xs, wg, wu, wd, tme=tme)
    return combine_call(pos3, x, mod, op, y, seg=seg, tm=tmd)


def _final_kernel(x_ref, w_ref, o_ref):
    x = x_ref[...]
    ms = jnp.mean(x * x, axis=-1, keepdims=True)
    o_ref[...] = (x * lax.rsqrt(ms + NORM_EPS)) * w_ref[...]


def final_call(x, w, *, row0, nrows, tm):
    d = x.shape[1]
    off = row0 // tm
    return pl.pallas_call(
        _final_kernel,
        grid=(nrows // tm,),
        in_specs=[pl.BlockSpec((tm, d), lambda i: (i + off, 0)), pl.BlockSpec((1, d), lambda i: (0, 0))],
        out_specs=pl.BlockSpec((tm, d), lambda i: (i, 0)),
        out_shape=jax.ShapeDtypeStruct((nrows, d), F32),
        compiler_params=_cparams(("arbitrary",)),
        name="final_norm",
    )(x, w)


def _segment_flags(prompt_segs, sample_segs, seg, tile):
    per_seg = seg // tile
    nseg = prompt_segs + sample_segs
    first = np.zeros(nseg * per_seg, np.int32)
    last = np.zeros(nseg * per_seg, np.int32)
    first[0] = 1
    last[prompt_segs * per_seg - 1] = 1
    for s in range(prompt_segs, nseg):
        first[s * per_seg] = 1
        last[(s + 1) * per_seg - 1] = 1
    return first, last


def encoder_pair(x_prompt, x_sample, c_prompt, c_sample, w_ada, b_ada, norm_mix, norm_ffn, w_in, conv_w,
                 a_log, dt_bias, gdn_norm, w_out, w_ffn_gate, w_ffn_up, w_ffn_down, w_router,
                 w_exp_gate, w_exp_up, w_exp_down, norm_final, *, tm=512, tb=256, tmd=256, tme=512):
    bp, sp, d = x_prompt.shape
    bs, seg, _ = x_sample.shape
    assert bp == 1 and sp % seg == 0 and seg % tm == 0 and tm % DFT1 == 0
    depth = w_ada.shape[0]
    prompt_segs = sp // seg
    nseg = prompt_segs + bs
    t = nseg * seg
    x = jnp.concatenate([x_prompt.reshape(sp, d), x_sample.reshape(bs * seg, d)], axis=0)

    nrow = -(-(1 + bs) // SUBLANES) * SUBLANES
    c_all = jnp.zeros((nrow, d), F32).at[0:1].set(c_prompt).at[1:1 + bs].set(c_sample)
    ada = ada_call(c_all, w_ada, b_ada).reshape(depth, nrow, 6, d)
    seg_row = np.concatenate([np.zeros(prompt_segs, np.int32), 1 + np.arange(bs, dtype=np.int32)])
    mod_all = jnp.pad(ada[:, seg_row], ((0, 0), (0, 0), (0, SUBLANES - 6), (0, 0)))

    first_c, last_c = _segment_flags(prompt_segs, bs, seg, tm)
    first_g, last_g = _segment_flags(prompt_segs, bs, seg, tb)
    reset_f = jnp.asarray(first_g)
    reset_b = jnp.asarray(last_g[::-1].copy())
    first_c, last_c = jnp.asarray(first_c), jnp.asarray(last_c)

    off_f, off_qkv, off_z = F_DIM, F_DIM + QKV_DIM, F_DIM + QKV_DIM + G_DIM
    off_b = off_z + 2 * NH
    for l in range(depth):
        mod = mod_all[l]
        wl = w_in[l]
        wf = wl[:, :off_f].astype(BF16)
        wqkv = wl[:, off_f:off_qkv].astype(BF16)
        wz = wl[:, off_qkv:off_z].astype(BF16)
        wba = jnp.pad(wl[:, off_z:], ((0, 0), (0, LANES - 4 * NH))).astype(BF16)
        f, qkv, z, ba = inproj_call(x, mod, norm_mix[l].reshape(1, d), wf, wqkv, wz, wba, seg=seg, tm=tm)

        cw = jnp.pad(conv_w[l], ((0, SUBLANES - CONV_K), (0, 0)))
        gp = jnp.zeros((SUBLANES, LANES), F32)
        gp = gp.at[0, 2 * NH:4 * NH].set(dt_bias[l].reshape(-1)).at[1, 2 * NH:4 * NH].set(a_log[l].reshape(-1))
        q, k, v, gcol, grow, gend = conv_call(first_c, last_c, qkv, cw, ba, gp, tm=tm)
        o_f, o_b = gdn_call(reset_f, reset_b, q, k, v, gcol, grow, gend, tb=tb)

        s2p = sp // DFT1
        fm_p = fourier_mix_call(f[:sp], nseq=1, s=sp, tcol=min(2048, s2p * F_DIM),
                                tk=min(16, DFT1), nsplit=prompt_segs)
        s2s = seg // DFT1
        fm_s = fourier_mix_call(f[sp:], nseq=bs, s=seg, tcol=min(2048, s2s * F_DIM),
                                tk=min(64, DFT1), nsplit=1)
        fm2 = jnp.concatenate([fm_p, fm_s], axis=0).reshape(nseg * DFT1, (seg // DFT1) * F_DIM)

        x = outproj_call(x, mod, o_f, o_b, z, fm2, gdn_norm[l].reshape(1, HD), w_out[l].astype(BF16),
                         seg=seg, tm=tm)
        nw = norm_ffn[l].reshape(1, d)
        if l % 2 == 0:
            i = l // 2
            x = ffn_call(x, mod, nw, w_ffn_gate[i].astype(BF16), w_ffn_up[i].astype(BF16),
                         w_ffn_down[i].astype(BF16), seg=seg, tm=tm)
        else:
            i = l // 2
            x = moe_block(x, mod, nw, w_router[i], w_exp_gate[i].astype(BF16), w_exp_up[i].astype(BF16),
                          w_exp_down[i].astype(BF16), seg=seg, tm=tm, tmd=tmd, tme=tme)

    wn = norm_final.reshape(1, d)
    y_p = final_call(x, wn, row0=0, nrows=sp, tm=tm).reshape(bp, sp, d)
    y_s = final_call(x, wn, row0=sp, nrows=bs * seg, tm=tm).reshape(bs, seg, d)
    return y_p, y_s


def kernel(x_prompt, x_sample, c_prompt, c_sample, w_ada, b_ada, norm_mix, norm_ffn, w_in, conv_w, a_log,
           dt_bias, gdn_norm, w_out, w_ffn_gate, w_ffn_up, w_ffn_down, w_router, w_exp_gate, w_exp_up,
           w_exp_down, norm_final):
    return encoder_pair(x_prompt, x_sample, c_prompt, c_sample, w_ada, b_ada, norm_mix, norm_ffn, w_in,
                        conv_w, a_log, dt_bias, gdn_norm, w_out, w_ffn_gate, w_ffn_up, w_ffn_down,
                        w_router, w_exp_gate, w_exp_up, w_exp_down, norm_final)
```

```python
import functools
import math

import numpy as np
import jax
import jax.numpy as jnp
from jax import lax
from jax.experimental import pallas as pl
from jax.experimental.pallas import tpu as pltpu

F32 = jnp.float32
BF16 = jnp.bfloat16

D_MODEL = 1024
DEPTH = 4
N_GROUPS = 4
GROUP_DIM = 64
F_DIM = N_GROUPS * GROUP_DIM
HD = 128
NH = 6
G_DIM = NH * HD
QKV_DIM = 3 * G_DIM
CONV_K = 5
CHUNK = 64
PREP_CHUNKS = 2
D_FF = 2816
N_EXPERTS = 8
D_EXPERT = 3584
NORM_EPS = 1e-6

LANES = 128
SUBLANES = 8
VMEM_LIMIT = 56 * 1024 * 1024
DFT1 = 128

GL_GAMMA, GL_BETA, GL_EG, GL_EGR = 0, 16, 32, 48
NEG_BIG = -1e30


def _cparams(sem):
    return pltpu.CompilerParams(dimension_semantics=sem, vmem_limit_bytes=VMEM_LIMIT)


def _mod_norm(x, nw, sc, sh):
    ms = jnp.mean(x * x, axis=-1, keepdims=True)
    y = x * lax.rsqrt(ms + NORM_EPS)
    return (y * nw) * (1.0 + sc) + sh


def _silu(x):
    return x * (1.0 / (1.0 + jnp.exp(-x)))


def _ada_kernel(c_ref, w_ref, b_ref, o_ref):
    c = _silu(c_ref[...])
    o_ref[0] = jnp.dot(c.astype(BF16), w_ref[0].astype(BF16), preferred_element_type=F32) + b_ref[0]


def ada_call(c_all, w_ada, b_ada):
    nrow = c_all.shape[0]
    depth, d, d6 = w_ada.shape
    tn = 1024
    return pl.pallas_call(
        _ada_kernel,
        grid=(depth, d6 // tn),
        in_specs=[
            pl.BlockSpec((nrow, d), lambda l, j: (0, 0)),
            pl.BlockSpec((1, d, tn), lambda l, j: (l, 0, j)),
            pl.BlockSpec((1, 1, tn), lambda l, j: (l, 0, j)),
        ],
        out_specs=pl.BlockSpec((1, nrow, tn), lambda l, j: (l, 0, j)),
        out_shape=jax.ShapeDtypeStruct((depth, nrow, d6), F32),
        compiler_params=_cparams(("arbitrary", "arbitrary")),
        name="ada",
    )(c_all, w_ada, b_ada.reshape(depth, 1, d6))


def _inproj_kernel(x_ref, mod_ref, nw_ref, wf_ref, wqkv_ref, wz_ref, wba_ref,
                   f_ref, qkv_ref, z_ref, ba_ref):
    h = _mod_norm(x_ref[...], nw_ref[...], mod_ref[0, 1:2, :], mod_ref[0, 0:1, :]).astype(BF16)
    f_ref[...] = jnp.dot(h, wf_ref[...], preferred_element_type=F32)
    qkv_ref[...] = jnp.dot(h, wqkv_ref[...], preferred_element_type=F32)
    z_ref[...] = jnp.dot(h, wz_ref[...], preferred_element_type=F32)
    ba_ref[...] = jnp.dot(h, wba_ref[...], preferred_element_type=F32)


def inproj_call(x, mod, nw, wf, wqkv, wz, wba, *, seg, tm):
    t, d = x.shape
    const = lambda i: (0, 0)
    row = lambda i: (i, 0)
    return pl.pallas_call(
        _inproj_kernel,
        grid=(t // tm,),
        in_specs=[
            pl.BlockSpec((tm, d), row),
            pl.BlockSpec((1, SUBLANES, d), lambda i: (i * tm // seg, 0, 0)),
            pl.BlockSpec((1, d), const),
            pl.BlockSpec(wf.shape, const),
            pl.BlockSpec(wqkv.shape, const),
            pl.BlockSpec(wz.shape, const),
            pl.BlockSpec(wba.shape, const),
        ],
        out_specs=[
            pl.BlockSpec((tm, F_DIM), row),
            pl.BlockSpec((tm, QKV_DIM), row),
            pl.BlockSpec((tm, G_DIM), row),
            pl.BlockSpec((tm, LANES), row),
        ],
        out_shape=[
            jax.ShapeDtypeStruct((t, F_DIM), F32),
            jax.ShapeDtypeStruct((t, QKV_DIM), F32),
            jax.ShapeDtypeStruct((t, G_DIM), F32),
            jax.ShapeDtypeStruct((t, LANES), F32),
        ],
        compiler_params=_cparams(("arbitrary",)),
        name="inproj",
    )(x, mod, nw, wf, wqkv, wz, wba)


def _conv_kernel(first_ref, last_ref,
                 qkv_ref, prev_ref, next_ref, cw_ref, ba_ref, gp_ref,
                 q_ref, k_ref, v_ref, gcol_ref, grow_ref, gend_ref,
                 xe_ref):
    i = pl.program_id(0)
    tm = qkv_ref.shape[0]
    nc = tm // CHUNK
    halo = SUBLANES
    pm = jnp.where(first_ref[i] == 1, 0.0, 1.0)
    nm = jnp.where(last_ref[i] == 1, 0.0, 1.0)
    xe_ref[0:halo, :] = prev_ref[...] * pm
    xe_ref[halo:halo + tm, :] = qkv_ref[...]
    xe_ref[halo + tm:, :] = next_ref[...] * nm

    outs = (q_ref, k_ref, v_ref)
    for s in range(3 * NH):
        lo = s * HD
        acc = None
        for j in range(CONV_K):
            r0 = halo - CONV_K // 2 + j
            term = xe_ref[r0:r0 + tm, lo:lo + HD] * cw_ref[j:j + 1, lo:lo + HD]
            acc = term if acc is None else acc + term
        y = _silu(acc)
        which, h = divmod(s, NH)
        if which < 2:
            y = y * lax.rsqrt(jnp.sum(y * y, axis=-1, keepdims=True) + 1e-6)
        if which == 0:
            y = y * (HD ** -0.5)
        outs[which][:, h * HD:(h + 1) * HD] = y

    ba = ba_ref[...]
    lane = lax.broadcasted_iota(jnp.int32, (tm, LANES), 1)
    beta = 1.0 / (1.0 + jnp.exp(-ba))
    xs = ba + gp_ref[0:1, :]
    softplus = jnp.maximum(xs, 0.0) + jnp.log(1.0 + jnp.exp(-jnp.abs(xs)))
    g = -jnp.exp(gp_ref[1:2, :]) * softplus
    g = jnp.where((lane >= 2 * NH) & (lane < 4 * NH), g, 0.0)
    g = pltpu.roll(g, LANES - 2 * NH, axis=1)

    rowc = lax.broadcasted_iota(jnp.int32, (tm, LANES), 0) % CHUNK
    p = g
    sh = 1
    while sh < CHUNK:
        p = p + jnp.where(rowc >= sh, pltpu.roll(p, sh, axis=0), 0.0)
        sh *= 2
    g3 = g.reshape(nc, CHUNK, LANES)
    tot = jnp.broadcast_to(jnp.sum(g3, axis=1, keepdims=True), (nc, CHUNK, LANES)).reshape(tm, LANES)
    is_bwd = (lane >= NH) & (lane < 2 * NH)
    gamma = jnp.where(is_bwd, tot - p + g, p)
    eg = jnp.exp(gamma)
    egr = jnp.exp(tot - gamma)
    m12 = lane < 2 * NH
    gcol = (jnp.where(m12, gamma, 0.0)
            + pltpu.roll(jnp.where(m12, beta, 0.0), GL_BETA, axis=1)
            + pltpu.roll(jnp.where(m12, eg, 0.0), GL_EG, axis=1)
            + pltpu.roll(jnp.where(m12, egr, 0.0), GL_EGR, axis=1))
    gcol_ref[...] = gcol
    gam_t = jnp.where(m12, gamma, 0.0).T[0:2 * SUBLANES, :]
    g_t = g.T[0:2 * SUBLANES, :]
    for c in range(nc):
        grow_ref[c] = gam_t[:, c * CHUNK:(c + 1) * CHUNK]
        tc = jnp.sum(g_t[:, c * CHUNK:(c + 1) * CHUNK], axis=-1, keepdims=True)
        gend_ref[c] = jnp.exp(jnp.broadcast_to(tc, (2 * SUBLANES, LANES)))


def conv_call(first, last, qkv, cw, ba, gp, *, tm):
    t = qkv.shape[0]
    nblk8 = t // SUBLANES
    r8 = tm // SUBLANES
    nc = tm // CHUNK
    grid_spec = pltpu.PrefetchScalarGridSpec(
        num_scalar_prefetch=2,
        grid=(t // tm,),
        in_specs=[
            pl.BlockSpec((tm, QKV_DIM), lambda i, f, l: (i, 0)),
            pl.BlockSpec((SUBLANES, QKV_DIM), lambda i, f, l: (jnp.maximum(i * r8 - 1, 0), 0)),
            pl.BlockSpec((SUBLANES, QKV_DIM), lambda i, f, l: (jnp.minimum((i + 1) * r8, nblk8 - 1), 0)),
            pl.BlockSpec((SUBLANES, QKV_DIM), lambda i, f, l: (0, 0)),
            pl.BlockSpec((tm, LANES), lambda i, f, l: (i, 0)),
            pl.BlockSpec((SUBLANES, LANES), lambda i, f, l: (0, 0)),
        ],
        out_specs=[
            pl.BlockSpec((tm, G_DIM), lambda i, f, l: (i, 0)),
            pl.BlockSpec((tm, G_DIM), lambda i, f, l: (i, 0)),
            pl.BlockSpec((tm, G_DIM), lambda i, f, l: (i, 0)),
            pl.BlockSpec((tm, LANES), lambda i, f, l: (i, 0)),
            pl.BlockSpec((nc, 2 * SUBLANES, CHUNK), lambda i, f, l: (i, 0, 0)),
            pl.BlockSpec((nc, 2 * SUBLANES, LANES), lambda i, f, l: (i, 0, 0)),
        ],
        scratch_shapes=[pltpu.VMEM((tm + 2 * SUBLANES, QKV_DIM), F32)],
    )
    return pl.pallas_call(
        _conv_kernel,
        grid_spec=grid_spec,
        out_shape=[
            jax.ShapeDtypeStruct((t, G_DIM), F32),
            jax.ShapeDtypeStruct((t, G_DIM), F32),
            jax.ShapeDtypeStruct((t, G_DIM), F32),
            jax.ShapeDtypeStruct((t, LANES), F32),
            jax.ShapeDtypeStruct((t // CHUNK, 2 * SUBLANES, CHUNK), F32),
            jax.ShapeDtypeStruct((t // CHUNK, 2 * SUBLANES, LANES), F32),
        ],
        compiler_params=_cparams(("arbitrary",)),
        name="conv_gates",
    )(first, last, qkv, qkv, qkv, cw, ba, gp)


def _gdn_kernel(rf_ref, rb_ref,
                qf_ref, kf_ref, vf_ref, gcf_ref, grf_ref, gef_ref,
                qb_ref, kb_ref, vb_ref, gcb_ref, grb_ref, geb_ref,
                of_ref, ob_ref, s_ref, wq_ref, ab_ref, u_ref):
    i = pl.program_id(0)
    tb = qf_ref.shape[0]
    nc = tb // CHUNK

    @pl.when(rf_ref[i] == 1)
    def _():
        s_ref[0:NH] = jnp.zeros((NH, HD, HD), F32)

    @pl.when(rb_ref[i] == 1)
    def _():
        s_ref[NH:2 * NH] = jnp.zeros((NH, HD, HD), F32)

    row = lax.broadcasted_iota(jnp.int32, (CHUNK, CHUNK), 0)
    col = lax.broadcasted_iota(jnp.int32, (CHUNK, CHUNK), 1)
    eye = jnp.where(row == col, 1.0, 0.0).astype(F32)
    level_masks = []
    b = 1
    while b < CHUNK:
        level_masks.append((row // (2 * b) == col // (2 * b)) & (row // b != col // b))
        b *= 2
    dirs = (
        (qf_ref, kf_ref, vf_ref, gcf_ref, grf_ref, gef_ref, of_ref, row >= col, row > col),
        (qb_ref, kb_ref, vb_ref, gcb_ref, grb_ref, geb_ref, ob_ref, row <= col, row < col),
    )

    hds = [(d, h) for d in range(2) for h in range(NH)]
    probs = [(u, d, h) for u in range(PREP_CHUNKS) for d, h in hds]
    npr = len(probs)

    def prep_step(it, carry):
        cs = [it * PREP_CHUNKS + u for u in range(PREP_CHUNKS)]
        qs, ks, vs, cols, decs = [], [], [], [], []
        for u, d, h in probs:
            r0 = pl.multiple_of(cs[u] * CHUNK, CHUNK)
            q_ref, k_ref, v_ref, gc_ref, gr_ref = dirs[d][0:5]
            hd = d * NH + h
            lo = h * HD
            gc = gc_ref[pl.ds(r0, CHUNK), :]
            qs.append(q_ref[pl.ds(r0, CHUNK), lo:lo + HD])
            ks.append(k_ref[pl.ds(r0, CHUNK), lo:lo + HD])
            vs.append(v_ref[pl.ds(r0, CHUNK), lo:lo + HD])
            gam_c = gc[:, GL_GAMMA + hd:GL_GAMMA + hd + 1]
            cols.append((gc[:, GL_BETA + hd:GL_BETA + hd + 1], gc[:, GL_EG + hd:GL_EG + hd + 1],
                         gc[:, GL_EGR + hd:GL_EGR + hd + 1]))
            gam_r = gr_ref[cs[u]][hd:hd + 1, :]
            decs.append(jnp.exp(jnp.where(dirs[d][7], gam_c - gam_r, NEG_BIG)))
        kqs = [lax.dot_general(jnp.concatenate([ks[n], qs[n]], axis=0), ks[n],
                               (((1,), (1,)), ((), ())), preferred_element_type=F32)
               for n in range(npr)]
        a_s = [jnp.where(dirs[d][8], kqs[n][0:CHUNK] * cols[n][0] * decs[n], 0.0)
               for n, (u, d, h) in enumerate(probs)]
        for n, (u, d, h) in enumerate(probs):
            ab_ref[d, cs[u], h, 0:CHUNK, :] = kqs[n][CHUNK:] * decs[n]
            ab_ref[d, cs[u], h, CHUNK:, :] = (ks[n] * cols[n][2]).T
        ts = [eye - jnp.where(level_masks[0], a_s[n], 0.0) for n in range(npr)]
        for lm in level_masks[1:]:
            lts = [jnp.dot(jnp.where(lm, a_s[n], 0.0), ts[n], preferred_element_type=F32) for n in range(npr)]
            ts = [ts[n] - jnp.dot(ts[n], lts[n], preferred_element_type=F32) for n in range(npr)]
        uws =[jnp.dot(ts[n], jnp.concatenate([vs[n] * cols[n][0], ks[n] * (cols[n][0] * cols[n][1])], axis=1),
                       preferred_element_type=F32) for n in range(npr)]
        for n, (u, d, h) in enumerate(probs):
            u_ref[d, cs[u], h] = uws[n][:, 0:HD]
            wq_ref[d, cs[u], h, 0:CHUNK, :] = uws[n][:, HD:]
            wq_ref[d, cs[u], h, CHUNK:, :] = qs[n] * cols[n][1]
        return carry

    lax.fori_loop(0, nc // PREP_CHUNKS, prep_step, 0)

    def scan_step(c, carry):
        ccs = (c, nc - 1 - c)
        sts = [s_ref[d * NH + h] for d, h in hds]
        wqs = [jnp.dot(wq_ref[d, ccs[d], h], sts[n], preferred_element_type=F32)
               for n, (d, h) in enumerate(hds)]
        vns = [u_ref[d, ccs[d], h] - wqs[n][0:CHUNK] for n, (d, h) in enumerate(hds)]
        avs = [jnp.dot(ab_ref[d, ccs[d], h], vns[n], preferred_element_type=F32)
               for n, (d, h) in enumerate(hds)]
        for n, (d, h) in enumerate(hds):
            r0 = pl.multiple_of(ccs[d] * CHUNK, CHUNK)
            dirs[d][6][pl.ds(r0, CHUNK), h * HD:(h + 1) * HD] = wqs[n][CHUNK:] + avs[n][0:CHUNK]
            ge = dirs[d][5][ccs[d]]
            hd = d * NH + h
            s_ref[hd] = sts[n] * ge[hd:hd + 1, :] + avs[n][CHUNK:]
        return carry

    lax.fori_loop(0, nc, scan_step, 0)


def gdn_call(reset_f, reset_b, q, k, v, gcol, grow, gend, *, tb):
    t = q.shape[0]
    nb = t // tb
    nc = tb // CHUNK
    fwd2 = lambda i, a, b: (i, 0)
    bwd2 = lambda i, a, b: (nb - 1 - i, 0)
    fwd3 = lambda i, a, b: (i, 0, 0)
    bwd3 = lambda i, a, b: (nb - 1 - i, 0, 0)

    def specs(m2, m3):
        return [
            pl.BlockSpec((tb, G_DIM), m2), pl.BlockSpec((tb, G_DIM), m2), pl.BlockSpec((tb, G_DIM), m2),
            pl.BlockSpec((tb, LANES), m2),
            pl.BlockSpec((nc, 2 * SUBLANES, CHUNK), m3),
            pl.BlockSpec((nc, 2 * SUBLANES, LANES), m3),
        ]

    grid_spec = pltpu.PrefetchScalarGridSpec(
        num_scalar_prefetch=2,
        grid=(nb,),
        in_specs=specs(fwd2, fwd3) + specs(bwd2, bwd3),
        out_specs=[pl.BlockSpec((tb, G_DIM), fwd2), pl.BlockSpec((tb, G_DIM), bwd2)],
        scratch_shapes=[pltpu.VMEM((2 * NH, HD, HD), F32),
                        pltpu.VMEM((2, nc, NH, 2 * CHUNK, HD), F32),
                        pltpu.VMEM((2, nc, NH, CHUNK + HD, CHUNK), F32),
                        pltpu.VMEM((2, nc, NH, CHUNK, HD), F32)],
    )
    return pl.pallas_call(
        _gdn_kernel,
        grid_spec=grid_spec,
        out_shape=[jax.ShapeDtypeStruct((t, G_DIM), F32), jax.ShapeDtypeStruct((t, G_DIM), F32)],
        compiler_params=_cparams(("arbitrary",)),
        name="gdn",
    )(reset_f, reset_b, q, k, v, gcol, grow, gend, q, k, v, gcol, grow, gend)


def _fft1_kernel(x_ref, m1_ref, tc_ref, ts_ref, br_ref, bi_ref):
    s1 = x_ref.shape[1]
    a = jnp.dot(m1_ref[...], x_ref[0], preferred_element_type=F32)
    ar, ai = a[0:s1], a[s1:]
    tc, ts = tc_ref[...], ts_ref[...]
    br_ref[0] = ar * tc + ai * ts
    bi_ref[0] = ai * tc - ar * ts


def fft1_call(x3, m1, twc, tws, *, tcol):
    nseq, s1, cols = x3.shape
    blk = pl.BlockSpec((1, s1, tcol), lambda j, b: (b, 0, j))
    tw = pl.BlockSpec((s1, tcol), lambda j, b: (0, j))
    return pl.pallas_call(
        _fft1_kernel,
        grid=(cols // tcol, nseq),
        in_specs=[blk, pl.BlockSpec(m1.shape, lambda j, b: (0, 0)), tw, tw],
        out_specs=[blk, blk],
        out_shape=[jax.ShapeDtypeStruct(x3.shape, F32)] * 2,
        compiler_params=_cparams(("arbitrary", "arbitrary")),
        name="fft_stage1",
    )(x3, m1, twc, tws)


def _fft2_kernel(br_ref, bi_ref, mc_ref, m2_ref, o_ref):
    _, tk, s2, c = br_ref.shape
    nsplit = o_ref.shape[0]
    s2o = s2 // nsplit
    b = jnp.concatenate([br_ref[0].reshape(tk * s2, c), bi_ref[0].reshape(tk * s2, c)], axis=1)
    z = jnp.dot(b, mc_ref[...], preferred_element_type=F32)
    m2 = m2_ref[...]
    for kk in range(tk):
        zk = z[kk * s2:(kk + 1) * s2]
        x = jnp.dot(m2, jnp.concatenate([zk[:, 0:c], zk[:, c:]], axis=0),
                    preferred_element_type=F32)
        for sp in range(nsplit):
            o_ref[sp, kk] = x[sp * s2o:(sp + 1) * s2o]


def fft2_call(br4, bi4, mc, m2, *, tk, nsplit):
    nseq, s1, s2, c = br4.shape
    s2o = s2 // nsplit
    blk = pl.BlockSpec((1, tk, s2, c), lambda b, j: (b, j, 0, 0))
    return pl.pallas_call(
        _fft2_kernel,
        grid=(nseq, s1 // tk),
        in_specs=[blk, blk, pl.BlockSpec(mc.shape, lambda b, j: (0, 0)),
                  pl.BlockSpec(m2.shape, lambda b, j: (0, 0))],
        out_specs=pl.BlockSpec((nsplit, tk, s2o, c), lambda b, j: (b, j, 0, 0)),
        out_shape=jax.ShapeDtypeStruct((nseq * nsplit, s1, s2o, c), F32),
        compiler_params=_cparams(("arbitrary", "arbitrary")),
        name="fft_stage2",
    )(br4, bi4, mc, m2)


def _dft_tables(s):
    s1 = DFT1
    s2 = s // s1
    k = np.arange(s1)
    ang1 = 2.0 * np.pi * ((k[:, None] * k[None, :]) % s1) / s1
    sc = 1.0 / math.sqrt(s)
    m1 = np.concatenate([np.cos(ang1), -np.sin(ang1)], axis=0) * sc
    n2 = np.arange(s2)
    angt = 2.0 * np.pi * ((k[:, None] * n2[None, :]) % s) / s
    twc = np.repeat(np.cos(angt), F_DIM, axis=1)
    tws = np.repeat(np.sin(angt), F_DIM, axis=1)
    ang2 = 2.0 * np.pi * ((n2[:, None] * n2[None, :]) % s2) / s2
    m2 = np.concatenate([np.cos(ang2), np.sin(ang2)], axis=1)
    return (jnp.asarray(m1, F32), jnp.asarray(np.cos(angt), F32), jnp.asarray(np.sin(angt), F32),
            jnp.asarray(m2, F32))


def _channel_dft_matrix():
    g = np.arange(GROUP_DIM)
    ang = 2.0 * np.pi * ((g[:, None] * g[None, :]) % GROUP_DIM) / GROUP_DIM
    cg = np.kron(np.eye(N_GROUPS), np.cos(ang)) / math.sqrt(GROUP_DIM)
    sg = np.kron(np.eye(N_GROUPS), np.sin(ang)) / math.sqrt(GROUP_DIM)
    return jnp.asarray(np.block([[cg, -sg], [sg, cg]]), F32)


def fourier_mix_call(f, *, nseq, s, tcol, tk, nsplit):
    s1 = DFT1
    s2 = s // s1
    m1, tcs, tss, m2 = _dft_tables(s)
    twc = jnp.broadcast_to(tcs[:, :, None], (s1, s2, F_DIM)).reshape(s1, s2 * F_DIM)
    tws = jnp.broadcast_to(tss[:, :, None], (s1, s2, F_DIM)).reshape(s1, s2 * F_DIM)
    x3 = f.reshape(nseq, s1, s2 * F_DIM)
    br, bi = fft1_call(x3, m1, twc, tws, tcol=tcol)
    out = fft2_call(br.reshape(nseq, s1, s2, F_DIM), bi.reshape(nseq, s1, s2, F_DIM),
                    _channel_dft_matrix(), m2, tk=tk, nsplit=nsplit)
    return out.reshape(nseq * nsplit, s1, (s2 // nsplit) * F_DIM)


def _outproj_kernel(x_ref, mod_ref, of_ref, ob_ref, z_ref, fm_ref, gn_ref, w_ref, o_ref):
    tm = x_ref.shape[0]
    o = of_ref[...] + ob_ref[...]
    z = z_ref[...]
    gn = gn_ref[...]
    parts = []
    fm = fm_ref[...]
    parts.append(jnp.concatenate(
        [fm[:, j * F_DIM:(j + 1) * F_DIM] for j in range(tm // DFT1)], axis=0).astype(BF16))
    for h in range(NH):
        oh = o[:, h * HD:(h + 1) * HD]
        ms = jnp.mean(oh * oh, axis=-1, keepdims=True)
        y = (oh * lax.rsqrt(ms + NORM_EPS)) * gn
        parts.append((y * _silu(z[:, h * HD:(h + 1) * HD])).astype(BF16))
    mixed = jnp.concatenate(parts, axis=1)
    proj = jnp.dot(mixed, w_ref[...], preferred_element_type=F32)
    o_ref[...] = x_ref[...] + mod_ref[0, 2:3, :] * proj


def outproj_call(x, mod, o_f, o_b, z, fm2, gn, w, *, seg, tm):
    t, d = x.shape
    row = lambda i: (i, 0)
    const = lambda i: (0, 0)
    per_seg = seg // tm
    return pl.pallas_call(
        _outproj_kernel,
        grid=(t // tm,),
        in_specs=[
            pl.BlockSpec((tm, d), row),
            pl.BlockSpec((1, SUBLANES, d), lambda i: (i // per_seg, 0, 0)),
            pl.BlockSpec((tm, G_DIM), row),
            pl.BlockSpec((tm, G_DIM), row),
            pl.BlockSpec((tm, G_DIM), row),
            pl.BlockSpec((DFT1, (tm // DFT1) * F_DIM), lambda i: (i // per_seg, i % per_seg)),
            pl.BlockSpec((1, HD), const),
            pl.BlockSpec(w.shape, const),
        ],
        out_specs=pl.BlockSpec((tm, d), row),
        out_shape=jax.ShapeDtypeStruct((t, d), F32),
        compiler_params=_cparams(("arbitrary",)),
        name="outproj",
    )(x, mod, o_f, o_b, z, fm2, gn, w)


def _ffn_kernel(x_ref, mod_ref, nw_ref, wg_ref, wu_ref, wd_ref, o_ref, *, nsplit):
    x = x_ref[...]
    h = _mod_norm(x, nw_ref[...], mod_ref[0, 4:5, :], mod_ref[0, 3:4, :]).astype(BF16)
    dff = wg_ref.shape[1]
    cw = dff // nsplit
    acc = None
    for c in range(nsplit):
        g = jnp.dot(h, wg_ref[:, c * cw:(c + 1) * cw], preferred_element_type=F32)
        u = jnp.dot(h, wu_ref[:, c * cw:(c + 1) * cw], preferred_element_type=F32)
        hid = (_silu(g) * u).astype(BF16)
        part = jnp.dot(hid, wd_ref[c * cw:(c + 1) * cw, :], preferred_element_type=F32)
        acc = part if acc is None else acc + part
    o_ref[...] = x + mod_ref[0, 5:6, :] * acc


def ffn_call(x, mod, nw, wg, wu, wd, *, seg, tm):
    t, d = x.shape
    row = lambda i: (i, 0)
    const = lambda i: (0, 0)
    single = pl.Buffered(1)
    return pl.pallas_call(
        functools.partial(_ffn_kernel, nsplit=2),
        grid=(t // tm,),
        in_specs=[
            pl.BlockSpec((tm, d), row),
            pl.BlockSpec((1, SUBLANES, d), lambda i: (i * tm // seg, 0, 0)),
            pl.BlockSpec((1, d), const),
            pl.BlockSpec(wg.shape, const, pipeline_mode=single),
            pl.BlockSpec(wu.shape, const, pipeline_mode=single),
            pl.BlockSpec(wd.shape, const, pipeline_mode=single),
        ],
        out_specs=pl.BlockSpec((tm, d), row),
        out_shape=jax.ShapeDtypeStruct((t, d), F32),
        compiler_params=_cparams(("arbitrary",)),
        name="ffn",
    )(x, mod, nw, wg, wu, wd)


def _router_kernel(x_ref, mod_ref, nw_ref, wr_ref, tri_ref, oi_ref, op_ref, cnt_ref, carry_ref):
    i = pl.program_id(0)
    tm = x_ref.shape[0]

    @pl.when(i == 0)
    def _():
        carry_ref[...] = jnp.zeros_like(carry_ref)

    h = _mod_norm(x_ref[...], nw_ref[...], mod_ref[0, 4:5, :], mod_ref[0, 3:4, :])
    logits = jnp.dot(h.astype(BF16), wr_ref[...], preferred_element_type=F32)
    lane = lax.broadcasted_iota(jnp.int32, (tm, LANES), 1)
    logits = jnp.where(lane < N_EXPERTS, logits, NEG_BIG)
    l1 = jnp.max(logits, axis=-1, keepdims=True)
    i1 = jnp.min(jnp.where(logits == l1, lane, LANES), axis=-1, keepdims=True)
    rest = jnp.where(lane == i1, NEG_BIG, logits)
    l2 = jnp.max(rest, axis=-1, keepdims=True)
    i2 = jnp.min(jnp.where(rest == l2, lane, LANES), axis=-1, keepdims=True)
    e21 = jnp.exp(l2 - l1)
    p1 = 1.0 / (1.0 + e21)
    p2 = e21 * p1
    oh1 = lane == i1
    oh2 = lane == i2
    oh = jnp.where(oh1 | oh2, 1.0, 0.0).astype(BF16)
    before = jnp.dot(tri_ref[...], oh, preferred_element_type=F32) + carry_ref[0:1, :]
    r1 = jnp.sum(jnp.where(oh1, before, 0.0), axis=-1, keepdims=True).astype(jnp.int32)
    r2 = jnp.sum(jnp.where(oh2, before, 0.0), axis=-1, keepdims=True).astype(jnp.int32)
    oi_ref[...] = jnp.where(lane == 0, i1, jnp.where(lane == 1, i2, jnp.where(lane == 2, r1, r2)))
    op_ref[...] = jnp.where(lane == 0, p1, p2)
    new_carry = carry_ref[0:1, :] + jnp.sum(oh.astype(F32), axis=0, keepdims=True)
    carry_ref[...] = jnp.broadcast_to(new_carry, carry_ref.shape)
    cnt_ref[...] = jnp.broadcast_to(new_carry, cnt_ref.shape)


def router_call(x, mod, nw, wr, tri, *, seg, tm):
    t, d = x.shape
    row = lambda i: (i, 0)
    const = lambda i: (0, 0)
    return pl.pallas_call(
        _router_kernel,
        grid=(t // tm,),
        in_specs=[
            pl.BlockSpec((tm, d), row),
            pl.BlockSpec((1, SUBLANES, d), lambda i: (i * tm // seg, 0, 0)),
            pl.BlockSpec((1, d), const),
            pl.BlockSpec(wr.shape, const),
            pl.BlockSpec(tri.shape, const),
        ],
        out_specs=[pl.BlockSpec((tm, LANES), row), pl.BlockSpec((tm, LANES), row),
                   pl.BlockSpec((SUBLANES, LANES), const)],
        out_shape=[jax.ShapeDtypeStruct((t, LANES), jnp.int32), jax.ShapeDtypeStruct((t, LANES), F32),
                   jax.ShapeDtypeStruct((SUBLANES, LANES), F32)],
        scratch_shapes=[pltpu.VMEM((SUBLANES, LANES), F32)],
        compiler_params=_cparams(("arbitrary",)),
        name="router",
    )(x, mod, nw, wr, tri)


def _dispatch_kernel(pos_ref, x_ref, mod_ref, nw_ref, xs_in_ref, xs_ref, hbuf, sem):
    del xs_in_ref
    i = pl.program_id(0)
    last = pl.num_programs(0) - 1
    tm = x_ref.shape[0]
    buf = i % 2
    hbuf[buf] = _mod_norm(x_ref[...], nw_ref[...], mod_ref[0, 4:5, :], mod_ref[0, 3:4, :])

    def start(r, c):
        for slot in range(2):
            pltpu.make_async_copy(hbuf.at[buf, pl.ds(r, 1), :],
                                  xs_ref.at[pl.ds(pos_ref[0, 0, slot * tm + r], 1), :],
                                  sem.at[buf]).start(priority=slot)
        return c

    lax.fori_loop(0, tm, start, 0, unroll=8)

    def wait_rows(b):
        for _ in range(2):
            pltpu.make_async_copy(hbuf.at[b], xs_ref.at[pl.ds(0, tm), :], sem.at[b]).wait()

    @pl.when(i > 0)
    def _():
        wait_rows(1 - buf)

    @pl.when(i == last)
    def _():
        wait_rows(buf)


def dispatch_call(pos3, x, mod, nw, xs_zero, *, seg, tm):
    t, d = x.shape
    return pl.pallas_call(
        _dispatch_kernel,
        grid=(t // tm,),
        in_specs=[
            pl.BlockSpec((1, 1, 2 * tm), lambda i: (i, 0, 0), memory_space=pltpu.SMEM),
            pl.BlockSpec((tm, d), lambda i: (i, 0)),
            pl.BlockSpec((1, SUBLANES, d), lambda i: (i * tm // seg, 0, 0)),
            pl.BlockSpec((1, d), lambda i: (0, 0)),
            pl.BlockSpec(memory_space=pl.ANY),
        ],
        out_specs=pl.BlockSpec(memory_space=pl.ANY),
        out_shape=jax.ShapeDtypeStruct(xs_zero.shape, F32),
        scratch_shapes=[pltpu.VMEM((2, tm, d), F32), pltpu.SemaphoreType.DMA((2,))],
        input_output_aliases={4: 0},
        compiler_params=_cparams(("arbitrary",)),
        name="moe_dispatch",
    )(pos3, x, mod, nw, xs_zero)


def _expert_kernel(te_ref, nu_ref, xs_ref, wg_ref, wu_ref, wd_ref, y_ref, acc_ref):
    j = pl.program_id(0)
    half = pl.program_id(1)

    @pl.when(j < nu_ref[0])
    def _():
        h = xs_ref[...].astype(BF16)
        g = jnp.dot(h, wg_ref[0], preferred_element_type=F32)
        u = jnp.dot(h, wu_ref[0], preferred_element_type=F32)
        hid = (_silu(g) * u).astype(BF16)
        part = jnp.dot(hid, wd_ref[0], preferred_element_type=F32)

        @pl.when(half == 0)
        def _():
            acc_ref[...] = part

        @pl.when(half == 1)
        def _():
            y_ref[...] = acc_ref[...] + part

    @pl.when((j >= nu_ref[0]) & (half == 1))
    def _():
        y_ref[...] = jnp.zeros_like(y_ref)


def expert_call(tile_expert, n_used, xs, wg, wu, wd, *, tme):
    nr, d = xs.shape
    ntiles = nr // tme
    fh = wg.shape[2] // 2

    def jj(j, nu):
        return jnp.minimum(j, nu[0] - 1)

    def hh(j, hf, nu):
        return jnp.where(j < nu[0], hf, 1)

    grid_spec = pltpu.PrefetchScalarGridSpec(
        num_scalar_prefetch=2,
        grid=(ntiles, 2),
        in_specs=[
            pl.BlockSpec((tme, d), lambda j, hf, te, nu: (jj(j, nu), 0)),
            pl.BlockSpec((1, d, fh), lambda j, hf, te, nu: (te[jj(j, nu)], 0, hh(j, hf, nu))),
            pl.BlockSpec((1, d, fh), lambda j, hf, te, nu: (te[jj(j, nu)], 0, hh(j, hf, nu))),
            pl.BlockSpec((1, fh, d), lambda j, hf, te, nu: (te[jj(j, nu)], hh(j, hf, nu), 0)),
        ],
        out_specs=pl.BlockSpec((tme, d), lambda j, hf, te, nu: (j, 0)),
        scratch_shapes=[pltpu.VMEM((tme, d), F32)],
    )
    return pl.pallas_call(
        _expert_kernel,
        grid_spec=grid_spec,
        out_shape=jax.ShapeDtypeStruct((nr, d), F32),
        compiler_params=_cparams(("arbitrary", "arbitrary")),
        name="moe_experts",
    )(tile_expert, n_used, xs, wg, wu, wd)


def _combine_kernel(pos_ref, posn_ref, x_ref, mod_ref, p_ref, y_ref, o_ref, ybuf, sem):
    i = pl.program_id(0)
    last = pl.num_programs(0) - 1
    tm = x_ref.shape[0]
    buf = i % 2

    def gather(idx_ref, b):
        def start(r, c):
            for slot in range(2):
                pltpu.make_async_copy(y_ref.at[pl.ds(idx_ref[0, 0, slot * tm + r], 1), :],
                                      ybuf.at[b, slot, pl.ds(r, 1), :], sem.at[b]).start(priority=slot)
            return c

        lax.fori_loop(0, tm, start, 0, unroll=8)

    @pl.when(i == 0)
    def _():
        gather(pos_ref, buf)

    @pl.when(i < last)
    def _():
        gather(posn_ref, 1 - buf)

    for slot in range(2):
        pltpu.make_async_copy(y_ref.at[pl.ds(0, tm), :], ybuf.at[buf, slot], sem.at[buf]).wait()
    p = p_ref[...]
    f = p[:, 0:1] * ybuf[buf, 0] + p[:, 1:2] * ybuf[buf, 1]
    o_ref[...] = x_ref[...] + mod_ref[0, 5:6, :] * f


def combine_call(pos3, x, mod, p, y, *, seg, tm):
    t, d = x.shape
    nt = t // tm
    return pl.pallas_call(
        _combine_kernel,
        grid=(nt,),
        in_specs=[
            pl.BlockSpec((1, 1, 2 * tm), lambda i: (i, 0, 0), memory_space=pltpu.SMEM),
            pl.BlockSpec((1, 1, 2 * tm), lambda i: (jnp.minimum(i + 1, nt - 1), 0, 0),
                         memory_space=pltpu.SMEM),
            pl.BlockSpec((tm, d), lambda i: (i, 0)),
            pl.BlockSpec((1, SUBLANES, d), lambda i: (i * tm // seg, 0, 0)),
            pl.BlockSpec((tm, LANES), lambda i: (i, 0)),
            pl.BlockSpec(memory_space=pl.ANY),
        ],
        out_specs=pl.BlockSpec((tm, d), lambda i: (i, 0)),
        out_shape=jax.ShapeDtypeStruct((t, d), F32),
        scratch_shapes=[pltpu.VMEM((2, 2, tm, d), F32), pltpu.SemaphoreType.DMA((2,))],
        compiler_params=_cparams(("arbitrary",)),
        name="moe_combine",
    )(pos3, pos3, x, mod, p, y)


def moe_block(x, mod, nw, wr, wg, wu, wd, *, seg, tm, tmd, tme):
    t, d = x.shape
    wr_pad = jnp.zeros((d, LANES), BF16).at[:, :N_EXPERTS].set(wr.astype(BF16))
    tri = jnp.asarray(np.tril(np.ones((tm, tm), np.float32), k=-1), BF16)
    oi, op, cnt = router_call(x, mod, nw, wr_pad, tri, seg=seg, tm=tm)
    counts = cnt[0, :N_EXPERTS].astype(jnp.int32)
    padded = ((counts + tme - 1) // tme) * tme
    ends = jnp.cumsum(padded)
    starts = ends - padded
    nr = ((2 * t + N_EXPERTS * (tme - 1)) // tme) * tme
    ntiles = nr // tme
    n_used = (ends[-1] // tme).astype(jnp.int32).reshape(1)
    tile_start = jnp.arange(ntiles, dtype=jnp.int32) * tme
    tile_expert = jnp.minimum(jnp.sum(tile_start[:, None] >= ends[None, :], axis=1),
                              N_EXPERTS - 1).astype(jnp.int32)
    pos1 = starts[oi[:, 0]] + oi[:, 2]
    pos2 = starts[oi[:, 1]] + oi[:, 3]
    pos3 = jnp.concatenate([pos1.reshape(t // tmd, 1, tmd), pos2.reshape(t // tmd, 1, tmd)], axis=2)
    xs = dispatch_call(pos3, x, mod, nw, jnp.zeros((nr, d), F32), seg=seg, tm=tmd)
    y = expert_call(tile_expert, n_used, xs, wg, wu, wd, tme=tme)
    return combine_call(pos3, x, mod, op, y, seg=seg, tm=tmd)


def _final_kernel(x_ref, w_ref, o_ref):
    x = x_ref[...]
    ms = jnp.mean(x * x, axis=-1, keepdims=True)
    o_ref[...] = (x * lax.rsqrt(ms + NORM_EPS)) * w_ref[...]


def final_call(x, w, *, row0, nrows, tm):
    d = x.shape[1]
    off = row0 // tm
    return pl.pallas_call(
        _final_kernel,
        grid=(nrows // tm,),
        in_specs=[pl.BlockSpec((tm, d), lambda i: (i + off, 0)), pl.BlockSpec((1, d), lambda i: (0, 0))],
        out_specs=pl.BlockSpec((tm, d), lambda i: (i, 0)),
        out_shape=jax.ShapeDtypeStruct((nrows, d), F32),
        compiler_params=_cparams(("arbitrary",)),
        name="final_norm",
    )(x, w)


def _segment_flags(prompt_segs, sample_segs, seg, tile):
    per_seg = seg // tile
    nseg = prompt_segs + sample_segs
    first = np.zeros(nseg * per_seg, np.int32)
    last = np.zeros(nseg * per_seg, np.int32)
    first[0] = 1
    last[prompt_segs * per_seg - 1] = 1
    for s in range(prompt_segs, nseg):
        first[s * per_seg] = 1
        last[(s + 1) * per_seg - 1] = 1
    return first, last


def encoder_pair(x_prompt, x_sample, c_prompt, c_sample, w_ada, b_ada, norm_mix, norm_ffn, w_in, conv_w,
                 a_log, dt_bias, gdn_norm, w_out, w_ffn_gate, w_ffn_up, w_ffn_down, w_router,
                 w_exp_gate, w_exp_up, w_exp_down, norm_final, *, tm=512, tb=256, tmd=256, tme=512):
    bp, sp, d = x_prompt.shape
    bs, seg, _ = x_sample.shape
    assert bp == 1 and sp % seg == 0 and seg % tm == 0 and tm % DFT1 == 0
    depth = w_ada.shape[0]
    prompt_segs = sp // seg
    nseg = prompt_segs + bs
    t = nseg * seg
    x = jnp.concatenate([x_prompt.reshape(sp, d), x_sample.reshape(bs * seg, d)], axis=0)

    nrow = -(-(1 + bs) // SUBLANES) * SUBLANES
    c_all = jnp.zeros((nrow, d), F32).at[0:1].set(c_prompt).at[1:1 + bs].set(c_sample)
    ada = ada_call(c_all, w_ada, b_ada).reshape(depth, nrow, 6, d)
    seg_row = np.concatenate([np.zeros(prompt_segs, np.int32), 1 + np.arange(bs, dtype=np.int32)])
    mod_all = jnp.pad(ada[:, seg_row], ((0, 0), (0, 0), (0, SUBLANES - 6), (0, 0)))

    first_c, last_c = _segment_flags(prompt_segs, bs, seg, tm)
    first_g, last_g = _segment_flags(prompt_segs, bs, seg, tb)
    reset_f = jnp.asarray(first_g)
    reset_b = jnp.asarray(last_g[::-1].copy())
    first_c, last_c = jnp.asarray(first_c), jnp.asarray(last_c)

    off_f, off_qkv, off_z = F_DIM, F_DIM + QKV_DIM, F_DIM + QKV_DIM + G_DIM
    off_b = off_z + 2 * NH
    for l in range(depth):
        mod = mod_all[l]
        wl = w_in[l]
        wf = wl[:, :off_f].astype(BF16)
        wqkv = wl[:, off_f:off_qkv].astype(BF16)
        wz = wl[:, off_qkv:off_z].astype(BF16)
        wba = jnp.pad(wl[:, off_z:], ((0, 0), (0, LANES - 4 * NH))).astype(BF16)
        f, qkv, z, ba = inproj_call(x, mod, norm_mix[l].reshape(1, d), wf, wqkv, wz, wba, seg=seg, tm=tm)

        cw = jnp.pad(conv_w[l], ((0, SUBLANES - CONV_K), (0, 0)))
        gp = jnp.zeros((SUBLANES, LANES), F32)
        gp = gp.at[0, 2 * NH:4 * NH].set(dt_bias[l].reshape(-1)).at[1, 2 * NH:4 * NH].set(a_log[l].reshape(-1))
        q, k, v, gcol, grow, gend = conv_call(first_c, last_c, qkv, cw, ba, gp, tm=tm)
        o_f, o_b = gdn_call(reset_f, reset_b, q, k, v, gcol, grow, gend, tb=tb)

        s2p = sp // DFT1
        fm_p = fourier_mix_call(f[:sp], nseq=1, s=sp, tcol=min(2048, s2p * F_DIM),
                                tk=min(16, DFT1), nsplit=prompt_segs)
        s2s = seg // DFT1
        fm_s = fourier_mix_call(f[sp:], nseq=bs, s=seg, tcol=min(2048, s2s * F_DIM),
                                tk=min(64, DFT1), nsplit=1)
        fm2 = jnp.concatenate([fm_p, fm_s], axis=0).reshape(nseg * DFT1, (seg // DFT1) * F_DIM)

        x = outproj_call(x, mod, o_f, o_b, z, fm2, gdn_norm[l].reshape(1, HD), w_out[l].astype(BF16),
                         seg=seg, tm=tm)
        nw = norm_ffn[l].reshape(1, d)
        if l % 2 == 0:
            i = l // 2
            x = ffn_call(x, mod, nw, w_ffn_gate[i].astype(BF16), w_ffn_up[i].astype(BF16),
                         w_ffn_down[i].astype(BF16), seg=seg, tm=tm)
        else:
            i = l // 2
            x = moe_block(x, mod, nw, w_router[i], w_exp_gate[i].astype(BF16), w_exp_up[i].astype(BF16),
                          w_exp_down[i].astype(BF16), seg=seg, tm=tm, tmd=tmd, tme=tme)

    wn = norm_final.reshape(1, d)
    y_p = final_call(x, wn, row0=0, nrows=sp, tm=tm).reshape(bp, sp, d)
    y_s = final_call(x, wn, row0=sp, nrows=bs * seg, tm=tm).reshape(bs, seg, d)
    return y_p, y_s


def kernel(x_prompt, x_sample, c_prompt, c_sample, w_ada, b_ada, norm_mix, norm_ffn, w_in, conv_w, a_log,
           dt_bias, gdn_norm, w_out, w_ffn_gate, w_ffn_up, w_ffn_down, w_router, w_exp_gate, w_exp_up,
           w_exp_down, norm_final):
    return encoder_pair(x_prompt, x_sample, c_prompt, c_sample, w_ada, b_ada, norm_mix, norm_ffn, w_in,
                        conv_w, a_log, dt_bias, gdn_norm, w_out, w_ffn_gate, w_ffn_up, w_ffn_down,
                        w_router, w_exp_gate, w_exp_up, w_exp_down, norm_final)
```

```python
import functools
import math

import numpy as np
import jax
import jax.numpy as jnp
from jax import lax
from jax.experimental import pallas as pl
from jax.experimental.pallas import tpu as pltpu

F32 = jnp.float32
BF16 = jnp.bfloat16

D_MODEL = 1024
DEPTH = 4
N_GROUPS = 4
GROUP_DIM = 64
F_DIM = N_GROUPS * GROUP_DIM
HD = 128
NH = 6
G_DIM = NH * HD
QKV_DIM = 3 * G_DIM
CONV_K = 5
CHUNK = 64
PREP_CHUNKS = 2
D_FF = 2816
N_EXPERTS = 8
D_EXPERT = 3584
NORM_EPS = 1e-6

LANES = 128
SUBLANES = 8
MXU_COLS = 256
VMEM_LIMIT = 56 * 1024 * 1024
DFT1 = 128

GL_GAMMA, GL_BETA, GL_EG, GL_EGR = 0, 16, 32, 48
NEG_BIG = -1e30


def _cparams(sem):
    return pltpu.CompilerParams(dimension_semantics=sem, vmem_limit_bytes=VMEM_LIMIT)


def _mod_norm(x, nw, sc, sh):
    ms = jnp.mean(x * x, axis=-1, keepdims=True)
    y = x * lax.rsqrt(ms + NORM_EPS)
    return (y * nw) * (1.0 + sc) + sh


def _silu(x):
    return x * (1.0 / (1.0 + jnp.exp(-x)))


def _ada_kernel(c_ref, w_ref, b_ref, o_ref):
    c = _silu(c_ref[...])
    o_ref[0] = jnp.dot(c.astype(BF16), w_ref[0].astype(BF16), preferred_element_type=F32) + b_ref[0]


def ada_call(c_all, w_ada, b_ada):
    nrow = c_all.shape[0]
    depth, d, d6 = w_ada.shape
    tn = 1024
    return pl.pallas_call(
        _ada_kernel,
        grid=(depth, d6 // tn),
        in_specs=[
            pl.BlockSpec((nrow, d), lambda l, j: (0, 0)),
            pl.BlockSpec((1, d, tn), lambda l, j: (l, 0, j)),
            pl.BlockSpec((1, 1, tn), lambda l, j: (l, 0, j)),
        ],
        out_specs=pl.BlockSpec((1, nrow, tn), lambda l, j: (l, 0, j)),
        out_shape=jax.ShapeDtypeStruct((depth, nrow, d6), F32),
        compiler_params=_cparams(("arbitrary", "arbitrary")),
        name="ada",
    )(c_all, w_ada, b_ada.reshape(depth, 1, d6))


def _inproj_kernel(first_ref, last_ref,
                   x_ref, xp_ref, xn_ref, mod_ref, nw_ref, wf_ref, wqkv_ref, wz_ref, wba_ref, cw_ref, gp_ref,
                   f_ref, z_ref, q_ref, k_ref, v_ref, gcol_ref, grow_ref, gend_ref,
                   xe_ref):
    i = pl.program_id(0)
    tm = x_ref.shape[0]
    nc = tm // CHUNK
    halo = SUBLANES
    xcat = jnp.concatenate([xp_ref[...], x_ref[...], xn_ref[...]], axis=0)
    h = _mod_norm(xcat, nw_ref[...], mod_ref[0, 1:2, :], mod_ref[0, 0:1, :]).astype(BF16)
    pm = jnp.where(first_ref[i] == 1, 0.0, 1.0)
    nm = jnp.where(last_ref[i] == 1, 0.0, 1.0)
    outs = (q_ref, k_ref, v_ref)
    for cb in range(QKV_DIM // MXU_COLS):
        c0 = cb * MXU_COLS
        res = jnp.dot(h, wqkv_ref[:, c0:c0 + MXU_COLS], preferred_element_type=F32)
        xe_ref[0:halo, c0:c0 + MXU_COLS] = res[0:halo] * pm
        xe_ref[halo:halo + tm, c0:c0 + MXU_COLS] = res[halo:halo + tm]
        xe_ref[halo + tm:, c0:c0 + MXU_COLS] = res[halo + tm:] * nm
        for s in range(cb * (MXU_COLS // HD), (cb + 1) * (MXU_COLS // HD)):
            lo = s * HD
            acc = None
            for j in range(CONV_K):
                r0 = halo - CONV_K // 2 + j
                term = xe_ref[r0:r0 + tm, lo:lo + HD] * cw_ref[j:j + 1, lo:lo + HD]
                acc = term if acc is None else acc + term
            y = _silu(acc)
            which, head = divmod(s, NH)
            if which < 2:
                y = y * lax.rsqrt(jnp.sum(y * y, axis=-1, keepdims=True) + 1e-6)
            if which == 0:
                y = y * (HD ** -0.5)
            outs[which][:, head * HD:(head + 1) * HD] = y
    f_ref[...] = jnp.dot(h, wf_ref[...], preferred_element_type=F32)[halo:halo + tm]
    z_ref[...] = jnp.dot(h, wz_ref[...], preferred_element_type=F32)[halo:halo + tm]
    ba = jnp.dot(h, wba_ref[...], preferred_element_type=F32)[halo:halo + tm]

    lane = lax.broadcasted_iota(jnp.int32, (tm, LANES), 1)
    beta = 1.0 / (1.0 + jnp.exp(-ba))
    xs = ba + gp_ref[0:1, :]
    softplus = jnp.maximum(xs, 0.0) + jnp.log(1.0 + jnp.exp(-jnp.abs(xs)))
    g = -jnp.exp(gp_ref[1:2, :]) * softplus
    g = jnp.where((lane >= 2 * NH) & (lane < 4 * NH), g, 0.0)
    g = pltpu.roll(g, LANES - 2 * NH, axis=1)

    rowc = lax.broadcasted_iota(jnp.int32, (tm, LANES), 0) % CHUNK
    p = g
    sh = 1
    while sh < CHUNK:
        p = p + jnp.where(rowc >= sh, pltpu.roll(p, sh, axis=0), 0.0)
        sh *= 2
    g3 = g.reshape(nc, CHUNK, LANES)
    tot = jnp.broadcast_to(jnp.sum(g3, axis=1, keepdims=True), (nc, CHUNK, LANES)).reshape(tm, LANES)
    is_bwd = (lane >= NH) & (lane < 2 * NH)
    gamma = jnp.where(is_bwd, tot - p + g, p)
    eg = jnp.exp(gamma)
    egr = jnp.exp(tot - gamma)
    m12 = lane < 2 * NH
    gcol = (jnp.where(m12, gamma, 0.0)
            + pltpu.roll(jnp.where(m12, beta, 0.0), GL_BETA, axis=1)
            + pltpu.roll(jnp.where(m12, eg, 0.0), GL_EG, axis=1)
            + pltpu.roll(jnp.where(m12, egr, 0.0), GL_EGR, axis=1))
    gcol_ref[...] = gcol
    gam_t = jnp.where(m12, gamma, 0.0).T[0:2 * SUBLANES, :]
    g_t = g.T[0:2 * SUBLANES, :]
    for c in range(nc):
        grow_ref[c] = gam_t[:, c * CHUNK:(c + 1) * CHUNK]
        tc = jnp.sum(g_t[:, c * CHUNK:(c + 1) * CHUNK], axis=-1, keepdims=True)
        gend_ref[c] = jnp.exp(jnp.broadcast_to(tc, (2 * SUBLANES, LANES)))


def inproj_call(first, last, x, mod, nw, wf, wqkv, wz, wba, cw, gp, *, seg, tm):
    t, d = x.shape
    nblk8 = t // SUBLANES
    r8 = tm // SUBLANES
    nc = tm // CHUNK
    row = lambda i, f, l: (i, 0)
    const = lambda i, f, l: (0, 0)
    chunked = lambda i, f, l: (i, 0, 0)
    single = pl.Buffered(1)
    grid_spec = pltpu.PrefetchScalarGridSpec(
        num_scalar_prefetch=2,
        grid=(t // tm,),
        in_specs=[
            pl.BlockSpec((tm, d), row),
            pl.BlockSpec((SUBLANES, d), lambda i, f, l: (jnp.maximum(i * r8 - 1, 0), 0)),
            pl.BlockSpec((SUBLANES, d), lambda i, f, l: (jnp.minimum((i + 1) * r8, nblk8 - 1), 0)),
            pl.BlockSpec((1, SUBLANES, d), lambda i, f, l: (i * tm // seg, 0, 0)),
            pl.BlockSpec((1, d), const),
            pl.BlockSpec(wf.shape, const, pipeline_mode=single),
            pl.BlockSpec(wqkv.shape, const, pipeline_mode=single),
            pl.BlockSpec(wz.shape, const, pipeline_mode=single),
            pl.BlockSpec(wba.shape, const, pipeline_mode=single),
            pl.BlockSpec((SUBLANES, QKV_DIM), const),
            pl.BlockSpec((SUBLANES, LANES), const),
        ],
        out_specs=[
            pl.BlockSpec((tm, F_DIM), row),
            pl.BlockSpec((tm, G_DIM), row),
            pl.BlockSpec((tm, G_DIM), row),
            pl.BlockSpec((tm, G_DIM), row),
            pl.BlockSpec((tm, G_DIM), row),
            pl.BlockSpec((tm, LANES), row),
            pl.BlockSpec((nc, 2 * SUBLANES, CHUNK), chunked),
            pl.BlockSpec((nc, 2 * SUBLANES, LANES), chunked),
        ],
        scratch_shapes=[pltpu.VMEM((tm + 2 * SUBLANES, QKV_DIM), F32)],
    )
    return pl.pallas_call(
        _inproj_kernel,
        grid_spec=grid_spec,
        out_shape=[
            jax.ShapeDtypeStruct((t, F_DIM), F32),
            jax.ShapeDtypeStruct((t, G_DIM), F32),
            jax.ShapeDtypeStruct((t, G_DIM), F32),
            jax.ShapeDtypeStruct((t, G_DIM), F32),
            jax.ShapeDtypeStruct((t, G_DIM), F32),
            jax.ShapeDtypeStruct((t, LANES), F32),
            jax.ShapeDtypeStruct((t // CHUNK, 2 * SUBLANES, CHUNK), F32),
            jax.ShapeDtypeStruct((t // CHUNK, 2 * SUBLANES, LANES), F32),
        ],
        compiler_params=_cparams(("arbitrary",)),
        name="inproj_conv",
    )(first, last, x, x, x, mod, nw, wf, wqkv, wz, wba, cw, gp)


def _gdn_kernel(rf_ref, rb_ref,
                qf_ref, kf_ref, vf_ref, gcf_ref, grf_ref, gef_ref,
                qb_ref, kb_ref, vb_ref, gcb_ref, grb_ref, geb_ref,
                of_ref, ob_ref, s_ref, wq_ref, ab_ref, u_ref):
    i = pl.program_id(0)
    tb = qf_ref.shape[0]
    nc = tb // CHUNK

    @pl.when(rf_ref[i] == 1)
    def _():
        s_ref[0:NH] = jnp.zeros((NH, HD, HD), F32)

    @pl.when(rb_ref[i] == 1)
    def _():
        s_ref[NH:2 * NH] = jnp.zeros((NH, HD, HD), F32)

    row = lax.broadcasted_iota(jnp.int32, (CHUNK, CHUNK), 0)
    col = lax.broadcasted_iota(jnp.int32, (CHUNK, CHUNK), 1)
    eye = jnp.where(row == col, 1.0, 0.0).astype(F32)
    level_masks = []
    b = 1
    while b < CHUNK:
        level_masks.append((row // (2 * b) == col // (2 * b)) & (row // b != col // b))
        b *= 2
    dirs = (
        (qf_ref, kf_ref, vf_ref, gcf_ref, grf_ref, gef_ref, of_ref, row >= col, row > col),
        (qb_ref, kb_ref, vb_ref, gcb_ref, grb_ref, geb_ref, ob_ref, row <= col, row < col),
    )

    hds = [(d, h) for d in range(2) for h in range(NH)]
    probs = [(u, d, h) for u in range(PREP_CHUNKS) for d, h in hds]
    npr = len(probs)

    def prep_step(it, carry):
        cs = [it * PREP_CHUNKS + u for u in range(PREP_CHUNKS)]
        qs, ks, vs, cols, decs = [], [], [], [], []
        for u, d, h in probs:
            r0 = pl.multiple_of(cs[u] * CHUNK, CHUNK)
            q_ref, k_ref, v_ref, gc_ref, gr_ref = dirs[d][0:5]
            hd = d * NH + h
            lo = h * HD
            gc = gc_ref[pl.ds(r0, CHUNK), :]
            qs.append(q_ref[pl.ds(r0, CHUNK), lo:lo + HD])
            ks.append(k_ref[pl.ds(r0, CHUNK), lo:lo + HD])
            vs.append(v_ref[pl.ds(r0, CHUNK), lo:lo + HD])
            gam_c = gc[:, GL_GAMMA + hd:GL_GAMMA + hd + 1]
            cols.append((gc[:, GL_BETA + hd:GL_BETA + hd + 1], gc[:, GL_EG + hd:GL_EG + hd + 1],
                         gc[:, GL_EGR + hd:GL_EGR + hd + 1]))
            gam_r = gr_ref[cs[u]][hd:hd + 1, :]
            decs.append(jnp.exp(jnp.where(dirs[d][7], gam_c - gam_r, NEG_BIG)))
        kqs = [lax.dot_general(jnp.concatenate([ks[n], qs[n]], axis=0), ks[n],
                               (((1,), (1,)), ((), ())), preferred_element_type=F32)
               for n in range(npr)]
        a_s = [jnp.where(dirs[d][8], kqs[n][0:CHUNK] * cols[n][0] * decs[n], 0.0)
               for n, (u, d, h) in enumerate(probs)]
        for n, (u, d, h) in enumerate(probs):
            ab_ref[d, cs[u], h, 0:CHUNK, :] = kqs[n][CHUNK:] * decs[n]
            ab_ref[d, cs[u], h, CHUNK:, :] = (ks[n] * cols[n][2]).T
        ts = [eye - jnp.where(level_masks[0], a_s[n], 0.0) for n in range(npr)]
        for lm in level_masks[1:]:
            lts = [jnp.dot(jnp.where(lm, a_s[n], 0.0), ts[n], preferred_element_type=F32) for n in range(npr)]
            ts = [ts[n] - jnp.dot(ts[n], lts[n], preferred_element_type=F32) for n in range(npr)]
        uws =[jnp.dot(ts[n], jnp.concatenate([vs[n] * cols[n][0], ks[n] * (cols[n][0] * cols[n][1])], axis=1),
                       preferred_element_type=F32) for n in range(npr)]
        for n, (u, d, h) in enumerate(probs):
            u_ref[d, cs[u], h] = uws[n][:, 0:HD]
            wq_ref[d, cs[u], h, 0:CHUNK, :] = uws[n][:, HD:]
            wq_ref[d, cs[u], h, CHUNK:, :] = qs[n] * cols[n][1]
        return carry

    lax.fori_loop(0, nc // PREP_CHUNKS, prep_step, 0)

    def scan_step(c, carry):
        ccs = (c, nc - 1 - c)
        sts = [s_ref[d * NH + h] for d, h in hds]
        wqs = [jnp.dot(wq_ref[d, ccs[d], h], sts[n], preferred_element_type=F32)
               for n, (d, h) in enumerate(hds)]
        vns = [u_ref[d, ccs[d], h] - wqs[n][0:CHUNK] for n, (d, h) in enumerate(hds)]
        avs = [jnp.dot(ab_ref[d, ccs[d], h], vns[n], preferred_element_type=F32)
               for n, (d, h) in enumerate(hds)]
        for n, (d, h) in enumerate(hds):
            r0 = pl.multiple_of(ccs[d] * CHUNK, CHUNK)
            dirs[d][6][pl.ds(r0, CHUNK), h * HD:(h + 1) * HD] = wqs[n][CHUNK:] + avs[n][0:CHUNK]
            ge = dirs[d][5][ccs[d]]
            hd = d * NH + h
            s_ref[hd] = sts[n] * ge[hd:hd + 1, :] + avs[n][CHUNK:]
        return carry

    lax.fori_loop(0, nc, scan_step, 0)


def gdn_call(reset_f, reset_b, q, k, v, gcol, grow, gend, *, tb):
    t = q.shape[0]
    nb = t // tb
    nc = tb // CHUNK
    fwd2 = lambda i, a, b: (i, 0)
    bwd2 = lambda i, a, b: (nb - 1 - i, 0)
    fwd3 = lambda i, a, b: (i, 0, 0)
    bwd3 = lambda i, a, b: (nb - 1 - i, 0, 0)

    def specs(m2, m3):
        return [
            pl.BlockSpec((tb, G_DIM), m2), pl.BlockSpec((tb, G_DIM), m2), pl.BlockSpec((tb, G_DIM), m2),
            pl.BlockSpec((tb, LANES), m2),
            pl.BlockSpec((nc, 2 * SUBLANES, CHUNK), m3),
            pl.BlockSpec((nc, 2 * SUBLANES, LANES), m3),
        ]

    grid_spec = pltpu.PrefetchScalarGridSpec(
        num_scalar_prefetch=2,
        grid=(nb,),
        in_specs=specs(fwd2, fwd3) + specs(bwd2, bwd3),
        out_specs=[pl.BlockSpec((tb, G_DIM), fwd2), pl.BlockSpec((tb, G_DIM), bwd2)],
        scratch_shapes=[pltpu.VMEM((2 * NH, HD, HD), F32),
                        pltpu.VMEM((2, nc, NH, 2 * CHUNK, HD), F32),
                        pltpu.VMEM((2, nc, NH, CHUNK + HD, CHUNK), F32),
                        pltpu.VMEM((2, nc, NH, CHUNK, HD), F32)],
    )
    return pl.pallas_call(
        _gdn_kernel,
        grid_spec=grid_spec,
        out_shape=[jax.ShapeDtypeStruct((t, G_DIM), F32), jax.ShapeDtypeStruct((t, G_DIM), F32)],
        compiler_params=_cparams(("arbitrary",)),
        name="gdn",
    )(reset_f, reset_b, q, k, v, gcol, grow, gend, q, k, v, gcol, grow, gend)


def _fft1_kernel(x_ref, m1_ref, tc_ref, ts_ref, br_ref, bi_ref):
    s1 = x_ref.shape[1]
    a = jnp.dot(m1_ref[...], x_ref[0], preferred_element_type=F32)
    ar, ai = a[0:s1], a[s1:]
    tc, ts = tc_ref[...], ts_ref[...]
    br_ref[0] = ar * tc + ai * ts
    bi_ref[0] = ai * tc - ar * ts


def fft1_call(x3, m1, twc, tws, *, tcol):
    nseq, s1, cols = x3.shape
    blk = pl.BlockSpec((1, s1, tcol), lambda j, b: (b, 0, j))
    tw = pl.BlockSpec((s1, tcol), lambda j, b: (0, j))
    return pl.pallas_call(
        _fft1_kernel,
        grid=(cols // tcol, nseq),
        in_specs=[blk, pl.BlockSpec(m1.shape, lambda j, b: (0, 0)), tw, tw],
        out_specs=[blk, blk],
        out_shape=[jax.ShapeDtypeStruct(x3.shape, F32)] * 2,
        compiler_params=_cparams(("arbitrary", "arbitrary")),
        name="fft_stage1",
    )(x3, m1, twc, tws)


def _fft2_kernel(br_ref, bi_ref, mc_ref, m2_ref, o_ref):
    _, tk, s2, c = br_ref.shape
    nsplit = o_ref.shape[0]
    s2o = s2 // nsplit
    b = jnp.concatenate([br_ref[0].reshape(tk * s2, c), bi_ref[0].reshape(tk * s2, c)], axis=1)
    z = jnp.dot(b, mc_ref[...], preferred_element_type=F32)
    m2 = m2_ref[...]
    for kk in range(tk):
        zk = z[kk * s2:(kk + 1) * s2]
        x = jnp.dot(m2, jnp.concatenate([zk[:, 0:c], zk[:, c:]], axis=0),
                    preferred_element_type=F32)
        for sp in range(nsplit):
            o_ref[sp, kk] = x[sp * s2o:(sp + 1) * s2o]


def fft2_call(br4, bi4, mc, m2, *, tk, nsplit):
    nseq, s1, s2, c = br4.shape
    s2o = s2 // nsplit
    blk = pl.BlockSpec((1, tk, s2, c), lambda b, j: (b, j, 0, 0))
    return pl.pallas_call(
        _fft2_kernel,
        grid=(nseq, s1 // tk),
        in_specs=[blk, blk, pl.BlockSpec(mc.shape, lambda b, j: (0, 0)),
                  pl.BlockSpec(m2.shape, lambda b, j: (0, 0))],
        out_specs=pl.BlockSpec((nsplit, tk, s2o, c), lambda b, j: (b, j, 0, 0)),
        out_shape=jax.ShapeDtypeStruct((nseq * nsplit, s1, s2o, c), F32),
        compiler_params=_cparams(("arbitrary", "arbitrary")),
        name="fft_stage2",
    )(br4, bi4, mc, m2)


def _dft_tables(s):
    s1 = DFT1
    s2 = s // s1
    k = np.arange(s1)
    ang1 = 2.0 * np.pi * ((k[:, None] * k[None, :]) % s1) / s1
    sc = 1.0 / math.sqrt(s)
    m1 = np.concatenate([np.cos(ang1), -np.sin(ang1)], axis=0) * sc
    n2 = np.arange(s2)
    angt = 2.0 * np.pi * ((k[:, None] * n2[None, :]) % s) / s
    twc = np.repeat(np.cos(angt), F_DIM, axis=1)
    tws = np.repeat(np.sin(angt), F_DIM, axis=1)
    ang2 = 2.0 * np.pi * ((n2[:, None] * n2[None, :]) % s2) / s2
    m2 = np.concatenate([np.cos(ang2), np.sin(ang2)], axis=1)
    return (jnp.asarray(m1, F32), jnp.asarray(np.cos(angt), F32), jnp.asarray(np.sin(angt), F32),
            jnp.asarray(m2, F32))


def _channel_dft_matrix():
    g = np.arange(GROUP_DIM)
    ang = 2.0 * np.pi * ((g[:, None] * g[None, :]) % GROUP_DIM) / GROUP_DIM
    cg = np.kron(np.eye(N_GROUPS), np.cos(ang)) / math.sqrt(GROUP_DIM)
    sg = np.kron(np.eye(N_GROUPS), np.sin(ang)) / math.sqrt(GROUP_DIM)
    return jnp.asarray(np.block([[cg, -sg], [sg, cg]]), F32)


def fourier_mix_call(f, *, nseq, s, tcol, tk, nsplit):
    s1 = DFT1
    s2 = s // s1
    m1, tcs, tss, m2 = _dft_tables(s)
    twc = jnp.broadcast_to(tcs[:, :, None], (s1, s2, F_DIM)).reshape(s1, s2 * F_DIM)
    tws = jnp.broadcast_to(tss[:, :, None], (s1, s2, F_DIM)).reshape(s1, s2 * F_DIM)
    x3 = f.reshape(nseq, s1, s2 * F_DIM)
    br, bi = fft1_call(x3, m1, twc, tws, tcol=tcol)
    out = fft2_call(br.reshape(nseq, s1, s2, F_DIM), bi.reshape(nseq, s1, s2, F_DIM),
                    _channel_dft_matrix(), m2, tk=tk, nsplit=nsplit)
    return out.reshape(nseq * nsplit, s1, (s2 // nsplit) * F_DIM)


def _outproj_kernel(x_ref, mod_ref, of_ref, ob_ref, z_ref, fm_ref, gn_ref, w_ref, o_ref):
    tm = x_ref.shape[0]
    o = of_ref[...] + ob_ref[...]
    z = z_ref[...]
    gn = gn_ref[...]
    parts = []
    fm = fm_ref[...]
    parts.append(jnp.concatenate(
        [fm[:, j * F_DIM:(j + 1) * F_DIM] for j in range(tm // DFT1)], axis=0).astype(BF16))
    for h in range(NH):
        oh = o[:, h * HD:(h + 1) * HD]
        ms = jnp.mean(oh * oh, axis=-1, keepdims=True)
        y = (oh * lax.rsqrt(ms + NORM_EPS)) * gn
        parts.append((y * _silu(z[:, h * HD:(h + 1) * HD])).astype(BF16))
    mixed = jnp.concatenate(parts, axis=1)
    proj = jnp.dot(mixed, w_ref[...], preferred_element_type=F32)
    o_ref[...] = x_ref[...] + mod_ref[0, 2:3, :] * proj


def outproj_call(x, mod, o_f, o_b, z, fm2, gn, w, *, seg, tm):
    t, d = x.shape
    row = lambda i: (i, 0)
    const = lambda i: (0, 0)
    per_seg = seg // tm
    return pl.pallas_call(
        _outproj_kernel,
        grid=(t // tm,),
        in_specs=[
            pl.BlockSpec((tm, d), row),
            pl.BlockSpec((1, SUBLANES, d), lambda i: (i // per_seg, 0, 0)),
            pl.BlockSpec((tm, G_DIM), row),
            pl.BlockSpec((tm, G_DIM), row),
            pl.BlockSpec((tm, G_DIM), row),
            pl.BlockSpec((DFT1, (tm // DFT1) * F_DIM), lambda i: (i // per_seg, i % per_seg)),
            pl.BlockSpec((1, HD), const),
            pl.BlockSpec(w.shape, const),
        ],
        out_specs=pl.BlockSpec((tm, d), row),
        out_shape=jax.ShapeDtypeStruct((t, d), F32),
        compiler_params=_cparams(("arbitrary",)),
        name="outproj",
    )(x, mod, o_f, o_b, z, fm2, gn, w)


def _ffn_kernel(x_ref, mod_ref, nw_ref, wg_ref, wu_ref, wd_ref, o_ref, *, nsplit):
    x = x_ref[...]
    h = _mod_norm(x, nw_ref[...], mod_ref[0, 4:5, :], mod_ref[0, 3:4, :]).astype(BF16)
    dff = wg_ref.shape[1]
    cw = dff // nsplit
    acc = None
    for c in range(nsplit):
        g = jnp.dot(h, wg_ref[:, c * cw:(c + 1) * cw], preferred_element_type=F32)
        u = jnp.dot(h, wu_ref[:, c * cw:(c + 1) * cw], preferred_element_type=F32)
        hid = (_silu(g) * u).astype(BF16)
        part = jnp.dot(hid, wd_ref[c * cw:(c + 1) * cw, :], preferred_element_type=F32)
        acc = part if acc is None else acc + part
    o_ref[...] = x + mod_ref[0, 5:6, :] * acc


def ffn_call(x, mod, nw, wg, wu, wd, *, seg, tm):
    t, d = x.shape
    row = lambda i: (i, 0)
    const = lambda i: (0, 0)
    single = pl.Buffered(1)
    return pl.pallas_call(
        functools.partial(_ffn_kernel, nsplit=2),
        grid=(t // tm,),
        in_specs=[
            pl.BlockSpec((tm, d), row),
            pl.BlockSpec((1, SUBLANES, d), lambda i: (i * tm // seg, 0, 0)),
            pl.BlockSpec((1, d), const),
            pl.BlockSpec(wg.shape, const, pipeline_mode=single),
            pl.BlockSpec(wu.shape, const, pipeline_mode=single),
            pl.BlockSpec(wd.shape, const, pipeline_mode=single),
        ],
        out_specs=pl.BlockSpec((tm, d), row),
        out_shape=jax.ShapeDtypeStruct((t, d), F32),
        compiler_params=_cparams(("arbitrary",)),
        name="ffn",
    )(x, mod, nw, wg, wu, wd)


def _router_kernel(x_ref, mod_ref, nw_ref, wr_ref, tri_ref, oi_ref, op_ref, cnt_ref, carry_ref):
    i = pl.program_id(0)
    tm = x_ref.shape[0]

    @pl.when(i == 0)
    def _():
        carry_ref[...] = jnp.zeros_like(carry_ref)

    h = _mod_norm(x_ref[...], nw_ref[...], mod_ref[0, 4:5, :], mod_ref[0, 3:4, :])
    logits = jnp.dot(h.astype(BF16), wr_ref[...], preferred_element_type=F32)
    lane = lax.broadcasted_iota(jnp.int32, (tm, LANES), 1)
    logits = jnp.where(lane < N_EXPERTS, logits, NEG_BIG)
    l1 = jnp.max(logits, axis=-1, keepdims=True)
    i1 = jnp.min(jnp.where(logits == l1, lane, LANES), axis=-1, keepdims=True)
    rest = jnp.where(lane == i1, NEG_BIG, logits)
    l2 = jnp.max(rest, axis=-1, keepdims=True)
    i2 = jnp.min(jnp.where(rest == l2, lane, LANES), axis=-1, keepdims=True)
    e21 = jnp.exp(l2 - l1)
    p1 = 1.0 / (1.0 + e21)
    p2 = e21 * p1
    oh1 = lane == i1
    oh2 = lane == i2
    oh = jnp.where(oh1 | oh2, 1.0, 0.0).astype(BF16)
    before = jnp.dot(tri_ref[...], oh, preferred_element_type=F32) + carry_ref[0:1, :]
    r1 = jnp.sum(jnp.where(oh1, before, 0.0), axis=-1, keepdims=True).astype(jnp.int32)
    r2 = jnp.sum(jnp.where(oh2, before, 0.0), axis=-1, keepdims=True).astype(jnp.int32)
    oi_ref[...] = jnp.where(lane == 0, i1, jnp.where(lane == 1, i2, jnp.where(lane == 2, r1, r2)))
    op_ref[...] = jnp.where(lane == 0, p1, p2)
    new_carry = carry_ref[0:1, :] + jnp.sum(oh.astype(F32), axis=0, keepdims=True)
    carry_ref[...] = jnp.broadcast_to(new_carry, carry_ref.shape)
    cnt_ref[...] = jnp.broadcast_to(new_carry, cnt_ref.shape)


def router_call(x, mod, nw, wr, tri, *, seg, tm):
    t, d = x.shape
    row = lambda i: (i, 0)
    const = lambda i: (0, 0)
    return pl.pallas_call(
        _router_kernel,
        grid=(t // tm,),
        in_specs=[
            pl.BlockSpec((tm, d), row),
            pl.BlockSpec((1, SUBLANES, d), lambda i: (i * tm // seg, 0, 0)),
            pl.BlockSpec((1, d), const),
            pl.BlockSpec(wr.shape, const),
            pl.BlockSpec(tri.shape, const),
        ],
        out_specs=[pl.BlockSpec((tm, LANES), row), pl.BlockSpec((tm, LANES), row),
                   pl.BlockSpec((SUBLANES, LANES), const)],
        out_shape=[jax.ShapeDtypeStruct((t, LANES), jnp.int32), jax.ShapeDtypeStruct((t, LANES), F32),
                   jax.ShapeDtypeStruct((SUBLANES, LANES), F32)],
        scratch_shapes=[pltpu.VMEM((SUBLANES, LANES), F32)],
        compiler_params=_cparams(("arbitrary",)),
        name="router",
    )(x, mod, nw, wr, tri)


def _dispatch_kernel(pos_ref, x_ref, mod_ref, nw_ref, xs_in_ref, xs_ref, hbuf, sem):
    del xs_in_ref
    i = pl.program_id(0)
    last = pl.num_programs(0) - 1
    tm = x_ref.shape[0]
    buf = i % 2
    hbuf[buf] = _mod_norm(x_ref[...], nw_ref[...], mod_ref[0, 4:5, :], mod_ref[0, 3:4, :])

    def start(r, c):
        for slot in range(2):
            pltpu.make_async_copy(hbuf.at[buf, pl.ds(r, 1), :],
                                  xs_ref.at[pl.ds(pos_ref[0, 0, slot * tm + r], 1), :],
                                  sem.at[buf]).start(priority=slot)
        return c

    lax.fori_loop(0, tm, start, 0, unroll=8)

    def wait_rows(b):
        for _ in range(2):
            pltpu.make_async_copy(hbuf.at[b], xs_ref.at[pl.ds(0, tm), :], sem.at[b]).wait()

    @pl.when(i > 0)
    def _():
        wait_rows(1 - buf)

    @pl.when(i == last)
    def _():
        wait_rows(buf)


def dispatch_call(pos3, x, mod, nw, xs_zero, *, seg, tm):
    t, d = x.shape
    return pl.pallas_call(
        _dispatch_kernel,
        grid=(t // tm,),
        in_specs=[
            pl.BlockSpec((1, 1, 2 * tm), lambda i: (i, 0, 0), memory_space=pltpu.SMEM),
            pl.BlockSpec((tm, d), lambda i: (i, 0)),
            pl.BlockSpec((1, SUBLANES, d), lambda i: (i * tm // seg, 0, 0)),
            pl.BlockSpec((1, d), lambda i: (0, 0)),
            pl.BlockSpec(memory_space=pl.ANY),
        ],
        out_specs=pl.BlockSpec(memory_space=pl.ANY),
        out_shape=jax.ShapeDtypeStruct(xs_zero.shape, F32),
        scratch_shapes=[pltpu.VMEM((2, tm, d), F32), pltpu.SemaphoreType.DMA((2,))],
        input_output_aliases={4: 0},
        compiler_params=_cparams(("arbitrary",)),
        name="moe_dispatch",
    )(pos3, x, mod, nw, xs_zero)


def _expert_kernel(te_ref, nu_ref, xs_ref, wg_ref, wu_ref, wd_ref, y_ref, acc_ref):
    j = pl.program_id(0)
    half = pl.program_id(1)

    @pl.when(j < nu_ref[0])
    def _():
        h = xs_ref[...].astype(BF16)
        g = jnp.dot(h, wg_ref[0], preferred_element_type=F32)
        u = jnp.dot(h, wu_ref[0], preferred_element_type=F32)
        hid = (_silu(g) * u).astype(BF16)
        part = jnp.dot(hid, wd_ref[0], preferred_element_type=F32)

        @pl.when(half == 0)
        def _():
            acc_ref[...] = part

        @pl.when(half == 1)
        def _():
            y_ref[...] = acc_ref[...] + part

    @pl.when((j >= nu_ref[0]) & (half == 1))
    def _():
        y_ref[...] = jnp.zeros_like(y_ref)


def expert_call(tile_expert, n_used, xs, wg, wu, wd, *, tme):
    nr, d = xs.shape
    ntiles = nr // tme
    fh = wg.shape[2] // 2

    def jj(j, nu):
        return jnp.minimum(j, nu[0] - 1)

    def hh(j, hf, nu):
        return jnp.where(j < nu[0], hf, 1)

    grid_spec = pltpu.PrefetchScalarGridSpec(
        num_scalar_prefetch=2,
        grid=(ntiles, 2),
        in_specs=[
            pl.BlockSpec((tme, d), lambda j, hf, te, nu: (jj(j, nu), 0)),
            pl.BlockSpec((1, d, fh), lambda j, hf, te, nu: (te[jj(j, nu)], 0, hh(j, hf, nu))),
            pl.BlockSpec((1, d, fh), lambda j, hf, te, nu: (te[jj(j, nu)], 0, hh(j, hf, nu))),
            pl.BlockSpec((1, fh, d), lambda j, hf, te, nu: (te[jj(j, nu)], hh(j, hf, nu), 0)),
        ],
        out_specs=pl.BlockSpec((tme, d), lambda j, hf, te, nu: (j, 0)),
        scratch_shapes=[pltpu.VMEM((tme, d), F32)],
    )
    return pl.pallas_call(
        _expert_kernel,
        grid_spec=grid_spec,
        out_shape=jax.ShapeDtypeStruct((nr, d), F32),
        compiler_params=_cparams(("arbitrary", "arbitrary")),
        name="moe_experts",
    )(tile_expert, n_used, xs, wg, wu, wd)


def _combine_kernel(pos_ref, posn_ref, x_ref, mod_ref, p_ref, y_ref, o_ref, ybuf, sem):
    i = pl.program_id(0)
    last = pl.num_programs(0) - 1
    tm = x_ref.shape[0]
    buf = i % 2

    def gather(idx_ref, b):
        def start(r, c):
            for slot in range(2):
                pltpu.make_async_copy(y_ref.at[pl.ds(idx_ref[0, 0, slot * tm + r], 1), :],
                                      ybuf.at[b, slot, pl.ds(r, 1), :], sem.at[b]).start(priority=slot)
            return c

        lax.fori_loop(0, tm, start, 0, unroll=8)

    @pl.when(i == 0)
    def _():
        gather(pos_ref, buf)

    @pl.when(i < last)
    def _():
        gather(posn_ref, 1 - buf)

    for slot in range(2):
        pltpu.make_async_copy(y_ref.at[pl.ds(0, tm), :], ybuf.at[buf, slot], sem.at[buf]).wait()
    p = p_ref[...]
    f = p[:, 0:1] * ybuf[buf, 0] + p[:, 1:2] * ybuf[buf, 1]
    o_ref[...] = x_ref[...] + mod_ref[0, 5:6, :] * f


def combine_call(pos3, x, mod, p, y, *, seg, tm):
    t, d = x.shape
    nt = t // tm
    return pl.pallas_call(
        _combine_kernel,
        grid=(nt,),
        in_specs=[
            pl.BlockSpec((1, 1, 2 * tm), lambda i: (i, 0, 0), memory_space=pltpu.SMEM),
            pl.BlockSpec((1, 1, 2 * tm), lambda i: (jnp.minimum(i + 1, nt - 1), 0, 0),
                         memory_space=pltpu.SMEM),
            pl.BlockSpec((tm, d), lambda i: (i, 0)),
            pl.BlockSpec((1, SUBLANES, d), lambda i: (i * tm // seg, 0, 0)),
            pl.BlockSpec((tm, LANES), lambda i: (i, 0)),
            pl.BlockSpec(memory_space=pl.ANY),
        ],
        out_specs=pl.BlockSpec((tm, d), lambda i: (i, 0)),
        out_shape=jax.ShapeDtypeStruct((t, d), F32),
        scratch_shapes=[pltpu.VMEM((2, 2, tm, d), F32), pltpu.SemaphoreType.DMA((2,))],
        compiler_params=_cparams(("arbitrary",)),
        name="moe_combine",
    )(pos3, pos3, x, mod, p, y)


def moe_block(x, mod, nw, wr, wg, wu, wd, *, seg, tm, tmd, tme):
    t, d = x.shape
    wr_pad = jnp.zeros((d, LANES), BF16).at[:, :N_EXPERTS].set(wr.astype(BF16))
    tri = jnp.asarray(np.tril(np.ones((tm, tm), np.float32), k=-1), BF16)
    oi, op, cnt = router_call(x, mod, nw, wr_pad, tri, seg=seg, tm=tm)
    counts = cnt[0, :N_EXPERTS].astype(jnp.int32)
    padded = ((counts + tme - 1) // tme) * tme
    ends = jnp.cumsum(padded)
    starts = ends - padded
    nr = ((2 * t + N_EXPERTS * (tme - 1)) // tme) * tme
    ntiles = nr // tme
    n_used = (ends[-1] // tme).astype(jnp.int32).reshape(1)
    tile_start = jnp.arange(ntiles, dtype=jnp.int32) * tme
    tile_expert = jnp.minimum(jnp.sum(tile_start[:, None] >= ends[None, :], axis=1),
                              N_EXPERTS - 1).astype(jnp.int32)
    pos1 = starts[oi[:, 0]] + oi[:, 2]
    pos2 = starts[oi[:, 1]] + oi[:, 3]
    pos3 = jnp.concatenate([pos1.reshape(t // tmd, 1, tmd), pos2.reshape(t // tmd, 1, tmd)], axis=2)
    xs = dispatch_call(pos3, x, mod, nw, jnp.zeros((nr, d), F32), seg=seg, tm=tmd)
    y = expert_call(tile_expert, n_used, xs, wg, wu, wd, tme=tme)
    return combine_call(pos3, x, mod, op, y, seg=seg, tm=tmd)


def _final_kernel(x_ref, w_ref, o_ref):
    x = x_ref[...]
    ms = jnp.mean(x * x, axis=-1, keepdims=True)
    o_ref[...] = (x * lax.rsqrt(ms + NORM_EPS)) * w_ref[...]


def final_call(x, w, *, row0, nrows, tm):
    d = x.shape[1]
    off = row0 // tm
    return pl.pallas_call(
        _final_kernel,
        grid=(nrows // tm,),
        in_specs=[pl.BlockSpec((tm, d), lambda i: (i + off, 0)), pl.BlockSpec((1, d), lambda i: (0, 0))],
        out_specs=pl.BlockSpec((tm, d), lambda i: (i, 0)),
        out_shape=jax.ShapeDtypeStruct((nrows, d), F32),
        compiler_params=_cparams(("arbitrary",)),
        name="final_norm",
    )(x, w)


def _segment_flags(prompt_segs, sample_segs, seg, tile):
    per_seg = seg // tile
    nseg = prompt_segs + sample_segs
    first = np.zeros(nseg * per_seg, np.int32)
    last = np.zeros(nseg * per_seg, np.int32)
    first[0] = 1
    last[prompt_segs * per_seg - 1] = 1
    for s in range(prompt_segs, nseg):
        first[s * per_seg] = 1
        last[(s + 1) * per_seg - 1] = 1
    return first, last


def encoder_pair(x_prompt, x_sample, c_prompt, c_sample, w_ada, b_ada, norm_mix, norm_ffn, w_in, conv_w,
                 a_log, dt_bias, gdn_norm, w_out, w_ffn_gate, w_ffn_up, w_ffn_down, w_router,
                 w_exp_gate, w_exp_up, w_exp_down, norm_final, *, tm=512, tb=256, tmd=256, tme=512):
    bp, sp, d = x_prompt.shape
    bs, seg, _ = x_sample.shape
    assert bp == 1 and sp % seg == 0 and seg % tm == 0 and tm % DFT1 == 0
    depth = w_ada.shape[0]
    prompt_segs = sp // seg
    nseg = prompt_segs + bs
    t = nseg * seg
    x = jnp.concatenate([x_prompt.reshape(sp, d), x_sample.reshape(bs * seg, d)], axis=0)

    nrow = -(-(1 + bs) // SUBLANES) * SUBLANES
    c_all = jnp.zeros((nrow, d), F32).at[0:1].set(c_prompt).at[1:1 + bs].set(c_sample)
    ada = ada_call(c_all, w_ada, b_ada).reshape(depth, nrow, 6, d)
    seg_row = np.concatenate([np.zeros(prompt_segs, np.int32), 1 + np.arange(bs, dtype=np.int32)])
    mod_all = jnp.pad(ada[:, seg_row], ((0, 0), (0, 0), (0, SUBLANES - 6), (0, 0)))

    first_c, last_c = _segment_flags(prompt_segs, bs, seg, tm)
    first_g, last_g = _segment_flags(prompt_segs, bs, seg, tb)
    reset_f = jnp.asarray(first_g)
    reset_b = jnp.asarray(last_g[::-1].copy())
    first_c, last_c = jnp.asarray(first_c), jnp.asarray(last_c)

    off_f, off_qkv, off_z = F_DIM, F_DIM + QKV_DIM, F_DIM + QKV_DIM + G_DIM
    off_b = off_z + 2 * NH
    for l in range(depth):
        mod = mod_all[l]
        wl = w_in[l]
        wf = wl[:, :off_f].astype(BF16)
        wqkv = wl[:, off_f:off_qkv].astype(BF16)
        wz = wl[:, off_qkv:off_z].astype(BF16)
        wba = jnp.pad(wl[:, off_z:], ((0, 0), (0, LANES - 4 * NH))).astype(BF16)
        cw = jnp.pad(conv_w[l], ((0, SUBLANES - CONV_K), (0, 0)))
        gp = jnp.zeros((SUBLANES, LANES), F32)
        gp = gp.at[0, 2 * NH:4 * NH].set(dt_bias[l].reshape(-1)).at[1, 2 * NH:4 * NH].set(a_log[l].reshape(-1))
        f, z, q, k, v, gcol, grow, gend = inproj_call(first_c, last_c, x, mod, norm_mix[l].reshape(1, d),
                                                      wf, wqkv, wz, wba, cw, gp, seg=seg, tm=tm)
        o_f, o_b = gdn_call(reset_f, reset_b, q, k, v, gcol, grow, gend, tb=tb)

        s2p = sp // DFT1
        fm_p = fourier_mix_call(f[:sp], nseq=1, s=sp, tcol=min(2048, s2p * F_DIM),
                                tk=min(16, DFT1), nsplit=prompt_segs)
        s2s = seg // DFT1
        fm_s = fourier_mix_call(f[sp:], nseq=bs, s=seg, tcol=min(2048, s2s * F_DIM),
                                tk=min(64, DFT1), nsplit=1)
        fm2 = jnp.concatenate([fm_p, fm_s], axis=0).reshape(nseg * DFT1, (seg // DFT1) * F_DIM)

        x = outproj_call(x, mod, o_f, o_b, z, fm2, gdn_norm[l].reshape(1, HD), w_out[l].astype(BF16),
                         seg=seg, tm=tm)
        nw = norm_ffn[l].reshape(1, d)
        if l % 2 == 0:
            i = l // 2
            x = ffn_call(x, mod, nw, w_ffn_gate[i].astype(BF16), w_ffn_up[i].astype(BF16),
                         w_ffn_down[i].astype(BF16), seg=seg, tm=tm)
        else:
            i = l // 2
            x = moe_block(x, mod, nw, w_router[i], w_exp_gate[i].astype(BF16), w_exp_up[i].astype(BF16),
                          w_exp_down[i].astype(BF16), seg=seg, tm=tm, tmd=tmd, tme=tme)

    wn = norm_final.reshape(1, d)
    y_p = final_call(x, wn, row0=0, nrows=sp, tm=tm).reshape(bp, sp, d)
    y_s = final_call(x, wn, row0=sp, nrows=bs * seg, tm=tm).reshape(bs, seg, d)
    return y_p, y_s


def kernel(x_prompt, x_sample, c_prompt, c_sample, w_ada, b_ada, norm_mix, norm_ffn, w_in, conv_w, a_log,
           dt_bias, gdn_norm, w_out, w_ffn_gate, w_ffn_up, w_ffn_down, w_router, w_exp_gate, w_exp_up,
           w_exp_down, norm_final):
    return encoder_pair(x_prompt, x_sample, c_prompt, c_sample, w_ada, b_ada, norm_mix, norm_ffn, w_in,
                        conv_w, a_log, dt_bias, gdn_norm, w_out, w_ffn_gate, w_ffn_up, w_ffn_down,
                        w_router, w_exp_gate, w_exp_up, w_exp_down, norm_final)
```

```python
import functools
import math

import numpy as np
import jax
import jax.numpy as jnp
from jax import lax
from jax.experimental import pallas as pl
from jax.experimental.pallas import tpu as pltpu

F32 = jnp.float32
BF16 = jnp.bfloat16

D_MODEL = 1024
DEPTH = 4
N_GROUPS = 4
GROUP_DIM = 64
F_DIM = N_GROUPS * GROUP_DIM
HD = 128
NH = 6
G_DIM = NH * HD
QKV_DIM = 3 * G_DIM
CONV_K = 5
CHUNK = 64
PREP_CHUNKS = 2
D_FF = 2816
N_EXPERTS = 8
D_EXPERT = 3584
NORM_EPS = 1e-6

LANES = 128
SUBLANES = 8
MXU_COLS = 256
VMEM_LIMIT = 56 * 1024 * 1024
DFT1 = 128

GL_GAMMA, GL_BETA, GL_EG, GL_EGR = 0, 16, 32, 48
NEG_BIG = -1e30


def _cparams(sem):
    return pltpu.CompilerParams(dimension_semantics=sem, vmem_limit_bytes=VMEM_LIMIT)


def _mod_norm(x, nw, sc, sh):
    ms = jnp.mean(x * x, axis=-1, keepdims=True)
    y = x * lax.rsqrt(ms + NORM_EPS)
    return (y * nw) * (1.0 + sc) + sh


def _silu(x):
    hx = 0.5 * x
    return hx + hx * jnp.tanh(hx)


def _ada_kernel(c_ref, w_ref, b_ref, o_ref):
    c = _silu(c_ref[...])
    o_ref[0] = jnp.dot(c.astype(BF16), w_ref[0].astype(BF16), preferred_element_type=F32) + b_ref[0]


def ada_call(c_all, w_ada, b_ada):
    nrow = c_all.shape[0]
    depth, d, d6 = w_ada.shape
    tn = 1024
    return pl.pallas_call(
        _ada_kernel,
        grid=(depth, d6 // tn),
        in_specs=[
            pl.BlockSpec((nrow, d), lambda l, j: (0, 0)),
            pl.BlockSpec((1, d, tn), lambda l, j: (l, 0, j)),
            pl.BlockSpec((1, 1, tn), lambda l, j: (l, 0, j)),
        ],
        out_specs=pl.BlockSpec((1, nrow, tn), lambda l, j: (l, 0, j)),
        out_shape=jax.ShapeDtypeStruct((depth, nrow, d6), F32),
        compiler_params=_cparams(("arbitrary", "arbitrary")),
        name="ada",
    )(c_all, w_ada, b_ada.reshape(depth, 1, d6))


def _inproj_kernel(first_ref, last_ref,
                   x_ref, xp_ref, xn_ref, mod_ref, nw_ref, wf_ref, wqkv_ref, wz_ref, wba_ref, cw_ref, gp_ref,
                   f_ref, z_ref, q_ref, k_ref, v_ref, gcol_ref, grow_ref, gend_ref,
                   xe_ref):
    i = pl.program_id(0)
    tm = x_ref.shape[0]
    nc = tm // CHUNK
    halo = SUBLANES
    xcat = jnp.concatenate([xp_ref[...], x_ref[...], xn_ref[...]], axis=0)
    h = _mod_norm(xcat, nw_ref[...], mod_ref[0, 1:2, :], mod_ref[0, 0:1, :]).astype(BF16)
    pm = jnp.where(first_ref[i] == 1, 0.0, 1.0)
    nm = jnp.where(last_ref[i] == 1, 0.0, 1.0)
    outs = (q_ref, k_ref, v_ref)
    for cb in range(QKV_DIM // MXU_COLS):
        c0 = cb * MXU_COLS
        res = jnp.dot(h, wqkv_ref[:, c0:c0 + MXU_COLS], preferred_element_type=F32)
        xe_ref[0:halo, c0:c0 + MXU_COLS] = res[0:halo] * pm
        xe_ref[halo:halo + tm, c0:c0 + MXU_COLS] = res[halo:halo + tm]
        xe_ref[halo + tm:, c0:c0 + MXU_COLS] = res[halo + tm:] * nm
        for s in range(cb * (MXU_COLS // HD), (cb + 1) * (MXU_COLS // HD)):
            lo = s * HD
            xs = xe_ref[:, lo:lo + HD]
            acc = None
            for j in range(CONV_K):
                shift = (CONV_K // 2 - j) % (tm + 2 * halo)
                xj = xs if shift == 0 else pltpu.roll(xs, shift, axis=0)
                term = xj[halo:halo + tm] * cw_ref[j:j + 1, lo:lo + HD]
                acc = term if acc is None else acc + term
            y = _silu(acc)
            which, head = divmod(s, NH)
            if which < 2:
                y = y * lax.rsqrt(jnp.sum(y * y, axis=-1, keepdims=True) + 1e-6)
            if which == 0:
                y = y * (HD ** -0.5)
            outs[which][:, head * HD:(head + 1) * HD] = y
    f_ref[...] = jnp.dot(h, wf_ref[...], preferred_element_type=F32)[halo:halo + tm]
    z_ref[...] = jnp.dot(h, wz_ref[...], preferred_element_type=F32)[halo:halo + tm]
    ba = jnp.dot(h, wba_ref[...], preferred_element_type=F32)[halo:halo + tm]

    lane = lax.broadcasted_iota(jnp.int32, (tm, LANES), 1)
    beta = 1.0 / (1.0 + jnp.exp(-ba))
    xs = ba + gp_ref[0:1, :]
    softplus = jnp.maximum(xs, 0.0) + jnp.log(1.0 + jnp.exp(-jnp.abs(xs)))
    g = -jnp.exp(gp_ref[1:2, :]) * softplus
    g = jnp.where((lane >= 2 * NH) & (lane < 4 * NH), g, 0.0)
    g = pltpu.roll(g, LANES - 2 * NH, axis=1)

    rowc = lax.broadcasted_iota(jnp.int32, (tm, LANES), 0) % CHUNK
    p = g
    sh = 1
    while sh < CHUNK:
        p = p + jnp.where(rowc >= sh, pltpu.roll(p, sh, axis=0), 0.0)
        sh *= 2
    g3 = g.reshape(nc, CHUNK, LANES)
    tot = jnp.broadcast_to(jnp.sum(g3, axis=1, keepdims=True), (nc, CHUNK, LANES)).reshape(tm, LANES)
    is_bwd = (lane >= NH) & (lane < 2 * NH)
    gamma = jnp.where(is_bwd, tot - p + g, p)
    eg = jnp.exp(gamma)
    egr = jnp.exp(tot - gamma)
    m12 = lane < 2 * NH
    gcol = (jnp.where(m12, gamma, 0.0)
            + pltpu.roll(jnp.where(m12, beta, 0.0), GL_BETA, axis=1)
            + pltpu.roll(jnp.where(m12, eg, 0.0), GL_EG, axis=1)
            + pltpu.roll(jnp.where(m12, egr, 0.0), GL_EGR, axis=1))
    gcol_ref[...] = gcol
    gam_t = jnp.where(m12, gamma, 0.0).T[0:2 * SUBLANES, :]
    g_t = g.T[0:2 * SUBLANES, :]
    for c in range(nc):
        grow_ref[c] = gam_t[:, c * CHUNK:(c + 1) * CHUNK]
        tc = jnp.sum(g_t[:, c * CHUNK:(c + 1) * CHUNK], axis=-1, keepdims=True)
        gend_ref[c] = jnp.exp(jnp.broadcast_to(tc, (2 * SUBLANES, LANES)))


def inproj_call(first, last, x, mod, nw, wf, wqkv, wz, wba, cw, gp, *, seg, tm):
    t, d = x.shape
    nblk8 = t // SUBLANES
    r8 = tm // SUBLANES
    nc = tm // CHUNK
    row = lambda i, f, l: (i, 0)
    const = lambda i, f, l: (0, 0)
    chunked = lambda i, f, l: (i, 0, 0)
    single = pl.Buffered(1)
    grid_spec = pltpu.PrefetchScalarGridSpec(
        num_scalar_prefetch=2,
        grid=(t // tm,),
        in_specs=[
            pl.BlockSpec((tm, d), row),
            pl.BlockSpec((SUBLANES, d), lambda i, f, l: (jnp.maximum(i * r8 - 1, 0), 0)),
            pl.BlockSpec((SUBLANES, d), lambda i, f, l: (jnp.minimum((i + 1) * r8, nblk8 - 1), 0)),
            pl.BlockSpec((1, SUBLANES, d), lambda i, f, l: (i * tm // seg, 0, 0)),
            pl.BlockSpec((1, d), const),
            pl.BlockSpec(wf.shape, const, pipeline_mode=single),
            pl.BlockSpec(wqkv.shape, const, pipeline_mode=single),
            pl.BlockSpec(wz.shape, const, pipeline_mode=single),
            pl.BlockSpec(wba.shape, const, pipeline_mode=single),
            pl.BlockSpec((SUBLANES, QKV_DIM), const),
            pl.BlockSpec((SUBLANES, LANES), const),
        ],
        out_specs=[
            pl.BlockSpec((tm, F_DIM), row),
            pl.BlockSpec((tm, G_DIM), row),
            pl.BlockSpec((tm, G_DIM), row),
            pl.BlockSpec((tm, G_DIM), row),
            pl.BlockSpec((tm, G_DIM), row),
            pl.BlockSpec((tm, LANES), row),
            pl.BlockSpec((nc, 2 * SUBLANES, CHUNK), chunked),
            pl.BlockSpec((nc, 2 * SUBLANES, LANES), chunked),
        ],
        scratch_shapes=[pltpu.VMEM((tm + 2 * SUBLANES, QKV_DIM), F32)],
    )
    return pl.pallas_call(
        _inproj_kernel,
        grid_spec=grid_spec,
        out_shape=[
            jax.ShapeDtypeStruct((t, F_DIM), F32),
            jax.ShapeDtypeStruct((t, G_DIM), F32),
            jax.ShapeDtypeStruct((t, G_DIM), F32),
            jax.ShapeDtypeStruct((t, G_DIM), F32),
            jax.ShapeDtypeStruct((t, G_DIM), F32),
            jax.ShapeDtypeStruct((t, LANES), F32),
            jax.ShapeDtypeStruct((t // CHUNK, 2 * SUBLANES, CHUNK), F32),
            jax.ShapeDtypeStruct((t // CHUNK, 2 * SUBLANES, LANES), F32),
        ],
        compiler_params=_cparams(("arbitrary",)),
        name="inproj_conv",
    )(first, last, x, x, x, mod, nw, wf, wqkv, wz, wba, cw, gp)


def _gdn_kernel(rf_ref, rb_ref,
                qf_ref, kf_ref, vf_ref, gcf_ref, grf_ref, gef_ref,
                qb_ref, kb_ref, vb_ref, gcb_ref, grb_ref, geb_ref,
                of_ref, ob_ref, s_ref, wq_ref, ab_ref, u_ref):
    i = pl.program_id(0)
    tb = qf_ref.shape[0]
    nc = tb // CHUNK

    @pl.when(rf_ref[i] == 1)
    def _():
        s_ref[0:NH] = jnp.zeros((NH, HD, HD), F32)

    @pl.when(rb_ref[i] == 1)
    def _():
        s_ref[NH:2 * NH] = jnp.zeros((NH, HD, HD), F32)

    row = lax.broadcasted_iota(jnp.int32, (CHUNK, CHUNK), 0)
    col = lax.broadcasted_iota(jnp.int32, (CHUNK, CHUNK), 1)
    eye = jnp.where(row == col, 1.0, 0.0).astype(F32)
    level_masks = []
    b = 1
    while b < CHUNK:
        level_masks.append((row // (2 * b) == col // (2 * b)) & (row // b != col // b))
        b *= 2
    dirs = (
        (qf_ref, kf_ref, vf_ref, gcf_ref, grf_ref, gef_ref, of_ref, row >= col, row > col),
        (qb_ref, kb_ref, vb_ref, gcb_ref, grb_ref, geb_ref, ob_ref, row <= col, row < col),
    )

    hds = [(d, h) for d in range(2) for h in range(NH)]
    probs = [(u, d, h) for u in range(PREP_CHUNKS) for d, h in hds]
    npr = len(probs)

    def prep_step(it, carry):
        cs = [it * PREP_CHUNKS + u for u in range(PREP_CHUNKS)]
        qs, ks, vs, cols, decs = [], [], [], [], []
        for u, d, h in probs:
            r0 = pl.multiple_of(cs[u] * CHUNK, CHUNK)
            q_ref, k_ref, v_ref, gc_ref, gr_ref = dirs[d][0:5]
            hd = d * NH + h
            lo = h * HD
            gc = gc_ref[pl.ds(r0, CHUNK), :]
            qs.append(q_ref[pl.ds(r0, CHUNK), lo:lo + HD])
            ks.append(k_ref[pl.ds(r0, CHUNK), lo:lo + HD])
            vs.append(v_ref[pl.ds(r0, CHUNK), lo:lo + HD])
            gam_c = gc[:, GL_GAMMA + hd:GL_GAMMA + hd + 1]
            cols.append((gc[:, GL_BETA + hd:GL_BETA + hd + 1], gc[:, GL_EG + hd:GL_EG + hd + 1],
                         gc[:, GL_EGR + hd:GL_EGR + hd + 1]))
            gam_r = gr_ref[cs[u]][hd:hd + 1, :]
            decs.append(jnp.exp(jnp.where(dirs[d][7], gam_c - gam_r, NEG_BIG)))
        kqs = [lax.dot_general(jnp.concatenate([ks[n], qs[n]], axis=0), ks[n],
                               (((1,), (1,)), ((), ())), preferred_element_type=F32)
               for n in range(npr)]
        a_s = [jnp.where(dirs[d][8], kqs[n][0:CHUNK] * cols[n][0] * decs[n], 0.0)
               for n, (u, d, h) in enumerate(probs)]
        for n, (u, d, h) in enumerate(probs):
            ab_ref[d, cs[u], h, 0:CHUNK, :] = kqs[n][CHUNK:] * decs[n]
            ab_ref[d, cs[u], h, CHUNK:, :] = (ks[n] * cols[n][2]).T
        ts = [eye - jnp.where(level_masks[0], a_s[n], 0.0) for n in range(npr)]
        for lm in level_masks[1:]:
            lts = [jnp.dot(jnp.where(lm, a_s[n], 0.0), ts[n], preferred_element_type=F32) for n in range(npr)]
            ts = [ts[n] - jnp.dot(ts[n], lts[n], preferred_element_type=F32) for n in range(npr)]
        uws =[jnp.dot(ts[n], jnp.concatenate([vs[n] * cols[n][0], ks[n] * (cols[n][0] * cols[n][1])], axis=1),
                       preferred_element_type=F32) for n in range(npr)]
        for n, (u, d, h) in enumerate(probs):
            u_ref[d, cs[u], h] = uws[n][:, 0:HD]
            wq_ref[d, cs[u], h, 0:CHUNK, :] = uws[n][:, HD:]
            wq_ref[d, cs[u], h, CHUNK:, :] = qs[n] * cols[n][1]
        return carry

    lax.fori_loop(0, nc // PREP_CHUNKS, prep_step, 0)

    def scan_step(c, carry):
        ccs = (c, nc - 1 - c)
        sts = [s_ref[d * NH + h] for d, h in hds]
        wqs = [jnp.dot(wq_ref[d, ccs[d], h], sts[n], preferred_element_type=F32)
               for n, (d, h) in enumerate(hds)]
        vns = [u_ref[d, ccs[d], h] - wqs[n][0:CHUNK] for n, (d, h) in enumerate(hds)]
        avs = [jnp.dot(ab_ref[d, ccs[d], h], vns[n], preferred_element_type=F32)
               for n, (d, h) in enumerate(hds)]
        for n, (d, h) in enumerate(hds):
            r0 = pl.multiple_of(ccs[d] * CHUNK, CHUNK)
            dirs[d][6][pl.ds(r0, CHUNK), h * HD:(h + 1) * HD] = wqs[n][CHUNK:] + avs[n][0:CHUNK]
            ge = dirs[d][5][ccs[d]]
            hd = d * NH + h
            s_ref[hd] = sts[n] * ge[hd:hd + 1, :] + avs[n][CHUNK:]
        return carry

    lax.fori_loop(0, nc, scan_step, 0)


def gdn_call(reset_f, reset_b, q, k, v, gcol, grow, gend, *, tb):
    t = q.shape[0]
    nb = t // tb
    nc = tb // CHUNK
    fwd2 = lambda i, a, b: (i, 0)
    bwd2 = lambda i, a, b: (nb - 1 - i, 0)
    fwd3 = lambda i, a, b: (i, 0, 0)
    bwd3 = lambda i, a, b: (nb - 1 - i, 0, 0)

    def specs(m2, m3):
        return [
            pl.BlockSpec((tb, G_DIM), m2), pl.BlockSpec((tb, G_DIM), m2), pl.BlockSpec((tb, G_DIM), m2),
            pl.BlockSpec((tb, LANES), m2),
            pl.BlockSpec((nc, 2 * SUBLANES, CHUNK), m3),
            pl.BlockSpec((nc, 2 * SUBLANES, LANES), m3),
        ]

    grid_spec = pltpu.PrefetchScalarGridSpec(
        num_scalar_prefetch=2,
        grid=(nb,),
        in_specs=specs(fwd2, fwd3) + specs(bwd2, bwd3),
        out_specs=[pl.BlockSpec((tb, G_DIM), fwd2), pl.BlockSpec((tb, G_DIM), bwd2)],
        scratch_shapes=[pltpu.VMEM((2 * NH, HD, HD), F32),
                        pltpu.VMEM((2, nc, NH, 2 * CHUNK, HD), F32),
                        pltpu.VMEM((2, nc, NH, CHUNK + HD, CHUNK), F32),
                        pltpu.VMEM((2, nc, NH, CHUNK, HD), F32)],
    )
    return pl.pallas_call(
        _gdn_kernel,
        grid_spec=grid_spec,
        out_shape=[jax.ShapeDtypeStruct((t, G_DIM), F32), jax.ShapeDtypeStruct((t, G_DIM), F32)],
        compiler_params=_cparams(("arbitrary",)),
        name="gdn",
    )(reset_f, reset_b, q, k, v, gcol, grow, gend, q, k, v, gcol, grow, gend)


def _fft1_kernel(x_ref, m1_ref, tc_ref, ts_ref, br_ref, bi_ref):
    s1 = x_ref.shape[1]
    a = jnp.dot(m1_ref[...], x_ref[0], preferred_element_type=F32)
    ar, ai = a[0:s1], a[s1:]
    tc, ts = tc_ref[...], ts_ref[...]
    br_ref[0] = ar * tc + ai * ts
    bi_ref[0] = ai * tc - ar * ts


def fft1_call(x3, m1, twc, tws, *, tcol):
    nseq, s1, cols = x3.shape
    blk = pl.BlockSpec((1, s1, tcol), lambda j, b: (b, 0, j))
    tw = pl.BlockSpec((s1, tcol), lambda j, b: (0, j))
    return pl.pallas_call(
        _fft1_kernel,
        grid=(cols // tcol, nseq),
        in_specs=[blk, pl.BlockSpec(m1.shape, lambda j, b: (0, 0)), tw, tw],
        out_specs=[blk, blk],
        out_shape=[jax.ShapeDtypeStruct(x3.shape, F32)] * 2,
        compiler_params=_cparams(("arbitrary", "arbitrary")),
        name="fft_stage1",
    )(x3, m1, twc, tws)


def _fft2_kernel(br_ref, bi_ref, mc_ref, m2_ref, o_ref):
    _, tk, s2, c = br_ref.shape
    nsplit = o_ref.shape[0]
    s2o = s2 // nsplit
    b = jnp.concatenate([br_ref[0].reshape(tk * s2, c), bi_ref[0].reshape(tk * s2, c)], axis=1)
    z = jnp.dot(b, mc_ref[...], preferred_element_type=F32)
    m2 = m2_ref[...]
    for kk in range(tk):
        zk = z[kk * s2:(kk + 1) * s2]
        x = jnp.dot(m2, jnp.concatenate([zk[:, 0:c], zk[:, c:]], axis=0),
                    preferred_element_type=F32)
        for sp in range(nsplit):
            o_ref[sp, kk] = x[sp * s2o:(sp + 1) * s2o]


def fft2_call(br4, bi4, mc, m2, *, tk, nsplit):
    nseq, s1, s2, c = br4.shape
    s2o = s2 // nsplit
    blk = pl.BlockSpec((1, tk, s2, c), lambda b, j: (b, j, 0, 0))
    return pl.pallas_call(
        _fft2_kernel,
        grid=(nseq, s1 // tk),
        in_specs=[blk, blk, pl.BlockSpec(mc.shape, lambda b, j: (0, 0)),
                  pl.BlockSpec(m2.shape, lambda b, j: (0, 0))],
        out_specs=pl.BlockSpec((nsplit, tk, s2o, c), lambda b, j: (b, j, 0, 0)),
        out_shape=jax.ShapeDtypeStruct((nseq * nsplit, s1, s2o, c), F32),
        compiler_params=_cparams(("arbitrary", "arbitrary")),
        name="fft_stage2",
    )(br4, bi4, mc, m2)


def _dft_tables(s):
    s1 = DFT1
    s2 = s // s1
    k = np.arange(s1)
    ang1 = 2.0 * np.pi * ((k[:, None] * k[None, :]) % s1) / s1
    sc = 1.0 / math.sqrt(s)
    m1 = np.concatenate([np.cos(ang1), -np.sin(ang1)], axis=0) * sc
    n2 = np.arange(s2)
    angt = 2.0 * np.pi * ((k[:, None] * n2[None, :]) % s) / s
    twc = np.repeat(np.cos(angt), F_DIM, axis=1)
    tws = np.repeat(np.sin(angt), F_DIM, axis=1)
    ang2 = 2.0 * np.pi * ((n2[:, None] * n2[None, :]) % s2) / s2
    m2 = np.concatenate([np.cos(ang2), np.sin(ang2)], axis=1)
    return (jnp.asarray(m1, F32), jnp.asarray(np.cos(angt), F32), jnp.asarray(np.sin(angt), F32),
            jnp.asarray(m2, F32))


def _channel_dft_matrix():
    g = np.arange(GROUP_DIM)
    ang = 2.0 * np.pi * ((g[:, None] * g[None, :]) % GROUP_DIM) / GROUP_DIM
    cg = np.kron(np.eye(N_GROUPS), np.cos(ang)) / math.sqrt(GROUP_DIM)
    sg = np.kron(np.eye(N_GROUPS), np.sin(ang)) / math.sqrt(GROUP_DIM)
    return jnp.asarray(np.block([[cg, -sg], [sg, cg]]), F32)


def fourier_mix_call(f, *, nseq, s, tcol, tk, nsplit):
    s1 = DFT1
    s2 = s // s1
    m1, tcs, tss, m2 = _dft_tables(s)
    twc = jnp.broadcast_to(tcs[:, :, None], (s1, s2, F_DIM)).reshape(s1, s2 * F_DIM)
    tws = jnp.broadcast_to(tss[:, :, None], (s1, s2, F_DIM)).reshape(s1, s2 * F_DIM)
    x3 = f.reshape(nseq, s1, s2 * F_DIM)
    br, bi = fft1_call(x3, m1, twc, tws, tcol=tcol)
    out = fft2_call(br.reshape(nseq, s1, s2, F_DIM), bi.reshape(nseq, s1, s2, F_DIM),
                    _channel_dft_matrix(), m2, tk=tk, nsplit=nsplit)
    return out.reshape(nseq * nsplit, s1, (s2 // nsplit) * F_DIM)


def _outproj_kernel(x_ref, mod_ref, of_ref, ob_ref, z_ref, fm_ref, gn_ref, w_ref, o_ref):
    tm = x_ref.shape[0]
    o = of_ref[...] + ob_ref[...]
    z = z_ref[...]
    gn = gn_ref[...]
    parts = []
    fm = fm_ref[...]
    parts.append(jnp.concatenate(
        [fm[:, j * F_DIM:(j + 1) * F_DIM] for j in range(tm // DFT1)], axis=0).astype(BF16))
    for h in range(NH):
        oh = o[:, h * HD:(h + 1) * HD]
        ms = jnp.mean(oh * oh, axis=-1, keepdims=True)
        y = (oh * lax.rsqrt(ms + NORM_EPS)) * gn
        parts.append((y * _silu(z[:, h * HD:(h + 1) * HD])).astype(BF16))
    mixed = jnp.concatenate(parts, axis=1)
    proj = jnp.dot(mixed, w_ref[...], preferred_element_type=F32)
    o_ref[...] = x_ref[...] + mod_ref[0, 2:3, :] * proj


def outproj_call(x, mod, o_f, o_b, z, fm2, gn, w, *, seg, tm):
    t, d = x.shape
    row = lambda i: (i, 0)
    const = lambda i: (0, 0)
    per_seg = seg // tm
    return pl.pallas_call(
        _outproj_kernel,
        grid=(t // tm,),
        in_specs=[
            pl.BlockSpec((tm, d), row),
            pl.BlockSpec((1, SUBLANES, d), lambda i: (i // per_seg, 0, 0)),
            pl.BlockSpec((tm, G_DIM), row),
            pl.BlockSpec((tm, G_DIM), row),
            pl.BlockSpec((tm, G_DIM), row),
            pl.BlockSpec((DFT1, (tm // DFT1) * F_DIM), lambda i: (i // per_seg, i % per_seg)),
            pl.BlockSpec((1, HD), const),
            pl.BlockSpec(w.shape, const),
        ],
        out_specs=pl.BlockSpec((tm, d), row),
        out_shape=jax.ShapeDtypeStruct((t, d), F32),
        compiler_params=_cparams(("arbitrary",)),
        name="outproj",
    )(x, mod, o_f, o_b, z, fm2, gn, w)


def _ffn_kernel(x_ref, mod_ref, nw_ref, wg_ref, wu_ref, wd_ref, o_ref, *, nsplit):
    x = x_ref[...]
    h = _mod_norm(x, nw_ref[...], mod_ref[0, 4:5, :], mod_ref[0, 3:4, :]).astype(BF16)
    dff = wg_ref.shape[1]
    cw = dff // nsplit
    acc = None
    for c in range(nsplit):
        g = jnp.dot(h, wg_ref[:, c * cw:(c + 1) * cw], preferred_element_type=F32)
        u = jnp.dot(h, wu_ref[:, c * cw:(c + 1) * cw], preferred_element_type=F32)
        hid = (_silu(g) * u).astype(BF16)
        part = jnp.dot(hid, wd_ref[c * cw:(c + 1) * cw, :], preferred_element_type=F32)
        acc = part if acc is None else acc + part
    o_ref[...] = x + mod_ref[0, 5:6, :] * acc


def ffn_call(x, mod, nw, wg, wu, wd, *, seg, tm):
    t, d = x.shape
    row = lambda i: (i, 0)
    const = lambda i: (0, 0)
    single = pl.Buffered(1)
    return pl.pallas_call(
        functools.partial(_ffn_kernel, nsplit=2),
        grid=(t // tm,),
        in_specs=[
            pl.BlockSpec((tm, d), row),
            pl.BlockSpec((1, SUBLANES, d), lambda i: (i * tm // seg, 0, 0)),
            pl.BlockSpec((1, d), const),
            pl.BlockSpec(wg.shape, const, pipeline_mode=single),
            pl.BlockSpec(wu.shape, const, pipeline_mode=single),
            pl.BlockSpec(wd.shape, const, pipeline_mode=single),
        ],
        out_specs=pl.BlockSpec((tm, d), row),
        out_shape=jax.ShapeDtypeStruct((t, d), F32),
        compiler_params=_cparams(("arbitrary",)),
        name="ffn",
    )(x, mod, nw, wg, wu, wd)


def _router_kernel(x_ref, mod_ref, nw_ref, wr_ref, tri_ref, oi_ref, op_ref, cnt_ref, carry_ref):
    i = pl.program_id(0)
    tm = x_ref.shape[0]

    @pl.when(i == 0)
    def _():
        carry_ref[...] = jnp.zeros_like(carry_ref)

    h = _mod_norm(x_ref[...], nw_ref[...], mod_ref[0, 4:5, :], mod_ref[0, 3:4, :])
    logits = jnp.dot(h.astype(BF16), wr_ref[...], preferred_element_type=F32)
    lane = lax.broadcasted_iota(jnp.int32, (tm, LANES), 1)
    logits = jnp.where(lane < N_EXPERTS, logits, NEG_BIG)
    l1 = jnp.max(logits, axis=-1, keepdims=True)
    i1 = jnp.min(jnp.where(logits == l1, lane, LANES), axis=-1, keepdims=True)
    rest = jnp.where(lane == i1, NEG_BIG, logits)
    l2 = jnp.max(rest, axis=-1, keepdims=True)
    i2 = jnp.min(jnp.where(rest == l2, lane, LANES), axis=-1, keepdims=True)
    e21 = jnp.exp(l2 - l1)
    p1 = 1.0 / (1.0 + e21)
    p2 = e21 * p1
    oh1 = lane == i1
    oh2 = lane == i2
    oh = jnp.where(oh1 | oh2, 1.0, 0.0).astype(BF16)
    before = jnp.dot(tri_ref[...], oh, preferred_element_type=F32) + carry_ref[0:1, :]
    r1 = jnp.sum(jnp.where(oh1, before, 0.0), axis=-1, keepdims=True).astype(jnp.int32)
    r2 = jnp.sum(jnp.where(oh2, before, 0.0), axis=-1, keepdims=True).astype(jnp.int32)
    oi_ref[...] = jnp.where(lane == 0, i1, jnp.where(lane == 1, i2, jnp.where(lane == 2, r1, r2)))
    op_ref[...] = jnp.where(lane == 0, p1, p2)
    new_carry = carry_ref[0:1, :] + jnp.sum(oh.astype(F32), axis=0, keepdims=True)
    carry_ref[...] = jnp.broadcast_to(new_carry, carry_ref.shape)
    cnt_ref[...] = jnp.broadcast_to(new_carry, cnt_ref.shape)


def router_call(x, mod, nw, wr, tri, *, seg, tm):
    t, d = x.shape
    row = lambda i: (i, 0)
    const = lambda i: (0, 0)
    return pl.pallas_call(
        _router_kernel,
        grid=(t // tm,),
        in_specs=[
            pl.BlockSpec((tm, d), row),
            pl.BlockSpec((1, SUBLANES, d), lambda i: (i * tm // seg, 0, 0)),
            pl.BlockSpec((1, d), const),
            pl.BlockSpec(wr.shape, const),
            pl.BlockSpec(tri.shape, const),
        ],
        out_specs=[pl.BlockSpec((tm, LANES), row), pl.BlockSpec((tm, LANES), row),
                   pl.BlockSpec((SUBLANES, LANES), const)],
        out_shape=[jax.ShapeDtypeStruct((t, LANES), jnp.int32), jax.ShapeDtypeStruct((t, LANES), F32),
                   jax.ShapeDtypeStruct((SUBLANES, LANES), F32)],
        scratch_shapes=[pltpu.VMEM((SUBLANES, LANES), F32)],
        compiler_params=_cparams(("arbitrary",)),
        name="router",
    )(x, mod, nw, wr, tri)


def _dispatch_kernel(pos_ref, x_ref, mod_ref, nw_ref, xs_in_ref, xs_ref, hbuf, sem):
    del xs_in_ref
    i = pl.program_id(0)
    last = pl.num_programs(0) - 1
    tm = x_ref.shape[0]
    buf = i % 2
    hbuf[buf] = _mod_norm(x_ref[...], nw_ref[...], mod_ref[0, 4:5, :], mod_ref[0, 3:4, :])

    def start(r, c):
        for slot in range(2):
            pltpu.make_async_copy(hbuf.at[buf, pl.ds(r, 1), :],
                                  xs_ref.at[pl.ds(pos_ref[0, 0, slot * tm + r], 1), :],
                                  sem.at[buf]).start(priority=slot)
        return c

    lax.fori_loop(0, tm, start, 0, unroll=8)

    def wait_rows(b):
        for _ in range(2):
            pltpu.make_async_copy(hbuf.at[b], xs_ref.at[pl.ds(0, tm), :], sem.at[b]).wait()

    @pl.when(i > 0)
    def _():
        wait_rows(1 - buf)

    @pl.when(i == last)
    def _():
        wait_rows(buf)


def dispatch_call(pos3, x, mod, nw, xs_zero, *, seg, tm):
    t, d = x.shape
    return pl.pallas_call(
        _dispatch_kernel,
        grid=(t // tm,),
        in_specs=[
            pl.BlockSpec((1, 1, 2 * tm), lambda i: (i, 0, 0), memory_space=pltpu.SMEM),
            pl.BlockSpec((tm, d), lambda i: (i, 0)),
            pl.BlockSpec((1, SUBLANES, d), lambda i: (i * tm // seg, 0, 0)),
            pl.BlockSpec((1, d), lambda i: (0, 0)),
            pl.BlockSpec(memory_space=pl.ANY),
        ],
        out_specs=pl.BlockSpec(memory_space=pl.ANY),
        out_shape=jax.ShapeDtypeStruct(xs_zero.shape, F32),
        scratch_shapes=[pltpu.VMEM((2, tm, d), F32), pltpu.SemaphoreType.DMA((2,))],
        input_output_aliases={4: 0},
        compiler_params=_cparams(("arbitrary",)),
        name="moe_dispatch",
    )(pos3, x, mod, nw, xs_zero)


def _expert_kernel(te_ref, nu_ref, xs_ref, wg_ref, wu_ref, wd_ref, y_ref, acc_ref):
    j = pl.program_id(0)
    half = pl.program_id(1)

    @pl.when(j < nu_ref[0])
    def _():
        h = xs_ref[...].astype(BF16)
        g = jnp.dot(h, wg_ref[0], preferred_element_type=F32)
        u = jnp.dot(h, wu_ref[0], preferred_element_type=F32)
        hid = (_silu(g) * u).astype(BF16)
        part = jnp.dot(hid, wd_ref[0], preferred_element_type=F32)

        @pl.when(half == 0)
        def _():
            acc_ref[...] = part

        @pl.when(half == 1)
        def _():
            y_ref[...] = acc_ref[...] + part

    @pl.when((j >= nu_ref[0]) & (half == 1))
    def _():
        y_ref[...] = jnp.zeros_like(y_ref)


def expert_call(tile_expert, n_used, xs, wg, wu, wd, *, tme):
    nr, d = xs.shape
    ntiles = nr // tme
    fh = wg.shape[2] // 2

    def jj(j, nu):
        return jnp.minimum(j, nu[0] - 1)

    def hh(j, hf, nu):
        return jnp.where(j < nu[0], hf, 1)

    grid_spec = pltpu.PrefetchScalarGridSpec(
        num_scalar_prefetch=2,
        grid=(ntiles, 2),
        in_specs=[
            pl.BlockSpec((tme, d), lambda j, hf, te, nu: (jj(j, nu), 0)),
            pl.BlockSpec((1, d, fh), lambda j, hf, te, nu: (te[jj(j, nu)], 0, hh(j, hf, nu))),
            pl.BlockSpec((1, d, fh), lambda j, hf, te, nu: (te[jj(j, nu)], 0, hh(j, hf, nu))),
            pl.BlockSpec((1, fh, d), lambda j, hf, te, nu: (te[jj(j, nu)], hh(j, hf, nu), 0)),
        ],
        out_specs=pl.BlockSpec((tme, d), lambda j, hf, te, nu: (j, 0)),
        scratch_shapes=[pltpu.VMEM((tme, d), F32)],
    )
    return pl.pallas_call(
        _expert_kernel,
        grid_spec=grid_spec,
        out_shape=jax.ShapeDtypeStruct((nr, d), F32),
        compiler_params=_cparams(("arbitrary", "arbitrary")),
        name="moe_experts",
    )(tile_expert, n_used, xs, wg, wu, wd)


def _combine_kernel(pos_ref, posn_ref, x_ref, mod_ref, p_ref, y_ref, o_ref, ybuf, sem):
    i = pl.program_id(0)
    last = pl.num_programs(0) - 1
    tm = x_ref.shape[0]
    buf = i % 2

    def gather(idx_ref, b):
        def start(r, c):
            for slot in range(2):
                pltpu.make_async_copy(y_ref.at[pl.ds(idx_ref[0, 0, slot * tm + r], 1), :],
                                      ybuf.at[b, slot, pl.ds(r, 1), :], sem.at[b]).start(priority=slot)
            return c

        lax.fori_loop(0, tm, start, 0, unroll=8)

    @pl.when(i == 0)
    def _():
        gather(pos_ref, buf)

    @pl.when(i < last)
    def _():
        gather(posn_ref, 1 - buf)

    for slot in range(2):
        pltpu.make_async_copy(y_ref.at[pl.ds(0, tm), :], ybuf.at[buf, slot], sem.at[buf]).wait()
    p = p_ref[...]
    f = p[:, 0:1] * ybuf[buf, 0] + p[:, 1:2] * ybuf[buf, 1]
    o_ref[...] = x_ref[...] + mod_ref[0, 5:6, :] * f


def combine_call(pos3, x, mod, p, y, *, seg, tm):
    t, d = x.shape
    nt = t // tm
    return pl.pallas_call(
        _combine_kernel,
        grid=(nt,),
        in_specs=[
            pl.BlockSpec((1, 1, 2 * tm), lambda i: (i, 0, 0), memory_space=pltpu.SMEM),
            pl.BlockSpec((1, 1, 2 * tm), lambda i: (jnp.minimum(i + 1, nt - 1), 0, 0),
                         memory_space=pltpu.SMEM),
            pl.BlockSpec((tm, d), lambda i: (i, 0)),
            pl.BlockSpec((1, SUBLANES, d), lambda i: (i * tm // seg, 0, 0)),
            pl.BlockSpec((tm, LANES), lambda i: (i, 0)),
            pl.BlockSpec(memory_space=pl.ANY),
        ],
        out_specs=pl.BlockSpec((tm, d), lambda i: (i, 0)),
        out_shape=jax.ShapeDtypeStruct((t, d), F32),
        scratch_shapes=[pltpu.VMEM((2, 2, tm, d), F32), pltpu.SemaphoreType.DMA((2,))],
        compiler_params=_cparams(("arbitrary",)),
        name="moe_combine",
    )(pos3, pos3, x, mod, p, y)


def moe_block(x, mod, nw, wr, wg, wu, wd, *, seg, tm, tmd, tme):
    t, d = x.shape
    wr_pad = jnp.zeros((d, LANES), BF16).at[:, :N_EXPERTS].set(wr.astype(BF16))
    tri = jnp.asarray(np.tril(np.ones((tm, tm), np.float32), k=-1), BF16)
    oi, op, cnt = router_call(x, mod, nw, wr_pad, tri, seg=seg, tm=tm)
    counts = cnt[0, :N_EXPERTS].astype(jnp.int32)
    padded = ((counts + tme - 1) // tme) * tme
    ends = jnp.cumsum(padded)
    starts = ends - padded
    nr = ((2 * t + N_EXPERTS * (tme - 1)) // tme) * tme
    ntiles = nr // tme
    n_used = (ends[-1] // tme).astype(jnp.int32).reshape(1)
    tile_start = jnp.arange(ntiles, dtype=jnp.int32) * tme
    tile_expert = jnp.minimum(jnp.sum(tile_start[:, None] >= ends[None, :], axis=1),
                              N_EXPERTS - 1).astype(jnp.int32)
    pos1 = starts[oi[:, 0]] + oi[:, 2]
    pos2 = starts[oi[:, 1]] + oi[:, 3]
    pos3 = jnp.concatenate([pos1.reshape(t // tmd, 1, tmd), pos2.reshape(t // tmd, 1, tmd)], axis=2)
    xs = dispatch_call(pos3, x, mod, nw, jnp.zeros((nr, d), F32), seg=seg, tm=tmd)
    y = expert_call(tile_expert, n_used, xs, wg, wu, wd, tme=tme)
    return combine_call(pos3, x, mod, op, y, seg=seg, tm=tmd)


def _final_kernel(x_ref, w_ref, o_ref):
    x = x_ref[...]
    ms = jnp.mean(x * x, axis=-1, keepdims=True)
    o_ref[...] = (x * lax.rsqrt(ms + NORM_EPS)) * w_ref[...]


def final_call(x, w, *, row0, nrows, tm):
    d = x.shape[1]
    off = row0 // tm
    return pl.pallas_call(
        _final_kernel,
        grid=(nrows // tm,),
        in_specs=[pl.BlockSpec((tm, d), lambda i: (i + off, 0)), pl.BlockSpec((1, d), lambda i: (0, 0))],
        out_specs=pl.BlockSpec((tm, d), lambda i: (i, 0)),
        out_shape=jax.ShapeDtypeStruct((nrows, d), F32),
        compiler_params=_cparams(("arbitrary",)),
        name="final_norm",
    )(x, w)


def _segment_flags(prompt_segs, sample_segs, seg, tile):
    per_seg = seg // tile
    nseg = prompt_segs + sample_segs
    first = np.zeros(nseg * per_seg, np.int32)
    last = np.zeros(nseg * per_seg, np.int32)
    first[0] = 1
    last[prompt_segs * per_seg - 1] = 1
    for s in range(prompt_segs, nseg):
        first[s * per_seg] = 1
        last[(s + 1) * per_seg - 1] = 1
    return first, last


def encoder_pair(x_prompt, x_sample, c_prompt, c_sample, w_ada, b_ada, norm_mix, norm_ffn, w_in, conv_w,
                 a_log, dt_bias, gdn_norm, w_out, w_ffn_gate, w_ffn_up, w_ffn_down, w_router,
                 w_exp_gate, w_exp_up, w_exp_down, norm_final, *, tm=512, tb=256, tmd=256, tme=512):
    bp, sp, d = x_prompt.shape
    bs, seg, _ = x_sample.shape
    assert bp == 1 and sp % seg == 0 and seg % tm == 0 and tm % DFT1 == 0
    depth = w_ada.shape[0]
    prompt_segs = sp // seg
    nseg = prompt_segs + bs
    t = nseg * seg
    x = jnp.concatenate([x_prompt.reshape(sp, d), x_sample.reshape(bs * seg, d)], axis=0)

    nrow = -(-(1 + bs) // SUBLANES) * SUBLANES
    c_all = jnp.zeros((nrow, d), F32).at[0:1].set(c_prompt).at[1:1 + bs].set(c_sample)
    ada = ada_call(c_all, w_ada, b_ada).reshape(depth, nrow, 6, d)
    seg_row = np.concatenate([np.zeros(prompt_segs, np.int32), 1 + np.arange(bs, dtype=np.int32)])
    mod_all = jnp.pad(ada[:, seg_row], ((0, 0), (0, 0), (0, SUBLANES - 6), (0, 0)))

    first_c, last_c = _segment_flags(prompt_segs, bs, seg, tm)
    first_g, last_g = _segment_flags(prompt_segs, bs, seg, tb)
    reset_f = jnp.asarray(first_g)
    reset_b = jnp.asarray(last_g[::-1].copy())
    first_c, last_c = jnp.asarray(first_c), jnp.asarray(last_c)

    off_f, off_qkv, off_z = F_DIM, F_DIM + QKV_DIM, F_DIM + QKV_DIM + G_DIM
    off_b = off_z + 2 * NH
    for l in range(depth):
        mod = mod_all[l]
        wl = w_in[l]
        wf = wl[:, :off_f].astype(BF16)
        wqkv = wl[:, off_f:off_qkv].astype(BF16)
        wz = wl[:, off_qkv:off_z].astype(BF16)
        wba = jnp.pad(wl[:, off_z:], ((0, 0), (0, LANES - 4 * NH))).astype(BF16)
        cw = jnp.pad(conv_w[l], ((0, SUBLANES - CONV_K), (0, 0)))
        gp = jnp.zeros((SUBLANES, LANES), F32)
        gp = gp.at[0, 2 * NH:4 * NH].set(dt_bias[l].reshape(-1)).at[1, 2 * NH:4 * NH].set(a_log[l].reshape(-1))
        f, z, q, k, v, gcol, grow, gend = inproj_call(first_c, last_c, x, mod, norm_mix[l].reshape(1, d),
                                                      wf, wqkv, wz, wba, cw, gp, seg=seg, tm=tm)
        o_f, o_b = gdn_call(reset_f, reset_b, q, k, v, gcol, grow, gend, tb=tb)

        s2p = sp // DFT1
        fm_p = fourier_mix_call(f[:sp], nseq=1, s=sp, tcol=min(2048, s2p * F_DIM),
                                tk=min(16, DFT1), nsplit=prompt_segs)
        s2s = seg // DFT1
        fm_s = fourier_mix_call(f[sp:], nseq=bs, s=seg, tcol=min(2048, s2s * F_DIM),
                                tk=min(64, DFT1), nsplit=1)
        fm2 = jnp.concatenate([fm_p, fm_s], axis=0).reshape(nseg * DFT1, (seg // DFT1) * F_DIM)

        x = outproj_call(x, mod, o_f, o_b, z, fm2, gdn_norm[l].reshape(1, HD), w_out[l].astype(BF16),
                         seg=seg, tm=tm)
        nw = norm_ffn[l].reshape(1, d)
        if l % 2 == 0:
            i = l // 2
            x = ffn_call(x, mod, nw, w_ffn_gate[i].astype(BF16), w_ffn_up[i].astype(BF16),
                         w_ffn_down[i].astype(BF16), seg=seg, tm=tm)
        else:
            i = l // 2
            x = moe_block(x, mod, nw, w_router[i], w_exp_gate[i].astype(BF16), w_exp_up[i].astype(BF16),
                          w_exp_down[i].astype(BF16), seg=seg, tm=tm, tmd=tmd, tme=tme)

    wn = norm_final.reshape(1, d)
    y_p = final_call(x, wn, row0=0, nrows=sp, tm=tm).reshape(bp, sp, d)
    y_s = final_call(x, wn, row0=sp, nrows=bs * seg, tm=tm).reshape(bs, seg, d)
    return y_p, y_s


def kernel(x_prompt, x_sample, c_prompt, c_sample, w_ada, b_ada, norm_mix, norm_ffn, w_in, conv_w, a_log,
           dt_bias, gdn_norm, w_out, w_ffn_gate, w_ffn_up, w_ffn_down, w_router, w_exp_gate, w_exp_up,
           w_exp_down, norm_final):
    return encoder_pair(x_prompt, x_sample, c_prompt, c_sample, w_ada, b_ada, norm_mix, norm_ffn, w_in,
                        conv_w, a_log, dt_bias, gdn_norm, w_out, w_ffn_gate, w_ffn_up, w_ffn_down,
                        w_router, w_exp_gate, w_exp_up, w_exp_down, norm_final)
```

```python
import functools
import math

import numpy as np
import jax
import jax.numpy as jnp
from jax import lax
from jax.experimental import pallas as pl
from jax.experimental.pallas import tpu as pltpu

F32 = jnp.float32
BF16 = jnp.bfloat16

D_MODEL = 1024
DEPTH = 4
N_GROUPS = 4
GROUP_DIM = 64
F_DIM = N_GROUPS * GROUP_DIM
HD = 128
NH = 6
G_DIM = NH * HD
QKV_DIM = 3 * G_DIM
CONV_K = 5
CHUNK = 64
PREP_CHUNKS = 2
D_FF = 2816
N_EXPERTS = 8
D_EXPERT = 3584
NORM_EPS = 1e-6

LANES = 128
SUBLANES = 8
MXU_COLS = 256
VMEM_LIMIT = 56 * 1024 * 1024
DFT1 = 128

GL_GAMMA, GL_BETA, GL_EG, GL_EGR = 0, 16, 32, 48
NEG_BIG = -1e30


def _cparams(sem):
    return pltpu.CompilerParams(dimension_semantics=sem, vmem_limit_bytes=VMEM_LIMIT)


def _mod_norm(x, nw, sc, sh):
    ms = jnp.mean(x * x, axis=-1, keepdims=True)
    y = x * lax.rsqrt(ms + NORM_EPS)
    return (y * nw) * (1.0 + sc) + sh


def _silu(x):
    hx = 0.5 * x
    return hx + hx * jnp.tanh(hx)


def _ada_kernel(c_ref, w_ref, b_ref, o_ref):
    c = _silu(c_ref[...])
    o_ref[0] = jnp.dot(c.astype(BF16), w_ref[0].astype(BF16), preferred_element_type=F32) + b_ref[0]


def ada_call(c_all, w_ada, b_ada):
    nrow = c_all.shape[0]
    depth, d, d6 = w_ada.shape
    tn = 1024
    return pl.pallas_call(
        _ada_kernel,
        grid=(depth, d6 // tn),
        in_specs=[
            pl.BlockSpec((nrow, d), lambda l, j: (0, 0)),
            pl.BlockSpec((1, d, tn), lambda l, j: (l, 0, j)),
            pl.BlockSpec((1, 1, tn), lambda l, j: (l, 0, j)),
        ],
        out_specs=pl.BlockSpec((1, nrow, tn), lambda l, j: (l, 0, j)),
        out_shape=jax.ShapeDtypeStruct((depth, nrow, d6), F32),
        compiler_params=_cparams(("arbitrary", "arbitrary")),
        name="ada",
    )(c_all, w_ada, b_ada.reshape(depth, 1, d6))


def _inproj_kernel(first_ref, last_ref,
                   x_ref, xp_ref, xn_ref, mod_ref, nw_ref, wf_ref, wqkv_ref, wz_ref, wba_ref, cw_ref, gp_ref,
                   f_ref, z_ref, q_ref, k_ref, v_ref, gcol_ref, grow_ref, gend_ref,
                   xe_ref):
    i = pl.program_id(0)
    tm = x_ref.shape[0]
    nc = tm // CHUNK
    halo = SUBLANES
    xcat = jnp.concatenate([xp_ref[...], x_ref[...], xn_ref[...]], axis=0)
    h = _mod_norm(xcat, nw_ref[...], mod_ref[0, 1:2, :], mod_ref[0, 0:1, :]).astype(BF16)
    pm = jnp.where(first_ref[i] == 1, 0.0, 1.0)
    nm = jnp.where(last_ref[i] == 1, 0.0, 1.0)
    outs = (q_ref, k_ref, v_ref)
    for cb in range(QKV_DIM // MXU_COLS):
        c0 = cb * MXU_COLS
        res = jnp.dot(h, wqkv_ref[:, c0:c0 + MXU_COLS], preferred_element_type=F32)
        xe_ref[0:halo, c0:c0 + MXU_COLS] = res[0:halo] * pm
        xe_ref[halo:halo + tm, c0:c0 + MXU_COLS] = res[halo:halo + tm]
        xe_ref[halo + tm:, c0:c0 + MXU_COLS] = res[halo + tm:] * nm
        for s in range(cb * (MXU_COLS // HD), (cb + 1) * (MXU_COLS // HD)):
            lo = s * HD
            xs = xe_ref[:, lo:lo + HD]
            acc = None
            for j in range(CONV_K):
                shift = (CONV_K // 2 - j) % (tm + 2 * halo)
                xj = xs if shift == 0 else pltpu.roll(xs, shift, axis=0)
                term = xj[halo:halo + tm] * cw_ref[j:j + 1, lo:lo + HD]
                acc = term if acc is None else acc + term
            y = _silu(acc)
            which, head = divmod(s, NH)
            if which < 2:
                y = y * lax.rsqrt(jnp.sum(y * y, axis=-1, keepdims=True) + 1e-6)
            if which == 0:
                y = y * (HD ** -0.5)
            outs[which][:, head * HD:(head + 1) * HD] = y
    f_ref[...] = jnp.dot(h, wf_ref[...], preferred_element_type=F32)[halo:halo + tm]
    z_ref[...] = jnp.dot(h, wz_ref[...], preferred_element_type=F32)[halo:halo + tm].astype(z_ref.dtype)
    ba = jnp.dot(h, wba_ref[...], preferred_element_type=F32)[halo:halo + tm]

    lane = lax.broadcasted_iota(jnp.int32, (tm, LANES), 1)
    beta = 1.0 / (1.0 + jnp.exp(-ba))
    xs = ba + gp_ref[0:1, :]
    softplus = jnp.maximum(xs, 0.0) + jnp.log(1.0 + jnp.exp(-jnp.abs(xs)))
    g = -jnp.exp(gp_ref[1:2, :]) * softplus
    g = jnp.where((lane >= 2 * NH) & (lane < 4 * NH), g, 0.0)
    g = pltpu.roll(g, LANES - 2 * NH, axis=1)

    rowc = lax.broadcasted_iota(jnp.int32, (tm, LANES), 0) % CHUNK
    p = g
    sh = 1
    while sh < CHUNK:
        p = p + jnp.where(rowc >= sh, pltpu.roll(p, sh, axis=0), 0.0)
        sh *= 2
    g3 = g.reshape(nc, CHUNK, LANES)
    tot = jnp.broadcast_to(jnp.sum(g3, axis=1, keepdims=True), (nc, CHUNK, LANES)).reshape(tm, LANES)
    is_bwd = (lane >= NH) & (lane < 2 * NH)
    gamma = jnp.where(is_bwd, tot - p + g, p)
    eg = jnp.exp(gamma)
    egr = jnp.exp(tot - gamma)
    m12 = lane < 2 * NH
    gcol = (jnp.where(m12, gamma, 0.0)
            + pltpu.roll(jnp.where(m12, beta, 0.0), GL_BETA, axis=1)
            + pltpu.roll(jnp.where(m12, eg, 0.0), GL_EG, axis=1)
            + pltpu.roll(jnp.where(m12, egr, 0.0), GL_EGR, axis=1))
    gcol_ref[...] = gcol
    gam_t = jnp.where(m12, gamma, 0.0).T[0:2 * SUBLANES, :]
    g_t = g.T[0:2 * SUBLANES, :]
    for c in range(nc):
        grow_ref[c] = gam_t[:, c * CHUNK:(c + 1) * CHUNK]
        tc = jnp.sum(g_t[:, c * CHUNK:(c + 1) * CHUNK], axis=-1, keepdims=True)
        gend_ref[c] = jnp.exp(jnp.broadcast_to(tc, (2 * SUBLANES, LANES)))


def inproj_call(first, last, x, mod, nw, wf, wqkv, wz, wba, cw, gp, *, seg, tm):
    t, d = x.shape
    nblk8 = t // SUBLANES
    r8 = tm // SUBLANES
    nc = tm // CHUNK
    row = lambda i, f, l: (i, 0)
    const = lambda i, f, l: (0, 0)
    chunked = lambda i, f, l: (i, 0, 0)
    single = pl.Buffered(1)
    grid_spec = pltpu.PrefetchScalarGridSpec(
        num_scalar_prefetch=2,
        grid=(t // tm,),
        in_specs=[
            pl.BlockSpec((tm, d), row),
            pl.BlockSpec((SUBLANES, d), lambda i, f, l: (jnp.maximum(i * r8 - 1, 0), 0)),
            pl.BlockSpec((SUBLANES, d), lambda i, f, l: (jnp.minimum((i + 1) * r8, nblk8 - 1), 0)),
            pl.BlockSpec((1, SUBLANES, d), lambda i, f, l: (i * tm // seg, 0, 0)),
            pl.BlockSpec((1, d), const),
            pl.BlockSpec(wf.shape, const, pipeline_mode=single),
            pl.BlockSpec(wqkv.shape, const, pipeline_mode=single),
            pl.BlockSpec(wz.shape, const, pipeline_mode=single),
            pl.BlockSpec(wba.shape, const, pipeline_mode=single),
            pl.BlockSpec((SUBLANES, QKV_DIM), const),
            pl.BlockSpec((SUBLANES, LANES), const),
        ],
        out_specs=[
            pl.BlockSpec((tm, F_DIM), row),
            pl.BlockSpec((tm, G_DIM), row),
            pl.BlockSpec((tm, G_DIM), row),
            pl.BlockSpec((tm, G_DIM), row),
            pl.BlockSpec((tm, G_DIM), row),
            pl.BlockSpec((tm, LANES), row),
            pl.BlockSpec((nc, 2 * SUBLANES, CHUNK), chunked),
            pl.BlockSpec((nc, 2 * SUBLANES, LANES), chunked),
        ],
        scratch_shapes=[pltpu.VMEM((tm + 2 * SUBLANES, QKV_DIM), F32)],
    )
    return pl.pallas_call(
        _inproj_kernel,
        grid_spec=grid_spec,
        out_shape=[
            jax.ShapeDtypeStruct((t, F_DIM), F32),
            jax.ShapeDtypeStruct((t, G_DIM), BF16),
            jax.ShapeDtypeStruct((t, G_DIM), F32),
            jax.ShapeDtypeStruct((t, G_DIM), F32),
            jax.ShapeDtypeStruct((t, G_DIM), F32),
            jax.ShapeDtypeStruct((t, LANES), F32),
            jax.ShapeDtypeStruct((t // CHUNK, 2 * SUBLANES, CHUNK), F32),
            jax.ShapeDtypeStruct((t // CHUNK, 2 * SUBLANES, LANES), F32),
        ],
        compiler_params=_cparams(("arbitrary",)),
        name="inproj_conv",
    )(first, last, x, x, x, mod, nw, wf, wqkv, wz, wba, cw, gp)


def _gdn_kernel(rf_ref, rb_ref,
                qf_ref, kf_ref, vf_ref, gcf_ref, grf_ref, gef_ref,
                qb_ref, kb_ref, vb_ref, gcb_ref, grb_ref, geb_ref,
                of_ref, ob_ref, s_ref, wq_ref, ab_ref, u_ref):
    i = pl.program_id(0)
    tb = qf_ref.shape[0]
    nc = tb // CHUNK

    @pl.when(rf_ref[i] == 1)
    def _():
        s_ref[0:NH] = jnp.zeros((NH, HD, HD), F32)

    @pl.when(rb_ref[i] == 1)
    def _():
        s_ref[NH:2 * NH] = jnp.zeros((NH, HD, HD), F32)

    row = lax.broadcasted_iota(jnp.int32, (CHUNK, CHUNK), 0)
    col = lax.broadcasted_iota(jnp.int32, (CHUNK, CHUNK), 1)
    eye = jnp.where(row == col, 1.0, 0.0).astype(F32)
    level_masks = []
    b = 1
    while b < CHUNK:
        level_masks.append((row // (2 * b) == col // (2 * b)) & (row // b != col // b))
        b *= 2
    dirs = (
        (qf_ref, kf_ref, vf_ref, gcf_ref, grf_ref, gef_ref, of_ref, row >= col, row > col),
        (qb_ref, kb_ref, vb_ref, gcb_ref, grb_ref, geb_ref, ob_ref, row <= col, row < col),
    )

    hds = [(d, h) for d in range(2) for h in range(NH)]
    probs = [(u, d, h) for u in range(PREP_CHUNKS) for d, h in hds]
    npr = len(probs)

    def prep_step(it, carry):
        cs = [it * PREP_CHUNKS + u for u in range(PREP_CHUNKS)]
        qs, ks, vs, cols, decs = [], [], [], [], []
        for u, d, h in probs:
            r0 = pl.multiple_of(cs[u] * CHUNK, CHUNK)
            q_ref, k_ref, v_ref, gc_ref, gr_ref = dirs[d][0:5]
            hd = d * NH + h
            lo = h * HD
            gc = gc_ref[pl.ds(r0, CHUNK), :]
            qs.append(q_ref[pl.ds(r0, CHUNK), lo:lo + HD])
            ks.append(k_ref[pl.ds(r0, CHUNK), lo:lo + HD])
            vs.append(v_ref[pl.ds(r0, CHUNK), lo:lo + HD])
            gam_c = gc[:, GL_GAMMA + hd:GL_GAMMA + hd + 1]
            cols.append((gc[:, GL_BETA + hd:GL_BETA + hd + 1], gc[:, GL_EG + hd:GL_EG + hd + 1],
                         gc[:, GL_EGR + hd:GL_EGR + hd + 1]))
            gam_r = gr_ref[cs[u]][hd:hd + 1, :]
            decs.append(jnp.exp(jnp.where(dirs[d][7], gam_c - gam_r, NEG_BIG)))
        kqs = [lax.dot_general(jnp.concatenate([ks[n], qs[n]], axis=0), ks[n],
                               (((1,), (1,)), ((), ())), preferred_element_type=F32)
               for n in range(npr)]
        a_s = [jnp.where(dirs[d][8], kqs[n][0:CHUNK] * cols[n][0] * decs[n], 0.0)
               for n, (u, d, h) in enumerate(probs)]
        for n, (u, d, h) in enumerate(probs):
            ab_ref[d, cs[u], h, 0:CHUNK, :] = kqs[n][CHUNK:] * decs[n]
            ab_ref[d, cs[u], h, CHUNK:, :] = (ks[n] * cols[n][2]).T
        ts = [eye - jnp.where(level_masks[0], a_s[n], 0.0) for n in range(npr)]
        for lm in level_masks[1:]:
            lts = [jnp.dot(jnp.where(lm, a_s[n], 0.0), ts[n], preferred_element_type=F32) for n in range(npr)]
            ts = [ts[n] - jnp.dot(ts[n], lts[n], preferred_element_type=F32) for n in range(npr)]
        uws =[jnp.dot(ts[n], jnp.concatenate([vs[n] * cols[n][0], ks[n] * (cols[n][0] * cols[n][1])], axis=1),
                       preferred_element_type=F32) for n in range(npr)]
        for n, (u, d, h) in enumerate(probs):
            u_ref[d, cs[u], h] = uws[n][:, 0:HD]
            wq_ref[d, cs[u], h, 0:CHUNK, :] = uws[n][:, HD:]
            wq_ref[d, cs[u], h, CHUNK:, :] = qs[n] * cols[n][1]
        return carry

    lax.fori_loop(0, nc // PREP_CHUNKS, prep_step, 0)

    def scan_step(c, carry):
        ccs = (c, nc - 1 - c)
        sts = [s_ref[d * NH + h] for d, h in hds]
        wqs = [jnp.dot(wq_ref[d, ccs[d], h], sts[n], preferred_element_type=F32)
               for n, (d, h) in enumerate(hds)]
        vns = [u_ref[d, ccs[d], h] - wqs[n][0:CHUNK] for n, (d, h) in enumerate(hds)]
        avs = [jnp.dot(ab_ref[d, ccs[d], h], vns[n], preferred_element_type=F32)
               for n, (d, h) in enumerate(hds)]
        for n, (d, h) in enumerate(hds):
            r0 = pl.multiple_of(ccs[d] * CHUNK, CHUNK)
            dirs[d][6][pl.ds(r0, CHUNK), h * HD:(h + 1) * HD] = (
                wqs[n][CHUNK:] + avs[n][0:CHUNK]).astype(dirs[d][6].dtype)
            ge = dirs[d][5][ccs[d]]
            hd = d * NH + h
            s_ref[hd] = sts[n] * ge[hd:hd + 1, :] + avs[n][CHUNK:]
        return carry

    lax.fori_loop(0, nc, scan_step, 0)


def gdn_call(reset_f, reset_b, q, k, v, gcol, grow, gend, *, tb):
    t = q.shape[0]
    nb = t // tb
    nc = tb // CHUNK
    fwd2 = lambda i, a, b: (i, 0)
    bwd2 = lambda i, a, b: (nb - 1 - i, 0)
    fwd3 = lambda i, a, b: (i, 0, 0)
    bwd3 = lambda i, a, b: (nb - 1 - i, 0, 0)

    def specs(m2, m3):
        return [
            pl.BlockSpec((tb, G_DIM), m2), pl.BlockSpec((tb, G_DIM), m2), pl.BlockSpec((tb, G_DIM), m2),
            pl.BlockSpec((tb, LANES), m2),
            pl.BlockSpec((nc, 2 * SUBLANES, CHUNK), m3),
            pl.BlockSpec((nc, 2 * SUBLANES, LANES), m3),
        ]

    grid_spec = pltpu.PrefetchScalarGridSpec(
        num_scalar_prefetch=2,
        grid=(nb,),
        in_specs=specs(fwd2, fwd3) + specs(bwd2, bwd3),
        out_specs=[pl.BlockSpec((tb, G_DIM), fwd2), pl.BlockSpec((tb, G_DIM), bwd2)],
        scratch_shapes=[pltpu.VMEM((2 * NH, HD, HD), F32),
                        pltpu.VMEM((2, nc, NH, 2 * CHUNK, HD), F32),
                        pltpu.VMEM((2, nc, NH, CHUNK + HD, CHUNK), F32),
                        pltpu.VMEM((2, nc, NH, CHUNK, HD), F32)],
    )
    return pl.pallas_call(
        _gdn_kernel,
        grid_spec=grid_spec,
        out_shape=[jax.ShapeDtypeStruct((t, G_DIM), BF16), jax.ShapeDtypeStruct((t, G_DIM), BF16)],
        compiler_params=_cparams(("arbitrary",)),
        name="gdn",
    )(reset_f, reset_b, q, k, v, gcol, grow, gend, q, k, v, gcol, grow, gend)


def _fft1_kernel(x_ref, m1_ref, tc_ref, ts_ref, br_ref, bi_ref):
    s1 = x_ref.shape[1]
    a = jnp.dot(m1_ref[...], x_ref[0], preferred_element_type=F32)
    ar, ai = a[0:s1], a[s1:]
    tc, ts = tc_ref[...], ts_ref[...]
    br_ref[0] = ar * tc + ai * ts
    bi_ref[0] = ai * tc - ar * ts


def fft1_call(x3, m1, twc, tws, *, tcol):
    nseq, s1, cols = x3.shape
    blk = pl.BlockSpec((1, s1, tcol), lambda j, b: (b, 0, j))
    tw = pl.BlockSpec((s1, tcol), lambda j, b: (0, j))
    return pl.pallas_call(
        _fft1_kernel,
        grid=(cols // tcol, nseq),
        in_specs=[blk, pl.BlockSpec(m1.shape, lambda j, b: (0, 0)), tw, tw],
        out_specs=[blk, blk],
        out_shape=[jax.ShapeDtypeStruct(x3.shape, F32)] * 2,
        compiler_params=_cparams(("arbitrary", "arbitrary")),
        name="fft_stage1",
    )(x3, m1, twc, tws)


def _fft2_kernel(br_ref, bi_ref, mc_ref, m2_ref, o_ref):
    _, tk, s2, c = br_ref.shape
    nsplit = o_ref.shape[0]
    s2o = s2 // nsplit
    b = jnp.concatenate([br_ref[0].reshape(tk * s2, c), bi_ref[0].reshape(tk * s2, c)], axis=1)
    z = jnp.dot(b, mc_ref[...], preferred_element_type=F32)
    m2 = m2_ref[...]
    for kk in range(tk):
        zk = z[kk * s2:(kk + 1) * s2]
        x = jnp.dot(m2, jnp.concatenate([zk[:, 0:c], zk[:, c:]], axis=0),
                    preferred_element_type=F32)
        for sp in range(nsplit):
            o_ref[sp, kk] = x[sp * s2o:(sp + 1) * s2o]


def fft2_call(br4, bi4, mc, m2, *, tk, nsplit):
    nseq, s1, s2, c = br4.shape
    s2o = s2 // nsplit
    blk = pl.BlockSpec((1, tk, s2, c), lambda b, j: (b, j, 0, 0))
    return pl.pallas_call(
        _fft2_kernel,
        grid=(nseq, s1 // tk),
        in_specs=[blk, blk, pl.BlockSpec(mc.shape, lambda b, j: (0, 0)),
                  pl.BlockSpec(m2.shape, lambda b, j: (0, 0))],
        out_specs=pl.BlockSpec((nsplit, tk, s2o, c), lambda b, j: (b, j, 0, 0)),
        out_shape=jax.ShapeDtypeStruct((nseq * nsplit, s1, s2o, c), F32),
        compiler_params=_cparams(("arbitrary", "arbitrary")),
        name="fft_stage2",
    )(br4, bi4, mc, m2)


def _dft_tables(s):
    s1 = DFT1
    s2 = s // s1
    k = np.arange(s1)
    ang1 = 2.0 * np.pi * ((k[:, None] * k[None, :]) % s1) / s1
    sc = 1.0 / math.sqrt(s)
    m1 = np.concatenate([np.cos(ang1), -np.sin(ang1)], axis=0) * sc
    n2 = np.arange(s2)
    angt = 2.0 * np.pi * ((k[:, None] * n2[None, :]) % s) / s
    twc = np.repeat(np.cos(angt), F_DIM, axis=1)
    tws = np.repeat(np.sin(angt), F_DIM, axis=1)
    ang2 = 2.0 * np.pi * ((n2[:, None] * n2[None, :]) % s2) / s2
    m2 = np.concatenate([np.cos(ang2), np.sin(ang2)], axis=1)
    return (jnp.asarray(m1, F32), jnp.asarray(np.cos(angt), F32), jnp.asarray(np.sin(angt), F32),
            jnp.asarray(m2, F32))


def _channel_dft_matrix():
    g = np.arange(GROUP_DIM)
    ang = 2.0 * np.pi * ((g[:, None] * g[None, :]) % GROUP_DIM) / GROUP_DIM
    cg = np.kron(np.eye(N_GROUPS), np.cos(ang)) / math.sqrt(GROUP_DIM)
    sg = np.kron(np.eye(N_GROUPS), np.sin(ang)) / math.sqrt(GROUP_DIM)
    return jnp.asarray(np.block([[cg, -sg], [sg, cg]]), F32)


def fourier_mix_call(f, *, nseq, s, tcol, tk, nsplit):
    s1 = DFT1
    s2 = s // s1
    m1, tcs, tss, m2 = _dft_tables(s)
    twc = jnp.broadcast_to(tcs[:, :, None], (s1, s2, F_DIM)).reshape(s1, s2 * F_DIM)
    tws = jnp.broadcast_to(tss[:, :, None], (s1, s2, F_DIM)).reshape(s1, s2 * F_DIM)
    x3 = f.reshape(nseq, s1, s2 * F_DIM)
    br, bi = fft1_call(x3, m1, twc, tws, tcol=tcol)
    out = fft2_call(br.reshape(nseq, s1, s2, F_DIM), bi.reshape(nseq, s1, s2, F_DIM),
                    _channel_dft_matrix(), m2, tk=tk, nsplit=nsplit)
    return out.reshape(nseq * nsplit, s1, (s2 // nsplit) * F_DIM)


def _mixer_out(x_ref, mod_ref, of_ref, ob_ref, z_ref, fm_ref, gn_ref, w_ref):
    tm = x_ref.shape[0]
    o = of_ref[...].astype(F32) + ob_ref[...].astype(F32)
    z = z_ref[...].astype(F32)
    gn = gn_ref[...]
    parts = []
    fm = fm_ref[...]
    parts.append(jnp.concatenate(
        [fm[:, j * F_DIM:(j + 1) * F_DIM] for j in range(tm // DFT1)], axis=0).astype(BF16))
    for h in range(NH):
        oh = o[:, h * HD:(h + 1) * HD]
        ms = jnp.mean(oh * oh, axis=-1, keepdims=True)
        y = (oh * lax.rsqrt(ms + NORM_EPS)) * gn
        parts.append((y * _silu(z[:, h * HD:(h + 1) * HD])).astype(BF16))
    mixed = jnp.concatenate(parts, axis=1)
    proj = jnp.dot(mixed, w_ref[...], preferred_element_type=F32)
    return x_ref[...] + mod_ref[0, 2:3, :] * proj


def _swiglu_residual(x, mod_ref, nw_ref, wg_ref, wu_ref, wd_ref, nsplit):
    h = _mod_norm(x, nw_ref[...], mod_ref[0, 4:5, :], mod_ref[0, 3:4, :]).astype(BF16)
    cw = wg_ref.shape[1] // nsplit
    acc = None
    for c in range(nsplit):
        g = jnp.dot(h, wg_ref[:, c * cw:(c + 1) * cw], preferred_element_type=F32)
        u = jnp.dot(h, wu_ref[:, c * cw:(c + 1) * cw], preferred_element_type=F32)
        hid = (_silu(g) * u).astype(BF16)
        part = jnp.dot(hid, wd_ref[c * cw:(c + 1) * cw, :], preferred_element_type=F32)
        acc = part if acc is None else acc + part
    return x + mod_ref[0, 5:6, :] * acc


def _outproj_kernel(x_ref, mod_ref, of_ref, ob_ref, z_ref, fm_ref, gn_ref, w_ref, o_ref):
    o_ref[...] = _mixer_out(x_ref, mod_ref, of_ref, ob_ref, z_ref, fm_ref, gn_ref, w_ref)


def _outproj_ffn_kernel(x_ref, mod_ref, of_ref, ob_ref, z_ref, fm_ref, gn_ref, w_ref,
                        nw_ref, wg_ref, wu_ref, wd_ref, o_ref, *, nsplit):
    x_mid = _mixer_out(x_ref, mod_ref, of_ref, ob_ref, z_ref, fm_ref, gn_ref, w_ref)
    o_ref[...] = _swiglu_residual(x_mid, mod_ref, nw_ref, wg_ref, wu_ref, wd_ref, nsplit)


def outproj_call(x, mod, o_f, o_b, z, fm2, gn, w, ffn=None, *, seg, tm):
    t, d = x.shape
    row = lambda i: (i, 0)
    const = lambda i: (0, 0)
    per_seg = seg // tm
    single = pl.Buffered(1)
    in_specs = [
        pl.BlockSpec((tm, d), row),
        pl.BlockSpec((1, SUBLANES, d), lambda i: (i // per_seg, 0, 0)),
        pl.BlockSpec((tm, G_DIM), row),
        pl.BlockSpec((tm, G_DIM), row),
        pl.BlockSpec((tm, G_DIM), row),
        pl.BlockSpec((DFT1, (tm // DFT1) * F_DIM), lambda i: (i // per_seg, i % per_seg)),
        pl.BlockSpec((1, HD), const),
        pl.BlockSpec(w.shape, const, pipeline_mode=single),
    ]
    args = [x, mod, o_f, o_b, z, fm2, gn, w]
    body = _outproj_kernel
    if ffn is not None:
        nw, wg, wu, wd = ffn
        in_specs += [pl.BlockSpec((1, d), const)] + [
            pl.BlockSpec(a.shape, const, pipeline_mode=single) for a in (wg, wu, wd)]
        args += [nw, wg, wu, wd]
        body = functools.partial(_outproj_ffn_kernel, nsplit=2)
    return pl.pallas_call(
        body,
        grid=(t // tm,),
        in_specs=in_specs,
        out_specs=pl.BlockSpec((tm, d), row),
        out_shape=jax.ShapeDtypeStruct((t, d), F32),
        compiler_params=_cparams(("arbitrary",)),
        name="outproj" if ffn is None else "outproj_ffn",
    )(*args)


def _router_kernel(x_ref, mod_ref, nw_ref, wr_ref, tri_ref, oi_ref, op_ref, cnt_ref, carry_ref):
    i = pl.program_id(0)
    tm = x_ref.shape[0]

    @pl.when(i == 0)
    def _():
        carry_ref[...] = jnp.zeros_like(carry_ref)

    h = _mod_norm(x_ref[...], nw_ref[...], mod_ref[0, 4:5, :], mod_ref[0, 3:4, :])
    logits = jnp.dot(h.astype(BF16), wr_ref[...], preferred_element_type=F32)
    lane = lax.broadcasted_iota(jnp.int32, (tm, LANES), 1)
    logits = jnp.where(lane < N_EXPERTS, logits, NEG_BIG)
    l1 = jnp.max(logits, axis=-1, keepdims=True)
    i1 = jnp.min(jnp.where(logits == l1, lane, LANES), axis=-1, keepdims=True)
    rest = jnp.where(lane == i1, NEG_BIG, logits)
    l2 = jnp.max(rest, axis=-1, keepdims=True)
    i2 = jnp.min(jnp.where(rest == l2, lane, LANES), axis=-1, keepdims=True)
    e21 = jnp.exp(l2 - l1)
    p1 = 1.0 / (1.0 + e21)
    p2 = e21 * p1
    oh1 = lane == i1
    oh2 = lane == i2
    oh = jnp.where(oh1 | oh2, 1.0, 0.0).astype(BF16)
    before = jnp.dot(tri_ref[...], oh, preferred_element_type=F32) + carry_ref[0:1, :]
    r1 = jnp.sum(jnp.where(oh1, before, 0.0), axis=-1, keepdims=True).astype(jnp.int32)
    r2 = jnp.sum(jnp.where(oh2, before, 0.0), axis=-1, keepdims=True).astype(jnp.int32)
    oi_ref[...] = jnp.where(lane == 0, i1, jnp.where(lane == 1, i2, jnp.where(lane == 2, r1, r2)))
    op_ref[...] = jnp.where(lane == 0, p1, p2)
    new_carry = carry_ref[0:1, :] + jnp.sum(oh.astype(F32), axis=0, keepdims=True)
    carry_ref[...] = jnp.broadcast_to(new_carry, carry_ref.shape)
    cnt_ref[...] = jnp.broadcast_to(new_carry, cnt_ref.shape)


def router_call(x, mod, nw, wr, tri, *, seg, tm):
    t, d = x.shape
    row = lambda i: (i, 0)
    const = lambda i: (0, 0)
    return pl.pallas_call(
        _router_kernel,
        grid=(t // tm,),
        in_specs=[
            pl.BlockSpec((tm, d), row),
            pl.BlockSpec((1, SUBLANES, d), lambda i: (i * tm // seg, 0, 0)),
            pl.BlockSpec((1, d), const),
            pl.BlockSpec(wr.shape, const),
            pl.BlockSpec(tri.shape, const),
        ],
        out_specs=[pl.BlockSpec((tm, LANES), row), pl.BlockSpec((tm, LANES), row),
                   pl.BlockSpec((SUBLANES, LANES), const)],
        out_shape=[jax.ShapeDtypeStruct((t, LANES), jnp.int32), jax.ShapeDtypeStruct((t, LANES), F32),
                   jax.ShapeDtypeStruct((SUBLANES, LANES), F32)],
        scratch_shapes=[pltpu.VMEM((SUBLANES, LANES), F32)],
        compiler_params=_cparams(("arbitrary",)),
        name="router",
    )(x, mod, nw, wr, tri)


def _dispatch_kernel(pos_ref, x_ref, mod_ref, nw_ref, xs_in_ref, xs_ref, hbuf, sem):
    del xs_in_ref
    i = pl.program_id(0)
    last = pl.num_programs(0) - 1
    tm = x_ref.shape[0]
    buf = i % 2
    hbuf[buf] = _mod_norm(x_ref[...], nw_ref[...], mod_ref[0, 4:5, :], mod_ref[0, 3:4, :])

    def start(r, c):
        for slot in range(2):
            pltpu.make_async_copy(hbuf.at[buf, pl.ds(r, 1), :],
                                  xs_ref.at[pl.ds(pos_ref[0, 0, slot * tm + r], 1), :],
                                  sem.at[buf]).start(priority=slot)
        return c

    lax.fori_loop(0, tm, start, 0, unroll=8)

    def wait_rows(b):
        for _ in range(2):
            pltpu.make_async_copy(hbuf.at[b], xs_ref.at[pl.ds(0, tm), :], sem.at[b]).wait()

    @pl.when(i > 0)
    def _():
        wait_rows(1 - buf)

    @pl.when(i == last)
    def _():
        wait_rows(buf)


def dispatch_call(pos3, x, mod, nw, xs_zero, *, seg, tm):
    t, d = x.shape
    return pl.pallas_call(
        _dispatch_kernel,
        grid=(t // tm,),
        in_specs=[
            pl.BlockSpec((1, 1, 2 * tm), lambda i: (i, 0, 0), memory_space=pltpu.SMEM),
            pl.BlockSpec((tm, d), lambda i: (i, 0)),
            pl.BlockSpec((1, SUBLANES, d), lambda i: (i * tm // seg, 0, 0)),
            pl.BlockSpec((1, d), lambda i: (0, 0)),
            pl.BlockSpec(memory_space=pl.ANY),
        ],
        out_specs=pl.BlockSpec(memory_space=pl.ANY),
        out_shape=jax.ShapeDtypeStruct(xs_zero.shape, F32),
        scratch_shapes=[pltpu.VMEM((2, tm, d), F32), pltpu.SemaphoreType.DMA((2,))],
        input_output_aliases={4: 0},
        compiler_params=_cparams(("arbitrary",)),
        name="moe_dispatch",
    )(pos3, x, mod, nw, xs_zero)


def _expert_kernel(te_ref, nu_ref, xs_ref, wg_ref, wu_ref, wd_ref, y_ref, acc_ref):
    j = pl.program_id(0)
    half = pl.program_id(1)

    @pl.when(j < nu_ref[0])
    def _():
        h = xs_ref[...].astype(BF16)
        g = jnp.dot(h, wg_ref[0], preferred_element_type=F32)
        u = jnp.dot(h, wu_ref[0], preferred_element_type=F32)
        hid = (_silu(g) * u).astype(BF16)
        part = jnp.dot(hid, wd_ref[0], preferred_element_type=F32)

        @pl.when(half == 0)
        def _():
            acc_ref[...] = part

        @pl.when(half == 1)
        def _():
            y_ref[...] = acc_ref[...] + part

    @pl.when((j >= nu_ref[0]) & (half == 1))
    def _():
        y_ref[...] = jnp.zeros_like(y_ref)


def expert_call(tile_expert, n_used, xs, wg, wu, wd, *, tme):
    nr, d = xs.shape
    ntiles = nr // tme
    fh = wg.shape[2] // 2

    def jj(j, nu):
        return jnp.minimum(j, nu[0] - 1)

    def hh(j, hf, nu):
        return jnp.where(j < nu[0], hf, 1)

    grid_spec = pltpu.PrefetchScalarGridSpec(
        num_scalar_prefetch=2,
        grid=(ntiles, 2),
        in_specs=[
            pl.BlockSpec((tme, d), lambda j, hf, te, nu: (jj(j, nu), 0)),
            pl.BlockSpec((1, d, fh), lambda j, hf, te, nu: (te[jj(j, nu)], 0, hh(j, hf, nu))),
            pl.BlockSpec((1, d, fh), lambda j, hf, te, nu: (te[jj(j, nu)], 0, hh(j, hf, nu))),
            pl.BlockSpec((1, fh, d), lambda j, hf, te, nu: (te[jj(j, nu)], hh(j, hf, nu), 0)),
        ],
        out_specs=pl.BlockSpec((tme, d), lambda j, hf, te, nu: (j, 0)),
        scratch_shapes=[pltpu.VMEM((tme, d), F32)],
    )
    return pl.pallas_call(
        _expert_kernel,
        grid_spec=grid_spec,
        out_shape=jax.ShapeDtypeStruct((nr, d), F32),
        compiler_params=_cparams(("arbitrary", "arbitrary")),
        name="moe_experts",
    )(tile_expert, n_used, xs, wg, wu, wd)


def _combine_kernel(pos_ref, posn_ref, x_ref, mod_ref, p_ref, y_ref, o_ref, ybuf, sem):
    i = pl.program_id(0)
    last = pl.num_programs(0) - 1
    tm = x_ref.shape[0]
    buf = i % 2

    def gather(idx_ref, b):
        def start(r, c):
            for slot in range(2):
                pltpu.make_async_copy(y_ref.at[pl.ds(idx_ref[0, 0, slot * tm + r], 1), :],
                                      ybuf.at[b, slot, pl.ds(r, 1), :], sem.at[b]).start(priority=slot)
            return c

        lax.fori_loop(0, tm, start, 0, unroll=8)

    @pl.when(i == 0)
    def _():
        gather(pos_ref, buf)

    @pl.when(i < last)
    def _():
        gather(posn_ref, 1 - buf)

    for slot in range(2):
        pltpu.make_async_copy(y_ref.at[pl.ds(0, tm), :], ybuf.at[buf, slot], sem.at[buf]).wait()
    p = p_ref[...]
    f = p[:, 0:1] * ybuf[buf, 0] + p[:, 1:2] * ybuf[buf, 1]
    o_ref[...] = x_ref[...] + mod_ref[0, 5:6, :] * f


def combine_call(pos3, x, mod, p, y, *, seg, tm):
    t, d = x.shape
    nt = t // tm
    return pl.pallas_call(
        _combine_kernel,
        grid=(nt,),
        in_specs=[
            pl.BlockSpec((1, 1, 2 * tm), lambda i: (i, 0, 0), memory_space=pltpu.SMEM),
            pl.BlockSpec((1, 1, 2 * tm), lambda i: (jnp.minimum(i + 1, nt - 1), 0, 0),
                         memory_space=pltpu.SMEM),
            pl.BlockSpec((tm, d), lambda i: (i, 0)),
            pl.BlockSpec((1, SUBLANES, d), lambda i: (i * tm // seg, 0, 0)),
            pl.BlockSpec((tm, LANES), lambda i: (i, 0)),
            pl.BlockSpec(memory_space=pl.ANY),
        ],
        out_specs=pl.BlockSpec((tm, d), lambda i: (i, 0)),
        out_shape=jax.ShapeDtypeStruct((t, d), F32),
        scratch_shapes=[pltpu.VMEM((2, 2, tm, d), F32), pltpu.SemaphoreType.DMA((2,))],
        compiler_params=_cparams(("arbitrary",)),
        name="moe_combine",
    )(pos3, pos3, x, mod, p, y)


def moe_block(x, mod, nw, wr, wg, wu, wd, *, seg, tm, tmd, tme):
    t, d = x.shape
    wr_pad = jnp.zeros((d, LANES), BF16).at[:, :N_EXPERTS].set(wr.astype(BF16))
    tri = jnp.asarray(np.tril(np.ones((tm, tm), np.float32), k=-1), BF16)
    oi, op, cnt = router_call(x, mod, nw, wr_pad, tri, seg=seg, tm=tm)
    counts = cnt[0, :N_EXPERTS].astype(jnp.int32)
    padded = ((counts + tme - 1) // tme) * tme
    ends = jnp.cumsum(padded)
    starts = ends - padded
    nr = ((2 * t + N_EXPERTS * (tme - 1)) // tme) * tme
    ntiles = nr // tme
    n_used = (ends[-1] // tme).astype(jnp.int32).reshape(1)
    tile_start = jnp.arange(ntiles, dtype=jnp.int32) * tme
    tile_expert = jnp.minimum(jnp.sum(tile_start[:, None] >= ends[None, :], axis=1),
                              N_EXPERTS - 1).astype(jnp.int32)
    pos1 = starts[oi[:, 0]] + oi[:, 2]
    pos2 = starts[oi[:, 1]] + oi[:, 3]
    pos3 = jnp.concatenate([pos1.reshape(t // tmd, 1, tmd), pos2.reshape(t // tmd, 1, tmd)], axis=2)
    xs = dispatch_call(pos3, x, mod, nw, jnp.zeros((nr, d), F32), seg=seg, tm=tmd)
    y = expert_call(tile_expert, n_used, xs, wg, wu, wd, tme=tme)
    return combine_call(pos3, x, mod, op, y, seg=seg, tm=tmd)


def _final_kernel(x_ref, w_ref, o_ref):
    x = x_ref[...]
    ms = jnp.mean(x * x, axis=-1, keepdims=True)
    o_ref[...] = (x * lax.rsqrt(ms + NORM_EPS)) * w_ref[...]


def final_call(x, w, *, row0, nrows, tm):
    d = x.shape[1]
    off = row0 // tm
    return pl.pallas_call(
        _final_kernel,
        grid=(nrows // tm,),
        in_specs=[pl.BlockSpec((tm, d), lambda i: (i + off, 0)), pl.BlockSpec((1, d), lambda i: (0, 0))],
        out_specs=pl.BlockSpec((tm, d), lambda i: (i, 0)),
        out_shape=jax.ShapeDtypeStruct((nrows, d), F32),
        compiler_params=_cparams(("arbitrary",)),
        name="final_norm",
    )(x, w)


def _segment_flags(prompt_segs, sample_segs, seg, tile):
    per_seg = seg // tile
    nseg = prompt_segs + sample_segs
    first = np.zeros(nseg * per_seg, np.int32)
    last = np.zeros(nseg * per_seg, np.int32)
    first[0] = 1
    last[prompt_segs * per_seg - 1] = 1
    for s in range(prompt_segs, nseg):
        first[s * per_seg] = 1
        last[(s + 1) * per_seg - 1] = 1
    return first, last


def encoder_pair(x_prompt, x_sample, c_prompt, c_sample, w_ada, b_ada, norm_mix, norm_ffn, w_in, conv_w,
                 a_log, dt_bias, gdn_norm, w_out, w_ffn_gate, w_ffn_up, w_ffn_down, w_router,
                 w_exp_gate, w_exp_up, w_exp_down, norm_final, *, tm=512, tb=256, tmd=256, tme=512):
    bp, sp, d = x_prompt.shape
    bs, seg, _ = x_sample.shape
    assert bp == 1 and sp % seg == 0 and seg % tm == 0 and tm % DFT1 == 0
    depth = w_ada.shape[0]
    prompt_segs = sp // seg
    nseg = prompt_segs + bs
    t = nseg * seg
    x = jnp.concatenate([x_prompt.reshape(sp, d), x_sample.reshape(bs * seg, d)], axis=0)

    nrow = -(-(1 + bs) // SUBLANES) * SUBLANES
    c_all = jnp.zeros((nrow, d), F32).at[0:1].set(c_prompt).at[1:1 + bs].set(c_sample)
    ada = ada_call(c_all, w_ada, b_ada).reshape(depth, nrow, 6, d)
    seg_row = np.concatenate([np.zeros(prompt_segs, np.int32), 1 + np.arange(bs, dtype=np.int32)])
    mod_all = jnp.pad(ada[:, seg_row], ((0, 0), (0, 0), (0, SUBLANES - 6), (0, 0)))

    first_c, last_c = _segment_flags(prompt_segs, bs, seg, tm)
    first_g, last_g = _segment_flags(prompt_segs, bs, seg, tb)
    reset_f = jnp.asarray(first_g)
    reset_b = jnp.asarray(last_g[::-1].copy())
    first_c, last_c = jnp.asarray(first_c), jnp.asarray(last_c)

    off_f, off_qkv, off_z = F_DIM, F_DIM + QKV_DIM, F_DIM + QKV_DIM + G_DIM
    off_b = off_z + 2 * NH
    for l in range(depth):
        mod = mod_all[l]
        wl = w_in[l]
        wf = wl[:, :off_f].astype(BF16)
        wqkv = wl[:, off_f:off_qkv].astype(BF16)
        wz = wl[:, off_qkv:off_z].astype(BF16)
        wba = jnp.pad(wl[:, off_z:], ((0, 0), (0, LANES - 4 * NH))).astype(BF16)
        cw = jnp.pad(conv_w[l], ((0, SUBLANES - CONV_K), (0, 0)))
        gp = jnp.zeros((SUBLANES, LANES), F32)
        gp = gp.at[0, 2 * NH:4 * NH].set(dt_bias[l].reshape(-1)).at[1, 2 * NH:4 * NH].set(a_log[l].reshape(-1))
        f, z, q, k, v, gcol, grow, gend = inproj_call(first_c, last_c, x, mod, norm_mix[l].reshape(1, d),
                                                      wf, wqkv, wz, wba, cw, gp, seg=seg, tm=tm)
        o_f, o_b = gdn_call(reset_f, reset_b, q, k, v, gcol, grow, gend, tb=tb)

        s2p = sp // DFT1
        fm_p = fourier_mix_call(f[:sp], nseq=1, s=sp, tcol=min(2048, s2p * F_DIM),
                                tk=min(16, DFT1), nsplit=prompt_segs)
        s2s = seg // DFT1
        fm_s = fourier_mix_call(f[sp:], nseq=bs, s=seg, tcol=min(2048, s2s * F_DIM),
                                tk=min(64, DFT1), nsplit=1)
        fm2 = jnp.concatenate([fm_p, fm_s], axis=0).reshape(nseg * DFT1, (seg // DFT1) * F_DIM)

        nw = norm_ffn[l].reshape(1, d)
        i = l // 2
        ffn = None
        if l % 2 == 0:
            ffn = (nw, w_ffn_gate[i].astype(BF16), w_ffn_up[i].astype(BF16), w_ffn_down[i].astype(BF16))
        x = outproj_call(x, mod, o_f, o_b, z, fm2, gdn_norm[l].reshape(1, HD), w_out[l].astype(BF16), ffn,
                         seg=seg, tm=tm)
        if l % 2 == 1:
            x = moe_block(x, mod, nw, w_router[i], w_exp_gate[i].astype(BF16), w_exp_up[i].astype(BF16),
                          w_exp_down[i].astype(BF16), seg=seg, tm=tm, tmd=tmd, tme=tme)

    wn = norm_final.reshape(1, d)
    y_p = final_call(x, wn, row0=0, nrows=sp, tm=tm).reshape(bp, sp, d)
    y_s = final_call(x, wn, row0=sp, nrows=bs * seg, tm=tm).reshape(bs, seg, d)
    return y_p, y_s


def kernel(x_prompt, x_sample, c_prompt, c_sample, w_ada, b_ada, norm_mix, norm_ffn, w_in, conv_w, a_log,
           dt_bias, gdn_norm, w_out, w_ffn_gate, w_ffn_up, w_ffn_down, w_router, w_exp_gate, w_exp_up,
           w_exp_down, norm_final):
    return encoder_pair(x_prompt, x_sample, c_prompt, c_sample, w_ada, b_ada, norm_mix, norm_ffn, w_in,
                        conv_w, a_log, dt_bias, gdn_norm, w_out, w_ffn_gate, w_ffn_up, w_ffn_down,
                        w_router, w_exp_gate, w_exp_up, w_exp_down, norm_final)
```

```python
import functools
import math

import numpy as np
import jax
import jax.numpy as jnp
from jax import lax
from jax.experimental import pallas as pl
from jax.experimental.pallas import tpu as pltpu

F32 = jnp.float32
BF16 = jnp.bfloat16

D_MODEL = 1024
DEPTH = 4
N_GROUPS = 4
GROUP_DIM = 64
F_DIM = N_GROUPS * GROUP_DIM
HD = 128
NH = 6
G_DIM = NH * HD
QKV_DIM = 3 * G_DIM
CONV_K = 5
CHUNK = 64
PREP_CHUNKS = 2
D_FF = 2816
N_EXPERTS = 8
D_EXPERT = 3584
NORM_EPS = 1e-6

LANES = 128
SUBLANES = 8
MXU_COLS = 256
VMEM_LIMIT = 56 * 1024 * 1024
DFT1 = 128

GL_GAMMA, GL_BETA, GL_EG, GL_EGR = 0, 16, 32, 48
NEG_BIG = -1e30


def _cparams(sem):
    return pltpu.CompilerParams(dimension_semantics=sem, vmem_limit_bytes=VMEM_LIMIT)


def _mod_norm(x, nw, sc, sh):
    ms = jnp.mean(x * x, axis=-1, keepdims=True)
    y = x * lax.rsqrt(ms + NORM_EPS)
    return (y * nw) * (1.0 + sc) + sh


def _silu(x):
    hx = 0.5 * x
    return hx + hx * jnp.tanh(hx)


def _ada_kernel(c_ref, w_ref, b_ref, o_ref):
    c = _silu(c_ref[...])
    o_ref[0] = jnp.dot(c.astype(BF16), w_ref[0].astype(BF16), preferred_element_type=F32) + b_ref[0]


def ada_call(c_all, w_ada, b_ada):
    nrow = c_all.shape[0]
    depth, d, d6 = w_ada.shape
    tn = 1024
    return pl.pallas_call(
        _ada_kernel,
        grid=(depth, d6 // tn),
        in_specs=[
            pl.BlockSpec((nrow, d), lambda l, j: (0, 0)),
            pl.BlockSpec((1, d, tn), lambda l, j: (l, 0, j)),
            pl.BlockSpec((1, 1, tn), lambda l, j: (l, 0, j)),
        ],
        out_specs=pl.BlockSpec((1, nrow, tn), lambda l, j: (l, 0, j)),
        out_shape=jax.ShapeDtypeStruct((depth, nrow, d6), F32),
        compiler_params=_cparams(("arbitrary", "arbitrary")),
        name="ada",
    )(c_all, w_ada, b_ada.reshape(depth, 1, d6))


def _inproj_kernel(first_ref, last_ref,
                   x_ref, xp_ref, xn_ref, mod_ref, nw_ref, wf_ref, wqkv_ref, wz_ref, wba_ref, cw_ref, gp_ref,
                   f_ref, z_ref, q_ref, k_ref, v_ref, gcol_ref, grow_ref, gend_ref,
                   xe_ref):
    i = pl.program_id(0)
    tm = x_ref.shape[0]
    nc = tm // CHUNK
    halo = SUBLANES
    xcat = jnp.concatenate([xp_ref[...], x_ref[...], xn_ref[...]], axis=0)
    h = _mod_norm(xcat, nw_ref[...], mod_ref[0, 1:2, :], mod_ref[0, 0:1, :]).astype(BF16)
    pm = jnp.where(first_ref[i] == 1, 0.0, 1.0)
    nm = jnp.where(last_ref[i] == 1, 0.0, 1.0)
    outs = (q_ref, k_ref, v_ref)
    for cb in range(QKV_DIM // MXU_COLS):
        c0 = cb * MXU_COLS
        res = jnp.dot(h, wqkv_ref[:, c0:c0 + MXU_COLS], preferred_element_type=F32)
        xe_ref[0:halo, c0:c0 + MXU_COLS] = res[0:halo] * pm
        xe_ref[halo:halo + tm, c0:c0 + MXU_COLS] = res[halo:halo + tm]
        xe_ref[halo + tm:, c0:c0 + MXU_COLS] = res[halo + tm:] * nm
        for s in range(cb * (MXU_COLS // HD), (cb + 1) * (MXU_COLS // HD)):
            lo = s * HD
            xs = xe_ref[:, lo:lo + HD]
            acc = None
            for j in range(CONV_K):
                shift = (CONV_K // 2 - j) % (tm + 2 * halo)
                xj = xs if shift == 0 else pltpu.roll(xs, shift, axis=0)
                term = xj[halo:halo + tm] * cw_ref[j:j + 1, lo:lo + HD]
                acc = term if acc is None else acc + term
            y = _silu(acc)
            which, head = divmod(s, NH)
            if which < 2:
                y = y * lax.rsqrt(jnp.sum(y * y, axis=-1, keepdims=True) + 1e-6)
            if which == 0:
                y = y * (HD ** -0.5)
            outs[which][:, head * HD:(head + 1) * HD] = y
    f_ref[...] = jnp.dot(h, wf_ref[...], preferred_element_type=F32)[halo:halo + tm]
    z_ref[...] = jnp.dot(h, wz_ref[...], preferred_element_type=F32)[halo:halo + tm].astype(z_ref.dtype)
    ba = jnp.dot(h, wba_ref[...], preferred_element_type=F32)[halo:halo + tm]

    lane = lax.broadcasted_iota(jnp.int32, (tm, LANES), 1)
    beta = 1.0 / (1.0 + jnp.exp(-ba))
    xs = ba + gp_ref[0:1, :]
    softplus = jnp.maximum(xs, 0.0) + jnp.log(1.0 + jnp.exp(-jnp.abs(xs)))
    g = -jnp.exp(gp_ref[1:2, :]) * softplus
    g = jnp.where((lane >= 2 * NH) & (lane < 4 * NH), g, 0.0)
    g = pltpu.roll(g, LANES - 2 * NH, axis=1)

    rowc = lax.broadcasted_iota(jnp.int32, (tm, LANES), 0) % CHUNK
    p = g
    sh = 1
    while sh < CHUNK:
        p = p + jnp.where(rowc >= sh, pltpu.roll(p, sh, axis=0), 0.0)
        sh *= 2
    g3 = g.reshape(nc, CHUNK, LANES)
    tot = jnp.broadcast_to(jnp.sum(g3, axis=1, keepdims=True), (nc, CHUNK, LANES)).reshape(tm, LANES)
    is_bwd = (lane >= NH) & (lane < 2 * NH)
    gamma = jnp.where(is_bwd, tot - p + g, p)
    eg = jnp.exp(gamma)
    egr = jnp.exp(tot - gamma)
    m12 = lane < 2 * NH
    gcol = (jnp.where(m12, gamma, 0.0)
            + pltpu.roll(jnp.where(m12, beta, 0.0), GL_BETA, axis=1)
            + pltpu.roll(jnp.where(m12, eg, 0.0), GL_EG, axis=1)
            + pltpu.roll(jnp.where(m12, egr, 0.0), GL_EGR, axis=1))
    gcol_ref[...] = gcol
    gam_t = jnp.where(m12, gamma, 0.0).T[0:2 * SUBLANES, :]
    g_t = g.T[0:2 * SUBLANES, :]
    for c in range(nc):
        grow_ref[c] = gam_t[:, c * CHUNK:(c + 1) * CHUNK]
        tc = jnp.sum(g_t[:, c * CHUNK:(c + 1) * CHUNK], axis=-1, keepdims=True)
        gend_ref[c] = jnp.exp(jnp.broadcast_to(tc, (2 * SUBLANES, LANES)))


def inproj_call(first, last, x, mod, nw, wf, wqkv, wz, wba, cw, gp, *, seg, tm):
    t, d = x.shape
    nblk8 = t // SUBLANES
    r8 = tm // SUBLANES
    nc = tm // CHUNK
    row = lambda i, f, l: (i, 0)
    const = lambda i, f, l: (0, 0)
    chunked = lambda i, f, l: (i, 0, 0)
    single = pl.Buffered(1)
    grid_spec = pltpu.PrefetchScalarGridSpec(
        num_scalar_prefetch=2,
        grid=(t // tm,),
        in_specs=[
            pl.BlockSpec((tm, d), row),
            pl.BlockSpec((SUBLANES, d), lambda i, f, l: (jnp.maximum(i * r8 - 1, 0), 0)),
            pl.BlockSpec((SUBLANES, d), lambda i, f, l: (jnp.minimum((i + 1) * r8, nblk8 - 1), 0)),
            pl.BlockSpec((1, SUBLANES, d), lambda i, f, l: (i * tm // seg, 0, 0)),
            pl.BlockSpec((1, d), const),
            pl.BlockSpec(wf.shape, const, pipeline_mode=single),
            pl.BlockSpec(wqkv.shape, const, pipeline_mode=single),
            pl.BlockSpec(wz.shape, const, pipeline_mode=single),
            pl.BlockSpec(wba.shape, const, pipeline_mode=single),
            pl.BlockSpec((SUBLANES, QKV_DIM), const),
            pl.BlockSpec((SUBLANES, LANES), const),
        ],
        out_specs=[
            pl.BlockSpec((tm, F_DIM), row),
            pl.BlockSpec((tm, G_DIM), row),
            pl.BlockSpec((tm, G_DIM), row),
            pl.BlockSpec((tm, G_DIM), row),
            pl.BlockSpec((tm, G_DIM), row),
            pl.BlockSpec((tm, LANES), row),
            pl.BlockSpec((nc, 2 * SUBLANES, CHUNK), chunked),
            pl.BlockSpec((nc, 2 * SUBLANES, LANES), chunked),
        ],
        scratch_shapes=[pltpu.VMEM((tm + 2 * SUBLANES, QKV_DIM), F32)],
    )
    return pl.pallas_call(
        _inproj_kernel,
        grid_spec=grid_spec,
        out_shape=[
            jax.ShapeDtypeStruct((t, F_DIM), F32),
            jax.ShapeDtypeStruct((t, G_DIM), BF16),
            jax.ShapeDtypeStruct((t, G_DIM), F32),
            jax.ShapeDtypeStruct((t, G_DIM), F32),
            jax.ShapeDtypeStruct((t, G_DIM), F32),
            jax.ShapeDtypeStruct((t, LANES), F32),
            jax.ShapeDtypeStruct((t // CHUNK, 2 * SUBLANES, CHUNK), F32),
            jax.ShapeDtypeStruct((t // CHUNK, 2 * SUBLANES, LANES), F32),
        ],
        compiler_params=_cparams(("arbitrary",)),
        name="inproj_conv",
    )(first, last, x, x, x, mod, nw, wf, wqkv, wz, wba, cw, gp)


def _gdn_kernel(rf_ref, rb_ref,
                qf_ref, kf_ref, vf_ref, gcf_ref, grf_ref, gef_ref,
                qb_ref, kb_ref, vb_ref, gcb_ref, grb_ref, geb_ref,
                of_ref, ob_ref, s_ref, wq_ref, ab_ref, u_ref):
    i = pl.program_id(0)
    tb = qf_ref.shape[0]
    nc = tb // CHUNK

    @pl.when(rf_ref[i] == 1)
    def _():
        s_ref[0:NH] = jnp.zeros((NH, HD, HD), F32)

    @pl.when(rb_ref[i] == 1)
    def _():
        s_ref[NH:2 * NH] = jnp.zeros((NH, HD, HD), F32)

    row = lax.broadcasted_iota(jnp.int32, (CHUNK, CHUNK), 0)
    col = lax.broadcasted_iota(jnp.int32, (CHUNK, CHUNK), 1)
    eye = jnp.where(row == col, 1.0, 0.0).astype(F32)
    level_masks = []
    b = 1
    while b < CHUNK:
        level_masks.append((row // (2 * b) == col // (2 * b)) & (row // b != col // b))
        b *= 2
    dirs = (
        (qf_ref, kf_ref, vf_ref, gcf_ref, grf_ref, gef_ref, of_ref, row >= col, row > col),
        (qb_ref, kb_ref, vb_ref, gcb_ref, grb_ref, geb_ref, ob_ref, row <= col, row < col),
    )

    hds = [(d, h) for d in range(2) for h in range(NH)]
    probs = [(u, d, h) for u in range(PREP_CHUNKS) for d, h in hds]
    npr = len(probs)

    def prep_step(it, carry):
        cs = [it * PREP_CHUNKS + u for u in range(PREP_CHUNKS)]
        qs, ks, vs, cols, decs = [], [], [], [], []
        for u, d, h in probs:
            r0 = pl.multiple_of(cs[u] * CHUNK, CHUNK)
            q_ref, k_ref, v_ref, gc_ref, gr_ref = dirs[d][0:5]
            hd = d * NH + h
            lo = h * HD
            gc = gc_ref[pl.ds(r0, CHUNK), :]
            qs.append(q_ref[pl.ds(r0, CHUNK), lo:lo + HD])
            ks.append(k_ref[pl.ds(r0, CHUNK), lo:lo + HD])
            vs.append(v_ref[pl.ds(r0, CHUNK), lo:lo + HD])
            gam_c = gc[:, GL_GAMMA + hd:GL_GAMMA + hd + 1]
            cols.append((gc[:, GL_BETA + hd:GL_BETA + hd + 1], gc[:, GL_EG + hd:GL_EG + hd + 1],
                         gc[:, GL_EGR + hd:GL_EGR + hd + 1]))
            gam_r = gr_ref[cs[u]][hd:hd + 1, :]
            decs.append(jnp.exp(jnp.where(dirs[d][7], gam_c - gam_r, NEG_BIG)))
        kqs = [lax.dot_general(jnp.concatenate([ks[n], qs[n]], axis=0), ks[n],
                               (((1,), (1,)), ((), ())), preferred_element_type=F32)
               for n in range(npr)]
        a_s = [jnp.where(dirs[d][8], kqs[n][0:CHUNK] * cols[n][0] * decs[n], 0.0)
               for n, (u, d, h) in enumerate(probs)]
        for n, (u, d, h) in enumerate(probs):
            ab_ref[d, cs[u], h, 0:CHUNK, :] = kqs[n][CHUNK:] * decs[n]
            ab_ref[d, cs[u], h, CHUNK:, :] = (ks[n] * cols[n][2]).T
        ts = [eye - jnp.where(level_masks[0], a_s[n], 0.0) for n in range(npr)]
        for lm in level_masks[1:]:
            lts = [jnp.dot(jnp.where(lm, a_s[n], 0.0), ts[n], preferred_element_type=F32) for n in range(npr)]
            ts = [ts[n] - jnp.dot(ts[n], lts[n], preferred_element_type=F32) for n in range(npr)]
        uws =[jnp.dot(ts[n], jnp.concatenate([vs[n] * cols[n][0], ks[n] * (cols[n][0] * cols[n][1])], axis=1),
                       preferred_element_type=F32) for n in range(npr)]
        for n, (u, d, h) in enumerate(probs):
            u_ref[d, cs[u], h] = uws[n][:, 0:HD]
            wq_ref[d, cs[u], h, 0:CHUNK, :] = uws[n][:, HD:]
            wq_ref[d, cs[u], h, CHUNK:, :] = qs[n] * cols[n][1]
        return carry

    lax.fori_loop(0, nc // PREP_CHUNKS, prep_step, 0)

    def scan_step(c, carry):
        ccs = (c, nc - 1 - c)
        sts = [s_ref[d * NH + h] for d, h in hds]
        wqs = [jnp.dot(wq_ref[d, ccs[d], h], sts[n], preferred_element_type=F32)
               for n, (d, h) in enumerate(hds)]
        vns = [u_ref[d, ccs[d], h] - wqs[n][0:CHUNK] for n, (d, h) in enumerate(hds)]
        avs = [jnp.dot(ab_ref[d, ccs[d], h], vns[n], preferred_element_type=F32)
               for n, (d, h) in enumerate(hds)]
        for n, (d, h) in enumerate(hds):
            r0 = pl.multiple_of(ccs[d] * CHUNK, CHUNK)
            dirs[d][6][pl.ds(r0, CHUNK), h * HD:(h + 1) * HD] = (
                wqs[n][CHUNK:] + avs[n][0:CHUNK]).astype(dirs[d][6].dtype)
            ge = dirs[d][5][ccs[d]]
            hd = d * NH + h
            s_ref[hd] = sts[n] * ge[hd:hd + 1, :] + avs[n][CHUNK:]
        return carry

    lax.fori_loop(0, nc, scan_step, 0)


def gdn_call(reset_f, reset_b, q, k, v, gcol, grow, gend, *, tb):
    t = q.shape[0]
    nb = t // tb
    nc = tb // CHUNK
    fwd2 = lambda i, a, b: (i, 0)
    bwd2 = lambda i, a, b: (nb - 1 - i, 0)
    fwd3 = lambda i, a, b: (i, 0, 0)
    bwd3 = lambda i, a, b: (nb - 1 - i, 0, 0)

    def specs(m2, m3):
        return [
            pl.BlockSpec((tb, G_DIM), m2), pl.BlockSpec((tb, G_DIM), m2), pl.BlockSpec((tb, G_DIM), m2),
            pl.BlockSpec((tb, LANES), m2),
            pl.BlockSpec((nc, 2 * SUBLANES, CHUNK), m3),
            pl.BlockSpec((nc, 2 * SUBLANES, LANES), m3),
        ]

    grid_spec = pltpu.PrefetchScalarGridSpec(
        num_scalar_prefetch=2,
        grid=(nb,),
        in_specs=specs(fwd2, fwd3) + specs(bwd2, bwd3),
        out_specs=[pl.BlockSpec((tb, G_DIM), fwd2), pl.BlockSpec((tb, G_DIM), bwd2)],
        scratch_shapes=[pltpu.VMEM((2 * NH, HD, HD), F32),
                        pltpu.VMEM((2, nc, NH, 2 * CHUNK, HD), F32),
                        pltpu.VMEM((2, nc, NH, CHUNK + HD, CHUNK), F32),
                        pltpu.VMEM((2, nc, NH, CHUNK, HD), F32)],
    )
    return pl.pallas_call(
        _gdn_kernel,
        grid_spec=grid_spec,
        out_shape=[jax.ShapeDtypeStruct((t, G_DIM), BF16), jax.ShapeDtypeStruct((t, G_DIM), BF16)],
        compiler_params=_cparams(("arbitrary",)),
        name="gdn",
    )(reset_f, reset_b, q, k, v, gcol, grow, gend, q, k, v, gcol, grow, gend)


def _fft1_kernel(x_ref, m1_ref, tc_ref, ts_ref, br_ref, bi_ref):
    s1 = x_ref.shape[1]
    a = jnp.dot(m1_ref[...], x_ref[0], preferred_element_type=F32)
    ar, ai = a[0:s1], a[s1:]
    tc, ts = tc_ref[...], ts_ref[...]
    br_ref[0] = ar * tc + ai * ts
    bi_ref[0] = ai * tc - ar * ts


def fft1_call(x3, m1, twc, tws, *, tcol):
    nseq, s1, cols = x3.shape
    blk = pl.BlockSpec((1, s1, tcol), lambda j, b: (b, 0, j))
    tw = pl.BlockSpec((s1, tcol), lambda j, b: (0, j))
    return pl.pallas_call(
        _fft1_kernel,
        grid=(cols // tcol, nseq),
        in_specs=[blk, pl.BlockSpec(m1.shape, lambda j, b: (0, 0)), tw, tw],
        out_specs=[blk, blk],
        out_shape=[jax.ShapeDtypeStruct(x3.shape, F32)] * 2,
        compiler_params=_cparams(("arbitrary", "arbitrary")),
        name="fft_stage1",
    )(x3, m1, twc, tws)


def _fft2_kernel(br_ref, bi_ref, mc_ref, m2_ref, o_ref):
    _, tk, s2, c = br_ref.shape
    nsplit = o_ref.shape[0]
    s2o = s2 // nsplit
    b = jnp.concatenate([br_ref[0].reshape(tk * s2, c), bi_ref[0].reshape(tk * s2, c)], axis=1)
    z = jnp.dot(b, mc_ref[...], preferred_element_type=F32)
    m2 = m2_ref[...]
    for kk in range(tk):
        zk = z[kk * s2:(kk + 1) * s2]
        x = jnp.dot(m2, jnp.concatenate([zk[:, 0:c], zk[:, c:]], axis=0),
                    preferred_element_type=F32)
        for sp in range(nsplit):
            o_ref[sp, kk] = x[sp * s2o:(sp + 1) * s2o]


def fft2_call(br4, bi4, mc, m2, *, tk, nsplit):
    nseq, s1, s2, c = br4.shape
    s2o = s2 // nsplit
    blk = pl.BlockSpec((1, tk, s2, c), lambda b, j: (b, j, 0, 0))
    return pl.pallas_call(
        _fft2_kernel,
        grid=(nseq, s1 // tk),
        in_specs=[blk, blk, pl.BlockSpec(mc.shape, lambda b, j: (0, 0)),
                  pl.BlockSpec(m2.shape, lambda b, j: (0, 0))],
        out_specs=pl.BlockSpec((nsplit, tk, s2o, c), lambda b, j: (b, j, 0, 0)),
        out_shape=jax.ShapeDtypeStruct((nseq * nsplit, s1, s2o, c), F32),
        compiler_params=_cparams(("arbitrary", "arbitrary")),
        name="fft_stage2",
    )(br4, bi4, mc, m2)


def _dft_tables(s):
    s1 = DFT1
    s2 = s // s1
    k = np.arange(s1)
    ang1 = 2.0 * np.pi * ((k[:, None] * k[None, :]) % s1) / s1
    sc = 1.0 / math.sqrt(s)
    m1 = np.concatenate([np.cos(ang1), -np.sin(ang1)], axis=0) * sc
    n2 = np.arange(s2)
    angt = 2.0 * np.pi * ((k[:, None] * n2[None, :]) % s) / s
    twc = np.repeat(np.cos(angt), F_DIM, axis=1)
    tws = np.repeat(np.sin(angt), F_DIM, axis=1)
    ang2 = 2.0 * np.pi * ((n2[:, None] * n2[None, :]) % s2) / s2
    m2 = np.concatenate([np.cos(ang2), np.sin(ang2)], axis=1)
    return (jnp.asarray(m1, F32), jnp.asarray(np.cos(angt), F32), jnp.asarray(np.sin(angt), F32),
            jnp.asarray(m2, F32))


def _channel_dft_matrix():
    g = np.arange(GROUP_DIM)
    ang = 2.0 * np.pi * ((g[:, None] * g[None, :]) % GROUP_DIM) / GROUP_DIM
    cg = np.kron(np.eye(N_GROUPS), np.cos(ang)) / math.sqrt(GROUP_DIM)
    sg = np.kron(np.eye(N_GROUPS), np.sin(ang)) / math.sqrt(GROUP_DIM)
    return jnp.asarray(np.block([[cg, -sg], [sg, cg]]), F32)


def fourier_mix_call(f, *, nseq, s, tcol, tk, nsplit):
    s1 = DFT1
    s2 = s // s1
    m1, tcs, tss, m2 = _dft_tables(s)
    twc = jnp.broadcast_to(tcs[:, :, None], (s1, s2, F_DIM)).reshape(s1, s2 * F_DIM)
    tws = jnp.broadcast_to(tss[:, :, None], (s1, s2, F_DIM)).reshape(s1, s2 * F_DIM)
    x3 = f.reshape(nseq, s1, s2 * F_DIM)
    br, bi = fft1_call(x3, m1, twc, tws, tcol=tcol)
    out = fft2_call(br.reshape(nseq, s1, s2, F_DIM), bi.reshape(nseq, s1, s2, F_DIM),
                    _channel_dft_matrix(), m2, tk=tk, nsplit=nsplit)
    return out.reshape(nseq * nsplit, s1, (s2 // nsplit) * F_DIM)


def _mixer_out(x_ref, mod_ref, of_ref, ob_ref, z_ref, fm_ref, gn_ref, w_ref):
    tm = x_ref.shape[0]
    o = of_ref[...].astype(F32) + ob_ref[...].astype(F32)
    z = z_ref[...].astype(F32)
    gn = gn_ref[...]
    parts = []
    fm = fm_ref[...]
    parts.append(jnp.concatenate(
        [fm[:, j * F_DIM:(j + 1) * F_DIM] for j in range(tm // DFT1)], axis=0).astype(BF16))
    for h in range(NH):
        oh = o[:, h * HD:(h + 1) * HD]
        ms = jnp.mean(oh * oh, axis=-1, keepdims=True)
        y = (oh * lax.rsqrt(ms + NORM_EPS)) * gn
        parts.append((y * _silu(z[:, h * HD:(h + 1) * HD])).astype(BF16))
    mixed = jnp.concatenate(parts, axis=1)
    proj = jnp.dot(mixed, w_ref[...], preferred_element_type=F32)
    return x_ref[...] + mod_ref[0, 2:3, :] * proj


def _swiglu_residual(x, mod_ref, nw_ref, wg_ref, wu_ref, wd_ref, nsplit):
    h = _mod_norm(x, nw_ref[...], mod_ref[0, 4:5, :], mod_ref[0, 3:4, :]).astype(BF16)
    cw = wg_ref.shape[1] // nsplit
    acc = None
    for c in range(nsplit):
        g = jnp.dot(h, wg_ref[:, c * cw:(c + 1) * cw], preferred_element_type=F32)
        u = jnp.dot(h, wu_ref[:, c * cw:(c + 1) * cw], preferred_element_type=F32)
        hid = (_silu(g) * u).astype(BF16)
        part = jnp.dot(hid, wd_ref[c * cw:(c + 1) * cw, :], preferred_element_type=F32)
        acc = part if acc is None else acc + part
    return x + mod_ref[0, 5:6, :] * acc


def _outproj_kernel(x_ref, mod_ref, of_ref, ob_ref, z_ref, fm_ref, gn_ref, w_ref, o_ref):
    o_ref[...] = _mixer_out(x_ref, mod_ref, of_ref, ob_ref, z_ref, fm_ref, gn_ref, w_ref)


def _outproj_ffn_kernel(x_ref, mod_ref, of_ref, ob_ref, z_ref, fm_ref, gn_ref, w_ref,
                        nw_ref, wg_ref, wu_ref, wd_ref, o_ref, *, nsplit):
    x_mid = _mixer_out(x_ref, mod_ref, of_ref, ob_ref, z_ref, fm_ref, gn_ref, w_ref)
    o_ref[...] = _swiglu_residual(x_mid, mod_ref, nw_ref, wg_ref, wu_ref, wd_ref, nsplit)


def outproj_call(x, mod, o_f, o_b, z, fm2, gn, w, ffn=None, *, seg, tm):
    t, d = x.shape
    row = lambda i: (i, 0)
    const = lambda i: (0, 0)
    per_seg = seg // tm
    single = pl.Buffered(1)
    in_specs = [
        pl.BlockSpec((tm, d), row),
        pl.BlockSpec((1, SUBLANES, d), lambda i: (i // per_seg, 0, 0)),
        pl.BlockSpec((tm, G_DIM), row),
        pl.BlockSpec((tm, G_DIM), row),
        pl.BlockSpec((tm, G_DIM), row),
        pl.BlockSpec((DFT1, (tm // DFT1) * F_DIM), lambda i: (i // per_seg, i % per_seg)),
        pl.BlockSpec((1, HD), const),
        pl.BlockSpec(w.shape, const, pipeline_mode=single),
    ]
    args = [x, mod, o_f, o_b, z, fm2, gn, w]
    body = _outproj_kernel
    if ffn is not None:
        nw, wg, wu, wd = ffn
        in_specs += [pl.BlockSpec((1, d), const)] + [
            pl.BlockSpec(a.shape, const, pipeline_mode=single) for a in (wg, wu, wd)]
        args += [nw, wg, wu, wd]
        body = functools.partial(_outproj_ffn_kernel, nsplit=2)
    return pl.pallas_call(
        body,
        grid=(t // tm,),
        in_specs=in_specs,
        out_specs=pl.BlockSpec((tm, d), row),
        out_shape=jax.ShapeDtypeStruct((t, d), F32),
        compiler_params=_cparams(("arbitrary",)),
        name="outproj" if ffn is None else "outproj_ffn",
    )(*args)


def _router_kernel(x_ref, mod_ref, nw_ref, wr_ref, tri_ref, oi_ref, op_ref, cnt_ref, carry_ref):
    i = pl.program_id(0)
    tm = x_ref.shape[0]

    @pl.when(i == 0)
    def _():
        carry_ref[...] = jnp.zeros_like(carry_ref)

    h = _mod_norm(x_ref[...], nw_ref[...], mod_ref[0, 4:5, :], mod_ref[0, 3:4, :])
    logits = jnp.dot(h.astype(BF16), wr_ref[...], preferred_element_type=F32)
    lane = lax.broadcasted_iota(jnp.int32, (tm, LANES), 1)
    logits = jnp.where(lane < N_EXPERTS, logits, NEG_BIG)
    l1 = jnp.max(logits, axis=-1, keepdims=True)
    i1 = jnp.min(jnp.where(logits == l1, lane, LANES), axis=-1, keepdims=True)
    rest = jnp.where(lane == i1, NEG_BIG, logits)
    l2 = jnp.max(rest, axis=-1, keepdims=True)
    i2 = jnp.min(jnp.where(rest == l2, lane, LANES), axis=-1, keepdims=True)
    e21 = jnp.exp(l2 - l1)
    p1 = 1.0 / (1.0 + e21)
    p2 = e21 * p1
    oh1 = lane == i1
    oh2 = lane == i2
    oh = jnp.where(oh1 | oh2, 1.0, 0.0).astype(BF16)
    before = jnp.dot(tri_ref[...], oh, preferred_element_type=F32) + carry_ref[0:1, :]
    r1 = jnp.sum(jnp.where(oh1, before, 0.0), axis=-1, keepdims=True).astype(jnp.int32)
    r2 = jnp.sum(jnp.where(oh2, before, 0.0), axis=-1, keepdims=True).astype(jnp.int32)
    oi_ref[...] = jnp.where(lane == 0, i1, jnp.where(lane == 1, i2, jnp.where(lane == 2, r1, r2)))
    op_ref[...] = jnp.where(lane == 0, p1, p2)
    new_carry = carry_ref[0:1, :] + jnp.sum(oh.astype(F32), axis=0, keepdims=True)
    carry_ref[...] = jnp.broadcast_to(new_carry, carry_ref.shape)
    cnt_ref[...] = jnp.broadcast_to(new_carry, cnt_ref.shape)


def router_call(x, mod, nw, wr, tri, *, seg, tm):
    t, d = x.shape
    row = lambda i: (i, 0)
    const = lambda i: (0, 0)
    return pl.pallas_call(
        _router_kernel,
        grid=(t // tm,),
        in_specs=[
            pl.BlockSpec((tm, d), row),
            pl.BlockSpec((1, SUBLANES, d), lambda i: (i * tm // seg, 0, 0)),
            pl.BlockSpec((1, d), const),
            pl.BlockSpec(wr.shape, const),
            pl.BlockSpec(tri.shape, const),
        ],
        out_specs=[pl.BlockSpec((tm, LANES), row), pl.BlockSpec((tm, LANES), row),
                   pl.BlockSpec((SUBLANES, LANES), const)],
        out_shape=[jax.ShapeDtypeStruct((t, LANES), jnp.int32), jax.ShapeDtypeStruct((t, LANES), F32),
                   jax.ShapeDtypeStruct((SUBLANES, LANES), F32)],
        scratch_shapes=[pltpu.VMEM((SUBLANES, LANES), F32)],
        compiler_params=_cparams(("arbitrary",)),
        name="router",
    )(x, mod, nw, wr, tri)


def _dispatch_kernel(pos_ref, gaps_ref, x_ref, mod_ref, nw_ref, xs_ref, hbuf, zrow, sem, zsem):
    i = pl.program_id(0)
    last = pl.num_programs(0) - 1
    tm = x_ref.shape[0]
    buf = i % 2

    @pl.when(i == 0)
    def _():
        zrow[...] = jnp.zeros_like(zrow)

        def zero_copy(row):
            return pltpu.make_async_copy(zrow.at[pl.ds(0, 1), :], xs_ref.at[pl.ds(row, 1), :], zsem)

        for g in range(gaps_ref.shape[2] // 2):
            lo = gaps_ref[0, 0, 2 * g]
            n = gaps_ref[0, 0, 2 * g + 1]
            lax.fori_loop(0, n, lambda r, c: (zero_copy(lo + r).start(), c)[1], 0)
        for g in range(gaps_ref.shape[2] // 2):
            lo = gaps_ref[0, 0, 2 * g]
            n = gaps_ref[0, 0, 2 * g + 1]
            lax.fori_loop(0, n, lambda r, c: (zero_copy(lo + r).wait(), c)[1], 0)

    hbuf[buf] = _mod_norm(x_ref[...], nw_ref[...], mod_ref[0, 4:5, :], mod_ref[0, 3:4, :])

    def start(r, c):
        for slot in range(2):
            pltpu.make_async_copy(hbuf.at[buf, pl.ds(r, 1), :],
                                  xs_ref.at[pl.ds(pos_ref[0, 0, slot * tm + r], 1), :],
                                  sem.at[buf]).start(priority=slot)
        return c

    lax.fori_loop(0, tm, start, 0, unroll=8)

    def wait_rows(b):
        for _ in range(2):
            pltpu.make_async_copy(hbuf.at[b], xs_ref.at[pl.ds(0, tm), :], sem.at[b]).wait()

    @pl.when(i > 0)
    def _():
        wait_rows(1 - buf)

    @pl.when(i == last)
    def _():
        wait_rows(buf)


def dispatch_call(pos3, gaps3, x, mod, nw, *, nr, seg, tm):
    t, d = x.shape
    return pl.pallas_call(
        _dispatch_kernel,
        grid=(t // tm,),
        in_specs=[
            pl.BlockSpec((1, 1, 2 * tm), lambda i: (i, 0, 0), memory_space=pltpu.SMEM),
            pl.BlockSpec(gaps3.shape, lambda i: (0, 0, 0), memory_space=pltpu.SMEM),
            pl.BlockSpec((tm, d), lambda i: (i, 0)),
            pl.BlockSpec((1, SUBLANES, d), lambda i: (i * tm // seg, 0, 0)),
            pl.BlockSpec((1, d), lambda i: (0, 0)),
        ],
        out_specs=pl.BlockSpec(memory_space=pl.ANY),
        out_shape=jax.ShapeDtypeStruct((nr, d), F32),
        scratch_shapes=[pltpu.VMEM((2, tm, d), F32), pltpu.VMEM((SUBLANES, d), F32),
                        pltpu.SemaphoreType.DMA((2,)), pltpu.SemaphoreType.DMA(())],
        compiler_params=_cparams(("arbitrary",)),
        name="moe_dispatch",
    )(pos3, gaps3, x, mod, nw)


def _expert_kernel(te_ref, nu_ref, xs_ref, wg_ref, wu_ref, wd_ref, y_ref, acc_ref):
    j = pl.program_id(0)
    half = pl.program_id(1)

    @pl.when(j < nu_ref[0])
    def _():
        h = xs_ref[...].astype(BF16)
        g = jnp.dot(h, wg_ref[0], preferred_element_type=F32)
        u = jnp.dot(h, wu_ref[0], preferred_element_type=F32)
        hid = (_silu(g) * u).astype(BF16)
        part = jnp.dot(hid, wd_ref[0], preferred_element_type=F32)

        @pl.when(half == 0)
        def _():
            acc_ref[...] = part

        @pl.when(half == 1)
        def _():
            y_ref[...] = acc_ref[...] + part

    @pl.when((j >= nu_ref[0]) & (half == 1))
    def _():
        y_ref[...] = jnp.zeros_like(y_ref)


def expert_call(tile_expert, n_used, xs, wg, wu, wd, *, tme):
    nr, d = xs.shape
    ntiles = nr // tme
    fh = wg.shape[2] // 2

    def jj(j, nu):
        return jnp.minimum(j, nu[0] - 1)

    def hh(j, hf, nu):
        return jnp.where(j < nu[0], hf, 1)

    grid_spec = pltpu.PrefetchScalarGridSpec(
        num_scalar_prefetch=2,
        grid=(ntiles, 2),
        in_specs=[
            pl.BlockSpec((tme, d), lambda j, hf, te, nu: (jj(j, nu), 0)),
            pl.BlockSpec((1, d, fh), lambda j, hf, te, nu: (te[jj(j, nu)], 0, hh(j, hf, nu))),
            pl.BlockSpec((1, d, fh), lambda j, hf, te, nu: (te[jj(j, nu)], 0, hh(j, hf, nu))),
            pl.BlockSpec((1, fh, d), lambda j, hf, te, nu: (te[jj(j, nu)], hh(j, hf, nu), 0)),
        ],
        out_specs=pl.BlockSpec((tme, d), lambda j, hf, te, nu: (j, 0)),
        scratch_shapes=[pltpu.VMEM((tme, d), F32)],
    )
    return pl.pallas_call(
        _expert_kernel,
        grid_spec=grid_spec,
        out_shape=jax.ShapeDtypeStruct((nr, d), F32),
        compiler_params=_cparams(("arbitrary", "arbitrary")),
        name="moe_experts",
    )(tile_expert, n_used, xs, wg, wu, wd)


def _combine_kernel(pos_ref, posn_ref, x_ref, mod_ref, p_ref, y_ref, o_ref, ybuf, sem):
    i = pl.program_id(0)
    last = pl.num_programs(0) - 1
    tm = x_ref.shape[0]
    buf = i % 2

    def gather(idx_ref, b):
        def start(r, c):
            for slot in range(2):
                pltpu.make_async_copy(y_ref.at[pl.ds(idx_ref[0, 0, slot * tm + r], 1), :],
                                      ybuf.at[b, slot, pl.ds(r, 1), :], sem.at[b]).start(priority=slot)
            return c

        lax.fori_loop(0, tm, start, 0, unroll=8)

    @pl.when(i == 0)
    def _():
        gather(pos_ref, buf)

    @pl.when(i < last)
    def _():
        gather(posn_ref, 1 - buf)

    for slot in range(2):
        pltpu.make_async_copy(y_ref.at[pl.ds(0, tm), :], ybuf.at[buf, slot], sem.at[buf]).wait()
    p = p_ref[...]
    f = p[:, 0:1] * ybuf[buf, 0] + p[:, 1:2] * ybuf[buf, 1]
    o_ref[...] = x_ref[...] + mod_ref[0, 5:6, :] * f


def combine_call(pos3, x, mod, p, y, *, seg, tm):
    t, d = x.shape
    nt = t // tm
    return pl.pallas_call(
        _combine_kernel,
        grid=(nt,),
        in_specs=[
            pl.BlockSpec((1, 1, 2 * tm), lambda i: (i, 0, 0), memory_space=pltpu.SMEM),
            pl.BlockSpec((1, 1, 2 * tm), lambda i: (jnp.minimum(i + 1, nt - 1), 0, 0),
                         memory_space=pltpu.SMEM),
            pl.BlockSpec((tm, d), lambda i: (i, 0)),
            pl.BlockSpec((1, SUBLANES, d), lambda i: (i * tm // seg, 0, 0)),
            pl.BlockSpec((tm, LANES), lambda i: (i, 0)),
            pl.BlockSpec(memory_space=pl.ANY),
        ],
        out_specs=pl.BlockSpec((tm, d), lambda i: (i, 0)),
        out_shape=jax.ShapeDtypeStruct((t, d), F32),
        scratch_shapes=[pltpu.VMEM((2, 2, tm, d), F32), pltpu.SemaphoreType.DMA((2,))],
        compiler_params=_cparams(("arbitrary",)),
        name="moe_combine",
    )(pos3, pos3, x, mod, p, y)


def moe_block(x, mod, nw, wr, wg, wu, wd, *, seg, tm, tmd, tme):
    t, d = x.shape
    wr_pad = jnp.zeros((d, LANES), BF16).at[:, :N_EXPERTS].set(wr.astype(BF16))
    tri = jnp.asarray(np.tril(np.ones((tm, tm), np.float32), k=-1), BF16)
    oi, op, cnt = router_call(x, mod, nw, wr_pad, tri, seg=seg, tm=tm)
    counts = cnt[0, :N_EXPERTS].astype(jnp.int32)
    padded = ((counts + tme - 1) // tme) * tme
    ends = jnp.cumsum(padded)
    starts = ends - padded
    nr = ((2 * t + N_EXPERTS * (tme - 1)) // tme) * tme
    ntiles = nr // tme
    n_used = (ends[-1] // tme).astype(jnp.int32).reshape(1)
    tile_start = jnp.arange(ntiles, dtype=jnp.int32) * tme
    tile_expert = jnp.minimum(jnp.sum(tile_start[:, None] >= ends[None, :], axis=1),
                              N_EXPERTS - 1).astype(jnp.int32)
    pos1 = starts[oi[:, 0]] + oi[:, 2]
    pos2 = starts[oi[:, 1]] + oi[:, 3]
    pos3 = jnp.concatenate([pos1.reshape(t // tmd, 1, tmd), pos2.reshape(t // tmd, 1, tmd)], axis=2)
    gap_lo = jnp.concatenate([starts + counts, ends[-1:]])
    gap_n = jnp.concatenate([padded - counts, nr - ends[-1:]])
    gaps3 = jnp.stack([gap_lo, gap_n], axis=1).reshape(1, 1, -1).astype(jnp.int32)
    xs = dispatch_call(pos3, gaps3, x, mod, nw, nr=nr, seg=seg, tm=tmd)
    y = expert_call(tile_expert, n_used, xs, wg, wu, wd, tme=tme)
    return combine_call(pos3, x, mod, op, y, seg=seg, tm=tmd)


def _final_kernel(x_ref, w_ref, o_ref):
    x = x_ref[...]
    ms = jnp.mean(x * x, axis=-1, keepdims=True)
    o_ref[...] = (x * lax.rsqrt(ms + NORM_EPS)) * w_ref[...]


def final_call(x, w, *, row0, nrows, tm):
    d = x.shape[1]
    off = row0 // tm
    return pl.pallas_call(
        _final_kernel,
        grid=(nrows // tm,),
        in_specs=[pl.BlockSpec((tm, d), lambda i: (i + off, 0)), pl.BlockSpec((1, d), lambda i: (0, 0))],
        out_specs=pl.BlockSpec((tm, d), lambda i: (i, 0)),
        out_shape=jax.ShapeDtypeStruct((nrows, d), F32),
        compiler_params=_cparams(("arbitrary",)),
        name="final_norm",
    )(x, w)


def _segment_flags(prompt_segs, sample_segs, seg, tile):
    per_seg = seg // tile
    nseg = prompt_segs + sample_segs
    first = np.zeros(nseg * per_seg, np.int32)
    last = np.zeros(nseg * per_seg, np.int32)
    first[0] = 1
    last[prompt_segs * per_seg - 1] = 1
    for s in range(prompt_segs, nseg):
        first[s * per_seg] = 1
        last[(s + 1) * per_seg - 1] = 1
    return first, last


def encoder_pair(x_prompt, x_sample, c_prompt, c_sample, w_ada, b_ada, norm_mix, norm_ffn, w_in, conv_w,
                 a_log, dt_bias, gdn_norm, w_out, w_ffn_gate, w_ffn_up, w_ffn_down, w_router,
                 w_exp_gate, w_exp_up, w_exp_down, norm_final, *, tm=512, tb=512, tmd=256, tme=512):
    bp, sp, d = x_prompt.shape
    bs, seg, _ = x_sample.shape
    assert bp == 1 and sp % seg == 0 and seg % tm == 0 and tm % DFT1 == 0
    depth = w_ada.shape[0]
    prompt_segs = sp // seg
    nseg = prompt_segs + bs
    t = nseg * seg
    x = jnp.concatenate([x_prompt.reshape(sp, d), x_sample.reshape(bs * seg, d)], axis=0)

    nrow = -(-(1 + bs) // SUBLANES) * SUBLANES
    c_all = jnp.zeros((nrow, d), F32).at[0:1].set(c_prompt).at[1:1 + bs].set(c_sample)
    ada = ada_call(c_all, w_ada, b_ada).reshape(depth, nrow, 6, d)
    seg_row = np.concatenate([np.zeros(prompt_segs, np.int32), 1 + np.arange(bs, dtype=np.int32)])
    mod_all = jnp.pad(ada[:, seg_row], ((0, 0), (0, 0), (0, SUBLANES - 6), (0, 0)))

    first_c, last_c = _segment_flags(prompt_segs, bs, seg, tm)
    first_g, last_g = _segment_flags(prompt_segs, bs, seg, tb)
    reset_f = jnp.asarray(first_g)
    reset_b = jnp.asarray(last_g[::-1].copy())
    first_c, last_c = jnp.asarray(first_c), jnp.asarray(last_c)

    off_f, off_qkv, off_z = F_DIM, F_DIM + QKV_DIM, F_DIM + QKV_DIM + G_DIM
    off_b = off_z + 2 * NH
    for l in range(depth):
        mod = mod_all[l]
        wl = w_in[l]
        wf = wl[:, :off_f].astype(BF16)
        wqkv = wl[:, off_f:off_qkv].astype(BF16)
        wz = wl[:, off_qkv:off_z].astype(BF16)
        wba = jnp.pad(wl[:, off_z:], ((0, 0), (0, LANES - 4 * NH))).astype(BF16)
        cw = jnp.pad(conv_w[l], ((0, SUBLANES - CONV_K), (0, 0)))
        gp = jnp.zeros((SUBLANES, LANES), F32)
        gp = gp.at[0, 2 * NH:4 * NH].set(dt_bias[l].reshape(-1)).at[1, 2 * NH:4 * NH].set(a_log[l].reshape(-1))
        f, z, q, k, v, gcol, grow, gend = inproj_call(first_c, last_c, x, mod, norm_mix[l].reshape(1, d),
                                                      wf, wqkv, wz, wba, cw, gp, seg=seg, tm=tm)
        o_f, o_b = gdn_call(reset_f, reset_b, q, k, v, gcol, grow, gend, tb=tb)

        s2p = sp // DFT1
        fm_p = fourier_mix_call(f[:sp], nseq=1, s=sp, tcol=min(2048, s2p * F_DIM),
                                tk=min(16, DFT1), nsplit=prompt_segs)
        s2s = seg // DFT1
        fm_s = fourier_mix_call(f[sp:], nseq=bs, s=seg, tcol=min(2048, s2s * F_DIM),
                                tk=min(64, DFT1), nsplit=1)
        fm2 = jnp.concatenate([fm_p, fm_s], axis=0).reshape(nseg * DFT1, (seg // DFT1) * F_DIM)

        nw = norm_ffn[l].reshape(1, d)
        i = l // 2
        ffn = None
        if l % 2 == 0:
            ffn = (nw, w_ffn_gate[i].astype(BF16), w_ffn_up[i].astype(BF16), w_ffn_down[i].astype(BF16))
        x = outproj_call(x, mod, o_f, o_b, z, fm2, gdn_norm[l].reshape(1, HD), w_out[l].astype(BF16), ffn,
                         seg=seg, tm=tm)
        if l % 2 == 1:
            x = moe_block(x, mod, nw, w_router[i], w_exp_gate[i].astype(BF16), w_exp_up[i].astype(BF16),
                          w_exp_down[i].astype(BF16), seg=seg, tm=tm, tmd=tmd, tme=tme)

    wn = norm_final.reshape(1, d)
    y_p = final_call(x, wn, row0=0, nrows=sp, tm=tm).reshape(bp, sp, d)
    y_s = final_call(x, wn, row0=sp, nrows=bs * seg, tm=tm).reshape(bs, seg, d)
    return y_p, y_s


def kernel(x_prompt, x_sample, c_prompt, c_sample, w_ada, b_ada, norm_mix, norm_ffn, w_in, conv_w, a_log,
           dt_bias, gdn_norm, w_out, w_ffn_gate, w_ffn_up, w_ffn_down, w_router, w_exp_gate, w_exp_up,
           w_exp_down, norm_final):
    return encoder_pair(x_prompt, x_sample, c_prompt, c_sample, w_ada, b_ada, norm_mix, norm_ffn, w_in,
                        conv_w, a_log, dt_bias, gdn_norm, w_out, w_ffn_gate, w_ffn_up, w_ffn_down,
                        w_router, w_exp_gate, w_exp_up, w_exp_down, norm_final)
```

```python
import functools
import math

import numpy as np
import jax
import jax.numpy as jnp
from jax import lax
from jax.experimental import pallas as pl
from jax.experimental.pallas import tpu as pltpu

F32 = jnp.float32
BF16 = jnp.bfloat16

D_MODEL = 1024
DEPTH = 4
N_GROUPS = 4
GROUP_DIM = 64
F_DIM = N_GROUPS * GROUP_DIM
HD = 128
NH = 6
G_DIM = NH * HD
QKV_DIM = 3 * G_DIM
CONV_K = 5
CHUNK = 64
PREP_CHUNKS = 2
D_FF = 2816
N_EXPERTS = 8
D_EXPERT = 3584
NORM_EPS = 1e-6

LANES = 128
SUBLANES = 8
MXU_COLS = 256
VMEM_LIMIT = 56 * 1024 * 1024
DFT1 = 128

GL_GAMMA, GL_BETA, GL_EG, GL_EGR = 0, 16, 32, 48
NEG_BIG = -1e30


def _cparams(sem):
    return pltpu.CompilerParams(dimension_semantics=sem, vmem_limit_bytes=VMEM_LIMIT)


def _mod_norm(x, nw, sc, sh):
    ms = jnp.mean(x * x, axis=-1, keepdims=True)
    y = x * lax.rsqrt(ms + NORM_EPS)
    return (y * nw) * (1.0 + sc) + sh


def _silu(x):
    hx = 0.5 * x
    return hx + hx * jnp.tanh(hx)


def _ada_kernel(c_ref, w_ref, b_ref, o_ref):
    c = _silu(c_ref[...])
    o_ref[0] = jnp.dot(c.astype(BF16), w_ref[0].astype(BF16), preferred_element_type=F32) + b_ref[0]


def ada_call(c_all, w_ada, b_ada):
    nrow = c_all.shape[0]
    depth, d, d6 = w_ada.shape
    tn = 1024
    return pl.pallas_call(
        _ada_kernel,
        grid=(depth, d6 // tn),
        in_specs=[
            pl.BlockSpec((nrow, d), lambda l, j: (0, 0)),
            pl.BlockSpec((1, d, tn), lambda l, j: (l, 0, j)),
            pl.BlockSpec((1, 1, tn), lambda l, j: (l, 0, j)),
        ],
        out_specs=pl.BlockSpec((1, nrow, tn), lambda l, j: (l, 0, j)),
        out_shape=jax.ShapeDtypeStruct((depth, nrow, d6), F32),
        compiler_params=_cparams(("arbitrary", "arbitrary")),
        name="ada",
    )(c_all, w_ada, b_ada.reshape(depth, 1, d6))


def _inproj_kernel(first_ref, last_ref,
                   x_ref, xp_ref, xn_ref, mod_ref, nw_ref, wf_ref, wqkv_ref, wz_ref, wba_ref, cw_ref, gp_ref,
                   f_ref, z_ref, q_ref, k_ref, v_ref, gcol_ref, grow_ref, gend_ref,
                   xe_ref):
    i = pl.program_id(0)
    tm = x_ref.shape[0]
    nc = tm // CHUNK
    halo = SUBLANES
    xcat = jnp.concatenate([xp_ref[...], x_ref[...], xn_ref[...]], axis=0)
    h = _mod_norm(xcat, nw_ref[...], mod_ref[0, 1:2, :], mod_ref[0, 0:1, :]).astype(BF16)
    pm = jnp.where(first_ref[i] == 1, 0.0, 1.0)
    nm = jnp.where(last_ref[i] == 1, 0.0, 1.0)
    outs = (q_ref, k_ref, v_ref)

    def side_work(cb):
        if cb == 0:
            ba = jnp.dot(h, wba_ref[...], preferred_element_type=F32)[halo:halo + tm]
            _gates(ba, gp_ref, gcol_ref, grow_ref, gend_ref)
        elif cb <= G_DIM // MXU_COLS:
            z0 = (cb - 1) * MXU_COLS
            z_ref[:, z0:z0 + MXU_COLS] = jnp.dot(h, wz_ref[:, z0:z0 + MXU_COLS], preferred_element_type=F32)[
                halo:halo + tm].astype(z_ref.dtype)
        elif cb == G_DIM // MXU_COLS + 1:
            f_ref[...] = jnp.dot(h, wf_ref[...], preferred_element_type=F32)[halo:halo + tm]

    for cb in range(QKV_DIM // MXU_COLS):
        c0 = cb * MXU_COLS
        res = jnp.dot(h, wqkv_ref[:, c0:c0 + MXU_COLS], preferred_element_type=F32)
        xe_ref[0:halo, c0:c0 + MXU_COLS] = res[0:halo] * pm
        xe_ref[halo:halo + tm, c0:c0 + MXU_COLS] = res[halo:halo + tm]
        xe_ref[halo + tm:, c0:c0 + MXU_COLS] = res[halo + tm:] * nm
        for s in range(cb * (MXU_COLS // HD), (cb + 1) * (MXU_COLS // HD)):
            lo = s * HD
            xs = xe_ref[:, lo:lo + HD]
            acc = None
            for j in range(CONV_K):
                shift = (CONV_K // 2 - j) % (tm + 2 * halo)
                xj = xs if shift == 0 else pltpu.roll(xs, shift, axis=0)
                term = xj[halo:halo + tm] * cw_ref[j:j + 1, lo:lo + HD]
                acc = term if acc is None else acc + term
            y = _silu(acc)
            which, head = divmod(s, NH)
            if which < 2:
                y = y * lax.rsqrt(jnp.sum(y * y, axis=-1, keepdims=True) + 1e-6)
            if which == 0:
                y = y * (HD ** -0.5)
            outs[which][:, head * HD:(head + 1) * HD] = y
        side_work(cb)


def _gates(ba, gp_ref, gcol_ref, grow_ref, gend_ref):
    tm = ba.shape[0]
    nc = tm // CHUNK
    lane = lax.broadcasted_iota(jnp.int32, (tm, LANES), 1)
    beta = 1.0 / (1.0 + jnp.exp(-ba))
    xs = ba + gp_ref[0:1, :]
    softplus = jnp.maximum(xs, 0.0) + jnp.log(1.0 + jnp.exp(-jnp.abs(xs)))
    g = -jnp.exp(gp_ref[1:2, :]) * softplus
    g = jnp.where((lane >= 2 * NH) & (lane < 4 * NH), g, 0.0)
    g = pltpu.roll(g, LANES - 2 * NH, axis=1)

    rowc = lax.broadcasted_iota(jnp.int32, (tm, LANES), 0) % CHUNK
    p = g
    sh = 1
    while sh < CHUNK:
        p = p + jnp.where(rowc >= sh, pltpu.roll(p, sh, axis=0), 0.0)
        sh *= 2
    g3 = g.reshape(nc, CHUNK, LANES)
    tot = jnp.broadcast_to(jnp.sum(g3, axis=1, keepdims=True), (nc, CHUNK, LANES)).reshape(tm, LANES)
    is_bwd = (lane >= NH) & (lane < 2 * NH)
    gamma = jnp.where(is_bwd, tot - p + g, p)
    eg = jnp.exp(gamma)
    egr = jnp.exp(tot - gamma)
    m12 = lane < 2 * NH
    gcol = (jnp.where(m12, gamma, 0.0)
            + pltpu.roll(jnp.where(m12, beta, 0.0), GL_BETA, axis=1)
            + pltpu.roll(jnp.where(m12, eg, 0.0), GL_EG, axis=1)
            + pltpu.roll(jnp.where(m12, egr, 0.0), GL_EGR, axis=1))
    gcol_ref[...] = gcol
    gam_t = jnp.where(m12, gamma, 0.0).T[0:2 * SUBLANES, :]
    g_t = g.T[0:2 * SUBLANES, :]
    for c in range(nc):
        grow_ref[c] = gam_t[:, c * CHUNK:(c + 1) * CHUNK]
        tc = jnp.sum(g_t[:, c * CHUNK:(c + 1) * CHUNK], axis=-1, keepdims=True)
        gend_ref[c] = jnp.exp(jnp.broadcast_to(tc, (2 * SUBLANES, LANES)))


def inproj_call(first, last, x, mod, nw, wf, wqkv, wz, wba, cw, gp, *, seg, tm):
    t, d = x.shape
    nblk8 = t // SUBLANES
    r8 = tm // SUBLANES
    nc = tm // CHUNK
    row = lambda i, f, l: (i, 0)
    const = lambda i, f, l: (0, 0)
    chunked = lambda i, f, l: (i, 0, 0)
    single = pl.Buffered(1)
    grid_spec = pltpu.PrefetchScalarGridSpec(
        num_scalar_prefetch=2,
        grid=(t // tm,),
        in_specs=[
            pl.BlockSpec((tm, d), row),
            pl.BlockSpec((SUBLANES, d), lambda i, f, l: (jnp.maximum(i * r8 - 1, 0), 0)),
            pl.BlockSpec((SUBLANES, d), lambda i, f, l: (jnp.minimum((i + 1) * r8, nblk8 - 1), 0)),
            pl.BlockSpec((1, SUBLANES, d), lambda i, f, l: (i * tm // seg, 0, 0)),
            pl.BlockSpec((1, d), const),
            pl.BlockSpec(wf.shape, const, pipeline_mode=single),
            pl.BlockSpec(wqkv.shape, const, pipeline_mode=single),
            pl.BlockSpec(wz.shape, const, pipeline_mode=single),
            pl.BlockSpec(wba.shape, const, pipeline_mode=single),
            pl.BlockSpec((SUBLANES, QKV_DIM), const),
            pl.BlockSpec((SUBLANES, LANES), const),
        ],
        out_specs=[
            pl.BlockSpec((tm, F_DIM), row),
            pl.BlockSpec((tm, G_DIM), row),
            pl.BlockSpec((tm, G_DIM), row),
            pl.BlockSpec((tm, G_DIM), row),
            pl.BlockSpec((tm, G_DIM), row),
            pl.BlockSpec((tm, LANES), row),
            pl.BlockSpec((nc, 2 * SUBLANES, CHUNK), chunked),
            pl.BlockSpec((nc, 2 * SUBLANES, LANES), chunked),
        ],
        scratch_shapes=[pltpu.VMEM((tm + 2 * SUBLANES, QKV_DIM), F32)],
    )
    return pl.pallas_call(
        _inproj_kernel,
        grid_spec=grid_spec,
        out_shape=[
            jax.ShapeDtypeStruct((t, F_DIM), F32),
            jax.ShapeDtypeStruct((t, G_DIM), BF16),
            jax.ShapeDtypeStruct((t, G_DIM), F32),
            jax.ShapeDtypeStruct((t, G_DIM), F32),
            jax.ShapeDtypeStruct((t, G_DIM), F32),
            jax.ShapeDtypeStruct((t, LANES), F32),
            jax.ShapeDtypeStruct((t // CHUNK, 2 * SUBLANES, CHUNK), F32),
            jax.ShapeDtypeStruct((t // CHUNK, 2 * SUBLANES, LANES), F32),
        ],
        compiler_params=_cparams(("arbitrary",)),
        name="inproj_conv",
    )(first, last, x, x, x, mod, nw, wf, wqkv, wz, wba, cw, gp)


def _gdn_kernel(rf_ref, rb_ref,
                qf_ref, kf_ref, vf_ref, gcf_ref, grf_ref, gef_ref,
                qb_ref, kb_ref, vb_ref, gcb_ref, grb_ref, geb_ref,
                of_ref, ob_ref, s_ref, wq_ref, ab_ref, u_ref):
    i = pl.program_id(0)
    tb = qf_ref.shape[0]
    nc = tb // CHUNK

    @pl.when(rf_ref[i] == 1)
    def _():
        s_ref[0:NH] = jnp.zeros((NH, HD, HD), F32)

    @pl.when(rb_ref[i] == 1)
    def _():
        s_ref[NH:2 * NH] = jnp.zeros((NH, HD, HD), F32)

    row = lax.broadcasted_iota(jnp.int32, (CHUNK, CHUNK), 0)
    col = lax.broadcasted_iota(jnp.int32, (CHUNK, CHUNK), 1)
    eye = jnp.where(row == col, 1.0, 0.0).astype(F32)
    level_masks = []
    b = 1
    while b < CHUNK:
        level_masks.append((row // (2 * b) == col // (2 * b)) & (row // b != col // b))
        b *= 2
    dirs = (
        (qf_ref, kf_ref, vf_ref, gcf_ref, grf_ref, gef_ref, of_ref, row >= col, row > col),
        (qb_ref, kb_ref, vb_ref, gcb_ref, grb_ref, geb_ref, ob_ref, row <= col, row < col),
    )

    hds = [(d, h) for d in range(2) for h in range(NH)]
    probs = [(u, d, h) for u in range(PREP_CHUNKS) for d, h in hds]
    npr = len(probs)

    def prep_step(it, carry):
        cs = [[it * PREP_CHUNKS + u for u in range(PREP_CHUNKS)],
              [nc - 1 - (it * PREP_CHUNKS + u) for u in range(PREP_CHUNKS)]]
        qs, ks, vs, cols, decs = [], [], [], [], []
        for u, d, h in probs:
            r0 = pl.multiple_of(cs[d][u] * CHUNK, CHUNK)
            q_ref, k_ref, v_ref, gc_ref, gr_ref = dirs[d][0:5]
            hd = d * NH + h
            lo = h * HD
            gc = gc_ref[pl.ds(r0, CHUNK), :]
            qs.append(q_ref[pl.ds(r0, CHUNK), lo:lo + HD])
            ks.append(k_ref[pl.ds(r0, CHUNK), lo:lo + HD])
            vs.append(v_ref[pl.ds(r0, CHUNK), lo:lo + HD])
            gam_c = gc[:, GL_GAMMA + hd:GL_GAMMA + hd + 1]
            cols.append((gc[:, GL_BETA + hd:GL_BETA + hd + 1], gc[:, GL_EG + hd:GL_EG + hd + 1],
                         gc[:, GL_EGR + hd:GL_EGR + hd + 1]))
            gam_r = gr_ref[cs[d][u]][hd:hd + 1, :]
            decs.append(jnp.exp(jnp.where(dirs[d][7], gam_c - gam_r, NEG_BIG)))
        kqs = [lax.dot_general(jnp.concatenate([ks[n], qs[n]], axis=0), ks[n],
                               (((1,), (1,)), ((), ())), preferred_element_type=F32)
               for n in range(npr)]
        a_s = [jnp.where(dirs[d][8], kqs[n][0:CHUNK] * cols[n][0] * decs[n], 0.0)
               for n, (u, d, h) in enumerate(probs)]
        for n, (u, d, h) in enumerate(probs):
            ab_ref[d, cs[d][u], h, 0:CHUNK, :] = kqs[n][CHUNK:] * decs[n]
            ab_ref[d, cs[d][u], h, CHUNK:, :] = (ks[n] * cols[n][2]).T
        ts = [eye - jnp.where(level_masks[0], a_s[n], 0.0) for n in range(npr)]
        for lm in level_masks[1:]:
            lts = [jnp.dot(jnp.where(lm, a_s[n], 0.0), ts[n], preferred_element_type=F32) for n in range(npr)]
            ts = [ts[n] - jnp.dot(ts[n], lts[n], preferred_element_type=F32) for n in range(npr)]
        uws =[jnp.dot(ts[n], jnp.concatenate([vs[n] * cols[n][0], ks[n] * (cols[n][0] * cols[n][1])], axis=1),
                       preferred_element_type=F32) for n in range(npr)]
        for n, (u, d, h) in enumerate(probs):
            u_ref[d, cs[d][u], h] = uws[n][:, 0:HD]
            wq_ref[d, cs[d][u], h, 0:CHUNK, :] = uws[n][:, HD:]
            wq_ref[d, cs[d][u], h, CHUNK:, :] = qs[n] * cols[n][1]
        return carry


    def scan_step(c, carry):
        ccs = (c, nc - 1 - c)
        sts = [s_ref[d * NH + h] for d, h in hds]
        wqs = [jnp.dot(wq_ref[d, ccs[d], h], sts[n], preferred_element_type=F32)
               for n, (d, h) in enumerate(hds)]
        vns = [u_ref[d, ccs[d], h] - wqs[n][0:CHUNK] for n, (d, h) in enumerate(hds)]
        avs = [jnp.dot(ab_ref[d, ccs[d], h], vns[n], preferred_element_type=F32)
               for n, (d, h) in enumerate(hds)]
        for n, (d, h) in enumerate(hds):
            r0 = pl.multiple_of(ccs[d] * CHUNK, CHUNK)
            dirs[d][6][pl.ds(r0, CHUNK), h * HD:(h + 1) * HD] = (
                wqs[n][CHUNK:] + avs[n][0:CHUNK]).astype(dirs[d][6].dtype)
            ge = dirs[d][5][ccs[d]]
            hd = d * NH + h
            s_ref[hd] = sts[n] * ge[hd:hd + 1, :] + avs[n][CHUNK:]
        return carry

    nit = nc // PREP_CHUNKS

    def scan_group(it):
        for u in range(PREP_CHUNKS):
            scan_step(it * PREP_CHUNKS + u, 0)

    def merged_step(it, carry):
        scan_group(it - 1)
        prep_step(it, 0)
        return carry

    prep_step(0, 0)
    lax.fori_loop(1, nit, merged_step, 0)
    scan_group(nit - 1)


def gdn_call(reset_f, reset_b, q, k, v, gcol, grow, gend, *, tb):
    t = q.shape[0]
    nb = t // tb
    nc = tb // CHUNK
    fwd2 = lambda i, a, b: (i, 0)
    bwd2 = lambda i, a, b: (nb - 1 - i, 0)
    fwd3 = lambda i, a, b: (i, 0, 0)
    bwd3 = lambda i, a, b: (nb - 1 - i, 0, 0)

    def specs(m2, m3):
        return [
            pl.BlockSpec((tb, G_DIM), m2), pl.BlockSpec((tb, G_DIM), m2), pl.BlockSpec((tb, G_DIM), m2),
            pl.BlockSpec((tb, LANES), m2),
            pl.BlockSpec((nc, 2 * SUBLANES, CHUNK), m3),
            pl.BlockSpec((nc, 2 * SUBLANES, LANES), m3),
        ]

    grid_spec = pltpu.PrefetchScalarGridSpec(
        num_scalar_prefetch=2,
        grid=(nb,),
        in_specs=specs(fwd2, fwd3) + specs(bwd2, bwd3),
        out_specs=[pl.BlockSpec((tb, G_DIM), fwd2), pl.BlockSpec((tb, G_DIM), bwd2)],
        scratch_shapes=[pltpu.VMEM((2 * NH, HD, HD), F32),
                        pltpu.VMEM((2, nc, NH, 2 * CHUNK, HD), F32),
                        pltpu.VMEM((2, nc, NH, CHUNK + HD, CHUNK), F32),
                        pltpu.VMEM((2, nc, NH, CHUNK, HD), F32)],
    )
    return pl.pallas_call(
        _gdn_kernel,
        grid_spec=grid_spec,
        out_shape=[jax.ShapeDtypeStruct((t, G_DIM), BF16), jax.ShapeDtypeStruct((t, G_DIM), BF16)],
        compiler_params=_cparams(("arbitrary",)),
        name="gdn",
    )(reset_f, reset_b, q, k, v, gcol, grow, gend, q, k, v, gcol, grow, gend)


def _fft1_kernel(x_ref, m1_ref, tc_ref, ts_ref, br_ref, bi_ref):
    s1 = x_ref.shape[1]
    a = jnp.dot(m1_ref[...], x_ref[0], preferred_element_type=F32)
    ar, ai = a[0:s1], a[s1:]
    tc, ts = tc_ref[...], ts_ref[...]
    br_ref[0] = ar * tc + ai * ts
    bi_ref[0] = ai * tc - ar * ts


def fft1_call(x3, m1, twc, tws, *, tcol):
    nseq, s1, cols = x3.shape
    blk = pl.BlockSpec((1, s1, tcol), lambda j, b: (b, 0, j))
    tw = pl.BlockSpec((s1, tcol), lambda j, b: (0, j))
    return pl.pallas_call(
        _fft1_kernel,
        grid=(cols // tcol, nseq),
        in_specs=[blk, pl.BlockSpec(m1.shape, lambda j, b: (0, 0)), tw, tw],
        out_specs=[blk, blk],
        out_shape=[jax.ShapeDtypeStruct(x3.shape, F32)] * 2,
        compiler_params=_cparams(("arbitrary", "arbitrary")),
        name="fft_stage1",
    )(x3, m1, twc, tws)


def _fft2_kernel(br_ref, bi_ref, mc_ref, m2_ref, o_ref):
    _, tk, s2, c = br_ref.shape
    nsplit = o_ref.shape[0]
    s2o = s2 // nsplit
    b = jnp.concatenate([br_ref[0].reshape(tk * s2, c), bi_ref[0].reshape(tk * s2, c)], axis=1)
    z = jnp.dot(b, mc_ref[...], preferred_element_type=F32)
    m2 = m2_ref[...]
    for kk in range(tk):
        zk = z[kk * s2:(kk + 1) * s2]
        x = jnp.dot(m2, jnp.concatenate([zk[:, 0:c], zk[:, c:]], axis=0),
                    preferred_element_type=F32)
        for sp in range(nsplit):
            o_ref[sp, kk] = x[sp * s2o:(sp + 1) * s2o]


def fft2_call(br4, bi4, mc, m2, *, tk, nsplit):
    nseq, s1, s2, c = br4.shape
    s2o = s2 // nsplit
    blk = pl.BlockSpec((1, tk, s2, c), lambda b, j: (b, j, 0, 0))
    return pl.pallas_call(
        _fft2_kernel,
        grid=(nseq, s1 // tk),
        in_specs=[blk, blk, pl.BlockSpec(mc.shape, lambda b, j: (0, 0)),
                  pl.BlockSpec(m2.shape, lambda b, j: (0, 0))],
        out_specs=pl.BlockSpec((nsplit, tk, s2o, c), lambda b, j: (b, j, 0, 0)),
        out_shape=jax.ShapeDtypeStruct((nseq * nsplit, s1, s2o, c), F32),
        compiler_params=_cparams(("arbitrary", "arbitrary")),
        name="fft_stage2",
    )(br4, bi4, mc, m2)


def _dft_tables(s):
    s1 = DFT1
    s2 = s // s1
    k = np.arange(s1)
    ang1 = 2.0 * np.pi * ((k[:, None] * k[None, :]) % s1) / s1
    sc = 1.0 / math.sqrt(s)
    m1 = np.concatenate([np.cos(ang1), -np.sin(ang1)], axis=0) * sc
    n2 = np.arange(s2)
    angt = 2.0 * np.pi * ((k[:, None] * n2[None, :]) % s) / s
    twc = np.repeat(np.cos(angt), F_DIM, axis=1)
    tws = np.repeat(np.sin(angt), F_DIM, axis=1)
    ang2 = 2.0 * np.pi * ((n2[:, None] * n2[None, :]) % s2) / s2
    m2 = np.concatenate([np.cos(ang2), np.sin(ang2)], axis=1)
    return (jnp.asarray(m1, F32), jnp.asarray(np.cos(angt), F32), jnp.asarray(np.sin(angt), F32),
            jnp.asarray(m2, F32))


def _channel_dft_matrix():
    g = np.arange(GROUP_DIM)
    ang = 2.0 * np.pi * ((g[:, None] * g[None, :]) % GROUP_DIM) / GROUP_DIM
    cg = np.kron(np.eye(N_GROUPS), np.cos(ang)) / math.sqrt(GROUP_DIM)
    sg = np.kron(np.eye(N_GROUPS), np.sin(ang)) / math.sqrt(GROUP_DIM)
    return jnp.asarray(np.block([[cg, -sg], [sg, cg]]), F32)


def fourier_mix_call(f, *, nseq, s, tcol, tk, nsplit):
    s1 = DFT1
    s2 = s // s1
    m1, tcs, tss, m2 = _dft_tables(s)
    twc = jnp.broadcast_to(tcs[:, :, None], (s1, s2, F_DIM)).reshape(s1, s2 * F_DIM)
    tws = jnp.broadcast_to(tss[:, :, None], (s1, s2, F_DIM)).reshape(s1, s2 * F_DIM)
    x3 = f.reshape(nseq, s1, s2 * F_DIM)
    br, bi = fft1_call(x3, m1, twc, tws, tcol=tcol)
    out = fft2_call(br.reshape(nseq, s1, s2, F_DIM), bi.reshape(nseq, s1, s2, F_DIM),
                    _channel_dft_matrix(), m2, tk=tk, nsplit=nsplit)
    return out.reshape(nseq * nsplit, s1, (s2 // nsplit) * F_DIM)


def _mixer_out(x_ref, mod_ref, of_ref, ob_ref, z_ref, fm_ref, gn_ref, w_ref):
    tm = x_ref.shape[0]
    o = of_ref[...].astype(F32) + ob_ref[...].astype(F32)
    z = z_ref[...].astype(F32)
    gn = gn_ref[...]
    parts = []
    fm = fm_ref[...]
    parts.append(jnp.concatenate(
        [fm[:, j * F_DIM:(j + 1) * F_DIM] for j in range(tm // DFT1)], axis=0).astype(BF16))
    for h in range(NH):
        oh = o[:, h * HD:(h + 1) * HD]
        ms = jnp.mean(oh * oh, axis=-1, keepdims=True)
        y = (oh * lax.rsqrt(ms + NORM_EPS)) * gn
        parts.append((y * _silu(z[:, h * HD:(h + 1) * HD])).astype(BF16))
    mixed = jnp.concatenate(parts, axis=1)
    proj = jnp.dot(mixed, w_ref[...], preferred_element_type=F32)
    return x_ref[...] + mod_ref[0, 2:3, :] * proj


def _swiglu_residual(x, mod_ref, nw_ref, wg_ref, wu_ref, wd_ref, nsplit):
    h = _mod_norm(x, nw_ref[...], mod_ref[0, 4:5, :], mod_ref[0, 3:4, :]).astype(BF16)
    cw = wg_ref.shape[1] // nsplit
    acc = None
    for c in range(nsplit):
        g = jnp.dot(h, wg_ref[:, c * cw:(c + 1) * cw], preferred_element_type=F32)
        u = jnp.dot(h, wu_ref[:, c * cw:(c + 1) * cw], preferred_element_type=F32)
        hid = (_silu(g) * u).astype(BF16)
        part = jnp.dot(hid, wd_ref[c * cw:(c + 1) * cw, :], preferred_element_type=F32)
        acc = part if acc is None else acc + part
    return x + mod_ref[0, 5:6, :] * acc


def _outproj_kernel(x_ref, mod_ref, of_ref, ob_ref, z_ref, fm_ref, gn_ref, w_ref, o_ref):
    o_ref[...] = _mixer_out(x_ref, mod_ref, of_ref, ob_ref, z_ref, fm_ref, gn_ref, w_ref)


def _outproj_ffn_kernel(x_ref, mod_ref, of_ref, ob_ref, z_ref, fm_ref, gn_ref, w_ref,
                        nw_ref, wg_ref, wu_ref, wd_ref, o_ref, *, nsplit):
    x_mid = _mixer_out(x_ref, mod_ref, of_ref, ob_ref, z_ref, fm_ref, gn_ref, w_ref)
    o_ref[...] = _swiglu_residual(x_mid, mod_ref, nw_ref, wg_ref, wu_ref, wd_ref, nsplit)


def outproj_call(x, mod, o_f, o_b, z, fm2, gn, w, ffn=None, *, seg, tm):
    t, d = x.shape
    row = lambda i: (i, 0)
    const = lambda i: (0, 0)
    per_seg = seg // tm
    single = pl.Buffered(1)
    in_specs = [
        pl.BlockSpec((tm, d), row),
        pl.BlockSpec((1, SUBLANES, d), lambda i: (i // per_seg, 0, 0)),
        pl.BlockSpec((tm, G_DIM), row),
        pl.BlockSpec((tm, G_DIM), row),
        pl.BlockSpec((tm, G_DIM), row),
        pl.BlockSpec((DFT1, (tm // DFT1) * F_DIM), lambda i: (i // per_seg, i % per_seg)),
        pl.BlockSpec((1, HD), const),
        pl.BlockSpec(w.shape, const, pipeline_mode=single),
    ]
    args = [x, mod, o_f, o_b, z, fm2, gn, w]
    body = _outproj_kernel
    if ffn is not None:
        nw, wg, wu, wd = ffn
        in_specs += [pl.BlockSpec((1, d), const)] + [
            pl.BlockSpec(a.shape, const, pipeline_mode=single) for a in (wg, wu, wd)]
        args += [nw, wg, wu, wd]
        body = functools.partial(_outproj_ffn_kernel, nsplit=wg.shape[1] // MXU_COLS)
    return pl.pallas_call(
        body,
        grid=(t // tm,),
        in_specs=in_specs,
        out_specs=pl.BlockSpec((tm, d), row),
        out_shape=jax.ShapeDtypeStruct((t, d), F32),
        compiler_params=_cparams(("arbitrary",)),
        name="outproj" if ffn is None else "outproj_ffn",
    )(*args)


def _router_kernel(x_ref, mod_ref, nw_ref, wr_ref, tri_ref, oi_ref, op_ref, cnt_ref, carry_ref):
    i = pl.program_id(0)
    tm = x_ref.shape[0]

    @pl.when(i == 0)
    def _():
        carry_ref[...] = jnp.zeros_like(carry_ref)

    h = _mod_norm(x_ref[...], nw_ref[...], mod_ref[0, 4:5, :], mod_ref[0, 3:4, :])
    logits = jnp.dot(h.astype(BF16), wr_ref[...], preferred_element_type=F32)
    lane = lax.broadcasted_iota(jnp.int32, (tm, LANES), 1)
    logits = jnp.where(lane < N_EXPERTS, logits, NEG_BIG)
    l1 = jnp.max(logits, axis=-1, keepdims=True)
    i1 = jnp.min(jnp.where(logits == l1, lane, LANES), axis=-1, keepdims=True)
    rest = jnp.where(lane == i1, NEG_BIG, logits)
    l2 = jnp.max(rest, axis=-1, keepdims=True)
    i2 = jnp.min(jnp.where(rest == l2, lane, LANES), axis=-1, keepdims=True)
    e21 = jnp.exp(l2 - l1)
    p1 = 1.0 / (1.0 + e21)
    p2 = e21 * p1
    oh1 = lane == i1
    oh2 = lane == i2
    oh = jnp.where(oh1 | oh2, 1.0, 0.0).astype(BF16)
    before = jnp.dot(tri_ref[...], oh, preferred_element_type=F32) + carry_ref[0:1, :]
    r1 = jnp.sum(jnp.where(oh1, before, 0.0), axis=-1, keepdims=True).astype(jnp.int32)
    r2 = jnp.sum(jnp.where(oh2, before, 0.0), axis=-1, keepdims=True).astype(jnp.int32)
    oi_ref[...] = jnp.where(lane == 0, i1, jnp.where(lane == 1, i2, jnp.where(lane == 2, r1, r2)))
    op_ref[...] = jnp.where(lane == 0, p1, p2)
    new_carry = carry_ref[0:1, :] + jnp.sum(oh.astype(F32), axis=0, keepdims=True)
    carry_ref[...] = jnp.broadcast_to(new_carry, carry_ref.shape)
    cnt_ref[...] = jnp.broadcast_to(new_carry, cnt_ref.shape)


def router_call(x, mod, nw, wr, tri, *, seg, tm):
    t, d = x.shape
    row = lambda i: (i, 0)
    const = lambda i: (0, 0)
    return pl.pallas_call(
        _router_kernel,
        grid=(t // tm,),
        in_specs=[
            pl.BlockSpec((tm, d), row),
            pl.BlockSpec((1, SUBLANES, d), lambda i: (i * tm // seg, 0, 0)),
            pl.BlockSpec((1, d), const),
            pl.BlockSpec(wr.shape, const),
            pl.BlockSpec(tri.shape, const),
        ],
        out_specs=[pl.BlockSpec((tm, LANES), row), pl.BlockSpec((tm, LANES), row),
                   pl.BlockSpec((SUBLANES, LANES), const)],
        out_shape=[jax.ShapeDtypeStruct((t, LANES), jnp.int32), jax.ShapeDtypeStruct((t, LANES), F32),
                   jax.ShapeDtypeStruct((SUBLANES, LANES), F32)],
        scratch_shapes=[pltpu.VMEM((SUBLANES, LANES), F32)],
        compiler_params=_cparams(("arbitrary",)),
        name="router",
    )(x, mod, nw, wr, tri)


def _dispatch_kernel(pos_ref, gaps_ref, x_ref, mod_ref, nw_ref, xs_ref, hbuf, zrow, sem, zsem):
    i = pl.program_id(0)
    last = pl.num_programs(0) - 1
    tm = x_ref.shape[0]
    buf = i % 2

    @pl.when(i == 0)
    def _():
        zrow[...] = jnp.zeros_like(zrow)

        def zero_copy(row):
            return pltpu.make_async_copy(zrow.at[pl.ds(0, 1), :], xs_ref.at[pl.ds(row, 1), :], zsem)

        for g in range(gaps_ref.shape[2] // 2):
            lo = gaps_ref[0, 0, 2 * g]
            n = gaps_ref[0, 0, 2 * g + 1]
            lax.fori_loop(0, n, lambda r, c: (zero_copy(lo + r).start(), c)[1], 0)
        for g in range(gaps_ref.shape[2] // 2):
            lo = gaps_ref[0, 0, 2 * g]
            n = gaps_ref[0, 0, 2 * g + 1]
            lax.fori_loop(0, n, lambda r, c: (zero_copy(lo + r).wait(), c)[1], 0)

    hbuf[buf] = _mod_norm(x_ref[...], nw_ref[...], mod_ref[0, 4:5, :], mod_ref[0, 3:4, :])

    def start(r, c):
        for slot in range(2):
            pltpu.make_async_copy(hbuf.at[buf, pl.ds(r, 1), :],
                                  xs_ref.at[pl.ds(pos_ref[0, 0, slot * tm + r], 1), :],
                                  sem.at[buf]).start(priority=slot)
        return c

    lax.fori_loop(0, tm, start, 0, unroll=8)

    def wait_rows(b):
        for _ in range(2):
            pltpu.make_async_copy(hbuf.at[b], xs_ref.at[pl.ds(0, tm), :], sem.at[b]).wait()

    @pl.when(i > 0)
    def _():
        wait_rows(1 - buf)

    @pl.when(i == last)
    def _():
        wait_rows(buf)


def dispatch_call(pos3, gaps3, x, mod, nw, *, nr, seg, tm):
    t, d = x.shape
    return pl.pallas_call(
        _dispatch_kernel,
        grid=(t // tm,),
        in_specs=[
            pl.BlockSpec((1, 1, 2 * tm), lambda i: (i, 0, 0), memory_space=pltpu.SMEM),
            pl.BlockSpec(gaps3.shape, lambda i: (0, 0, 0), memory_space=pltpu.SMEM),
            pl.BlockSpec((tm, d), lambda i: (i, 0)),
            pl.BlockSpec((1, SUBLANES, d), lambda i: (i * tm // seg, 0, 0)),
            pl.BlockSpec((1, d), lambda i: (0, 0)),
        ],
        out_specs=pl.BlockSpec(memory_space=pl.ANY),
        out_shape=jax.ShapeDtypeStruct((nr, d), F32),
        scratch_shapes=[pltpu.VMEM((2, tm, d), F32), pltpu.VMEM((SUBLANES, d), F32),
                        pltpu.SemaphoreType.DMA((2,)), pltpu.SemaphoreType.DMA(())],
        compiler_params=_cparams(("arbitrary",)),
        name="moe_dispatch",
    )(pos3, gaps3, x, mod, nw)


def _expert_kernel(te_ref, nu_ref, xs_ref, wg_ref, wu_ref, wd_ref, y_ref, acc_ref):
    j = pl.program_id(0)
    half = pl.program_id(1)

    @pl.when(j < nu_ref[0])
    def _():
        h = xs_ref[...].astype(BF16)
        g = jnp.dot(h, wg_ref[0], preferred_element_type=F32)
        u = jnp.dot(h, wu_ref[0], preferred_element_type=F32)
        hid = (_silu(g) * u).astype(BF16)
        part = jnp.dot(hid, wd_ref[0], preferred_element_type=F32)

        @pl.when(half == 0)
        def _():
            acc_ref[...] = part

        @pl.when(half == 1)
        def _():
            y_ref[...] = acc_ref[...] + part

    @pl.when((j >= nu_ref[0]) & (half == 1))
    def _():
        y_ref[...] = jnp.zeros_like(y_ref)


def expert_call(tile_expert, n_used, xs, wg, wu, wd, *, tme):
    nr, d = xs.shape
    ntiles = nr // tme
    fh = wg.shape[2] // 2

    def jj(j, nu):
        return jnp.minimum(j, nu[0] - 1)

    def hh(j, hf, nu):
        return jnp.where(j < nu[0], hf, 1)

    grid_spec = pltpu.PrefetchScalarGridSpec(
        num_scalar_prefetch=2,
        grid=(ntiles, 2),
        in_specs=[
            pl.BlockSpec((tme, d), lambda j, hf, te, nu: (jj(j, nu), 0)),
            pl.BlockSpec((1, d, fh), lambda j, hf, te, nu: (te[jj(j, nu)], 0, hh(j, hf, nu))),
            pl.BlockSpec((1, d, fh), lambda j, hf, te, nu: (te[jj(j, nu)], 0, hh(j, hf, nu))),
            pl.BlockSpec((1, fh, d), lambda j, hf, te, nu: (te[jj(j, nu)], hh(j, hf, nu), 0)),
        ],
        out_specs=pl.BlockSpec((tme, d), lambda j, hf, te, nu: (j, 0)),
        scratch_shapes=[pltpu.VMEM((tme, d), F32)],
    )
    return pl.pallas_call(
        _expert_kernel,
        grid_spec=grid_spec,
        out_shape=jax.ShapeDtypeStruct((nr, d), F32),
        compiler_params=_cparams(("arbitrary", "arbitrary")),
        name="moe_experts",
    )(tile_expert, n_used, xs, wg, wu, wd)


def _combine_kernel(pos_ref, posn_ref, x_ref, mod_ref, p_ref, y_ref, o_ref, ybuf, sem):
    i = pl.program_id(0)
    last = pl.num_programs(0) - 1
    tm = x_ref.shape[0]
    buf = i % 2

    def gather(idx_ref, b):
        def start(r, c):
            for slot in range(2):
                pltpu.make_async_copy(y_ref.at[pl.ds(idx_ref[0, 0, slot * tm + r], 1), :],
                                      ybuf.at[b, slot, pl.ds(r, 1), :], sem.at[b]).start(priority=slot)
            return c

        lax.fori_loop(0, tm, start, 0, unroll=8)

    @pl.when(i == 0)
    def _():
        gather(pos_ref, buf)

    @pl.when(i < last)
    def _():
        gather(posn_ref, 1 - buf)

    for slot in range(2):
        pltpu.make_async_copy(y_ref.at[pl.ds(0, tm), :], ybuf.at[buf, slot], sem.at[buf]).wait()
    p = p_ref[...]
    f = p[:, 0:1] * ybuf[buf, 0] + p[:, 1:2] * ybuf[buf, 1]
    o_ref[...] = x_ref[...] + mod_ref[0, 5:6, :] * f


def combine_call(pos3, x, mod, p, y, *, seg, tm):
    t, d = x.shape
    nt = t // tm
    return pl.pallas_call(
        _combine_kernel,
        grid=(nt,),
        in_specs=[
            pl.BlockSpec((1, 1, 2 * tm), lambda i: (i, 0, 0), memory_space=pltpu.SMEM),
            pl.BlockSpec((1, 1, 2 * tm), lambda i: (jnp.minimum(i + 1, nt - 1), 0, 0),
                         memory_space=pltpu.SMEM),
            pl.BlockSpec((tm, d), lambda i: (i, 0)),
            pl.BlockSpec((1, SUBLANES, d), lambda i: (i * tm // seg, 0, 0)),
            pl.BlockSpec((tm, LANES), lambda i: (i, 0)),
            pl.BlockSpec(memory_space=pl.ANY),
        ],
        out_specs=pl.BlockSpec((tm, d), lambda i: (i, 0)),
        out_shape=jax.ShapeDtypeStruct((t, d), F32),
        scratch_shapes=[pltpu.VMEM((2, 2, tm, d), F32), pltpu.SemaphoreType.DMA((2,))],
        compiler_params=_cparams(("arbitrary",)),
        name="moe_combine",
    )(pos3, pos3, x, mod, p, y)


def moe_block(x, mod, nw, wr, wg, wu, wd, *, seg, tm, tmd, tme):
    t, d = x.shape
    wr_pad = jnp.zeros((d, LANES), BF16).at[:, :N_EXPERTS].set(wr.astype(BF16))
    tri = jnp.asarray(np.tril(np.ones((tm, tm), np.float32), k=-1), BF16)
    oi, op, cnt = router_call(x, mod, nw, wr_pad, tri, seg=seg, tm=tm)
    counts = cnt[0, :N_EXPERTS].astype(jnp.int32)
    padded = ((counts + tme - 1) // tme) * tme
    ends = jnp.cumsum(padded)
    starts = ends - padded
    nr = ((2 * t + N_EXPERTS * (tme - 1)) // tme) * tme
    ntiles = nr // tme
    n_used = (ends[-1] // tme).astype(jnp.int32).reshape(1)
    tile_start = jnp.arange(ntiles, dtype=jnp.int32) * tme
    tile_expert = jnp.minimum(jnp.sum(tile_start[:, None] >= ends[None, :], axis=1),
                              N_EXPERTS - 1).astype(jnp.int32)
    pos1 = starts[oi[:, 0]] + oi[:, 2]
    pos2 = starts[oi[:, 1]] + oi[:, 3]
    pos3 = jnp.concatenate([pos1.reshape(t // tmd, 1, tmd), pos2.reshape(t // tmd, 1, tmd)], axis=2)
    gap_lo = jnp.concatenate([starts + counts, ends[-1:]])
    gap_n = jnp.concatenate([padded - counts, nr - ends[-1:]])
    gaps3 = jnp.stack([gap_lo, gap_n], axis=1).reshape(1, 1, -1).astype(jnp.int32)
    xs = dispatch_call(pos3, gaps3, x, mod, nw, nr=nr, seg=seg, tm=tmd)
    y = expert_call(tile_expert, n_used, xs, wg, wu, wd, tme=tme)
    return combine_call(pos3, x, mod, op, y, seg=seg, tm=tmd)


def _final_kernel(x_ref, w_ref, o_ref):
    x = x_ref[...]
    ms = jnp.mean(x * x, axis=-1, keepdims=True)
    o_ref[...] = (x * lax.rsqrt(ms + NORM_EPS)) * w_ref[...]


def final_call(x, w, *, row0, nrows, tm):
    d = x.shape[1]
    off = row0 // tm
    return pl.pallas_call(
        _final_kernel,
        grid=(nrows // tm,),
        in_specs=[pl.BlockSpec((tm, d), lambda i: (i + off, 0)), pl.BlockSpec((1, d), lambda i: (0, 0))],
        out_specs=pl.BlockSpec((tm, d), lambda i: (i, 0)),
        out_shape=jax.ShapeDtypeStruct((nrows, d), F32),
        compiler_params=_cparams(("arbitrary",)),
        name="final_norm",
    )(x, w)


def _segment_flags(prompt_segs, sample_segs, seg, tile):
    per_seg = seg // tile
    nseg = prompt_segs + sample_segs
    first = np.zeros(nseg * per_seg, np.int32)
    last = np.zeros(nseg * per_seg, np.int32)
    first[0] = 1
    last[prompt_segs * per_seg - 1] = 1
    for s in range(prompt_segs, nseg):
        first[s * per_seg] = 1
        last[(s + 1) * per_seg - 1] = 1
    return first, last


def encoder_pair(x_prompt, x_sample, c_prompt, c_sample, w_ada, b_ada, norm_mix, norm_ffn, w_in, conv_w,
                 a_log, dt_bias, gdn_norm, w_out, w_ffn_gate, w_ffn_up, w_ffn_down, w_router,
                 w_exp_gate, w_exp_up, w_exp_down, norm_final, *, tm=512, tb=512, tmd=256, tme=512):
    bp, sp, d = x_prompt.shape
    bs, seg, _ = x_sample.shape
    assert bp == 1 and sp % seg == 0 and seg % tm == 0 and tm % DFT1 == 0
    depth = w_ada.shape[0]
    prompt_segs = sp // seg
    nseg = prompt_segs + bs
    t = nseg * seg
    x = jnp.concatenate([x_prompt.reshape(sp, d), x_sample.reshape(bs * seg, d)], axis=0)

    nrow = -(-(1 + bs) // SUBLANES) * SUBLANES
    c_all = jnp.zeros((nrow, d), F32).at[0:1].set(c_prompt).at[1:1 + bs].set(c_sample)
    ada = ada_call(c_all, w_ada, b_ada).reshape(depth, nrow, 6, d)
    seg_row = np.concatenate([np.zeros(prompt_segs, np.int32), 1 + np.arange(bs, dtype=np.int32)])
    mod_all = jnp.pad(ada[:, seg_row], ((0, 0), (0, 0), (0, SUBLANES - 6), (0, 0)))

    first_c, last_c = _segment_flags(prompt_segs, bs, seg, tm)
    first_g, last_g = _segment_flags(prompt_segs, bs, seg, tb)
    reset_f = jnp.asarray(first_g)
    reset_b = jnp.asarray(last_g[::-1].copy())
    first_c, last_c = jnp.asarray(first_c), jnp.asarray(last_c)

    off_f, off_qkv, off_z = F_DIM, F_DIM + QKV_DIM, F_DIM + QKV_DIM + G_DIM
    off_b = off_z + 2 * NH
    for l in range(depth):
        mod = mod_all[l]
        wl = w_in[l]
        wf = wl[:, :off_f].astype(BF16)
        wqkv = wl[:, off_f:off_qkv].astype(BF16)
        wz = wl[:, off_qkv:off_z].astype(BF16)
        wba = jnp.pad(wl[:, off_z:], ((0, 0), (0, LANES - 4 * NH))).astype(BF16)
        cw = jnp.pad(conv_w[l], ((0, SUBLANES - CONV_K), (0, 0)))
        gp = jnp.zeros((SUBLANES, LANES), F32)
        gp = gp.at[0, 2 * NH:4 * NH].set(dt_bias[l].reshape(-1)).at[1, 2 * NH:4 * NH].set(a_log[l].reshape(-1))
        f, z, q, k, v, gcol, grow, gend = inproj_call(first_c, last_c, x, mod, norm_mix[l].reshape(1, d),
                                                      wf, wqkv, wz, wba, cw, gp, seg=seg, tm=tm)
        o_f, o_b = gdn_call(reset_f, reset_b, q, k, v, gcol, grow, gend, tb=tb)

        s2p = sp // DFT1
        fm_p = fourier_mix_call(f[:sp], nseq=1, s=sp, tcol=min(2048, s2p * F_DIM),
                                tk=min(16, DFT1), nsplit=prompt_segs)
        s2s = seg // DFT1
        fm_s = fourier_mix_call(f[sp:], nseq=bs, s=seg, tcol=min(2048, s2s * F_DIM),
                                tk=min(64, DFT1), nsplit=1)
        fm2 = jnp.concatenate([fm_p, fm_s], axis=0).reshape(nseg * DFT1, (seg // DFT1) * F_DIM)

        nw = norm_ffn[l].reshape(1, d)
        i = l // 2
        ffn = None
        if l % 2 == 0:
            ffn = (nw, w_ffn_gate[i].astype(BF16), w_ffn_up[i].astype(BF16), w_ffn_down[i].astype(BF16))
        x = outproj_call(x, mod, o_f, o_b, z, fm2, gdn_norm[l].reshape(1, HD), w_out[l].astype(BF16), ffn,
                         seg=seg, tm=tm)
        if l % 2 == 1:
            x = moe_block(x, mod, nw, w_router[i], w_exp_gate[i].astype(BF16), w_exp_up[i].astype(BF16),
                          w_exp_down[i].astype(BF16), seg=seg, tm=tm, tmd=tmd, tme=tme)

    wn = norm_final.reshape(1, d)
    y_p = final_call(x, wn, row0=0, nrows=sp, tm=tm).reshape(bp, sp, d)
    y_s = final_call(x, wn, row0=sp, nrows=bs * seg, tm=tm).reshape(bs, seg, d)
    return y_p, y_s


def kernel(x_prompt, x_sample, c_prompt, c_sample, w_ada, b_ada, norm_mix, norm_ffn, w_in, conv_w, a_log,
           dt_bias, gdn_norm, w_out, w_ffn_gate, w_ffn_up, w_ffn_down, w_router, w_exp_gate, w_exp_up,
           w_exp_down, norm_final):
    return encoder_pair(x_prompt, x_sample, c_prompt, c_sample, w_ada, b_ada, norm_mix, norm_ffn, w_in,
                        conv_w, a_log, dt_bias, gdn_norm, w_out, w_ffn_gate, w_ffn_up, w_ffn_down,
                        w_router, w_exp_gate, w_exp_up, w_exp_down, norm_final)
```

```python
import functools
import math

import numpy as np
import jax
import jax.numpy as jnp
from jax import lax
from jax.experimental import pallas as pl
from jax.experimental.pallas import tpu as pltpu

F32 = jnp.float32
BF16 = jnp.bfloat16

D_MODEL = 1024
DEPTH = 4
N_GROUPS = 4
GROUP_DIM = 64
F_DIM = N_GROUPS * GROUP_DIM
HD = 128
NH = 6
G_DIM = NH * HD
QKV_DIM = 3 * G_DIM
CONV_K = 5
CHUNK = 64
PREP_CHUNKS = 2
D_FF = 2816
N_EXPERTS = 8
D_EXPERT = 3584
NORM_EPS = 1e-6

LANES = 128
SUBLANES = 8
MXU_COLS = 256
VMEM_LIMIT = 56 * 1024 * 1024
DFT1 = 128

GL_GAMMA, GL_BETA, GL_EG, GL_EGR = 0, 16, 32, 48
NEG_BIG = -1e30


def _cparams(sem):
    return pltpu.CompilerParams(dimension_semantics=sem, vmem_limit_bytes=VMEM_LIMIT)


def _mod_norm(x, nw, sc, sh):
    ms = jnp.mean(x * x, axis=-1, keepdims=True)
    y = x * lax.rsqrt(ms + NORM_EPS)
    return (y * nw) * (1.0 + sc) + sh


def _silu(x):
    hx = 0.5 * x
    return hx + hx * jnp.tanh(hx)


def _ada_kernel(c_ref, w_ref, b_ref, o_ref):
    c = _silu(c_ref[...])
    o_ref[0] = jnp.dot(c.astype(BF16), w_ref[0].astype(BF16), preferred_element_type=F32) + b_ref[0]


def ada_call(c_all, w_ada, b_ada):
    nrow = c_all.shape[0]
    depth, d, d6 = w_ada.shape
    tn = 1024
    return pl.pallas_call(
        _ada_kernel,
        grid=(depth, d6 // tn),
        in_specs=[
            pl.BlockSpec((nrow, d), lambda l, j: (0, 0)),
            pl.BlockSpec((1, d, tn), lambda l, j: (l, 0, j)),
            pl.BlockSpec((1, 1, tn), lambda l, j: (l, 0, j)),
        ],
        out_specs=pl.BlockSpec((1, nrow, tn), lambda l, j: (l, 0, j)),
        out_shape=jax.ShapeDtypeStruct((depth, nrow, d6), F32),
        compiler_params=_cparams(("arbitrary", "arbitrary")),
        name="ada",
    )(c_all, w_ada, b_ada.reshape(depth, 1, d6))


def _inproj_kernel(first_ref, last_ref,
                   x_ref, xp_ref, xn_ref, mod_ref, nw_ref, wf_ref, wqkv_ref, wz_ref, wba_ref, cw_ref, gp_ref,
                   f_ref, z_ref, q_ref, k_ref, v_ref, gcol_ref, grow_ref, gend_ref,
                   xe_ref):
    i = pl.program_id(0)
    tm = x_ref.shape[0]
    nc = tm // CHUNK
    halo = SUBLANES
    xcat = jnp.concatenate([xp_ref[...], x_ref[...], xn_ref[...]], axis=0)
    h = _mod_norm(xcat, nw_ref[...], mod_ref[0, 1:2, :], mod_ref[0, 0:1, :]).astype(BF16)
    pm = jnp.where(first_ref[i] == 1, 0.0, 1.0)
    nm = jnp.where(last_ref[i] == 1, 0.0, 1.0)
    outs = (q_ref, k_ref, v_ref)

    def side_work(cb):
        if cb == 0:
            ba = jnp.dot(h, wba_ref[...], preferred_element_type=F32)[halo:halo + tm]
            _gates(ba, gp_ref, gcol_ref, grow_ref, gend_ref)
        elif cb <= G_DIM // MXU_COLS:
            z0 = (cb - 1) * MXU_COLS
            z_ref[:, z0:z0 + MXU_COLS] = jnp.dot(h, wz_ref[:, z0:z0 + MXU_COLS], preferred_element_type=F32)[
                halo:halo + tm].astype(z_ref.dtype)
        elif cb == G_DIM // MXU_COLS + 1:
            f_ref[...] = jnp.dot(h, wf_ref[...], preferred_element_type=F32)[halo:halo + tm].astype(f_ref.dtype)

    for cb in range(QKV_DIM // MXU_COLS):
        c0 = cb * MXU_COLS
        res = jnp.dot(h, wqkv_ref[:, c0:c0 + MXU_COLS], preferred_element_type=F32)
        xe_ref[0:halo, c0:c0 + MXU_COLS] = res[0:halo] * pm
        xe_ref[halo:halo + tm, c0:c0 + MXU_COLS] = res[halo:halo + tm]
        xe_ref[halo + tm:, c0:c0 + MXU_COLS] = res[halo + tm:] * nm
        for s in range(cb * (MXU_COLS // HD), (cb + 1) * (MXU_COLS // HD)):
            lo = s * HD
            xs = xe_ref[:, lo:lo + HD]
            acc = None
            for j in range(CONV_K):
                shift = (CONV_K // 2 - j) % (tm + 2 * halo)
                xj = xs if shift == 0 else pltpu.roll(xs, shift, axis=0)
                term = xj[halo:halo + tm] * cw_ref[j:j + 1, lo:lo + HD]
                acc = term if acc is None else acc + term
            y = _silu(acc)
            which, head = divmod(s, NH)
            if which < 2:
                y = y * lax.rsqrt(jnp.sum(y * y, axis=-1, keepdims=True) + 1e-6)
            if which == 0:
                y = y * (HD ** -0.5)
            outs[which][:, head * HD:(head + 1) * HD] = y
        side_work(cb)


def _gates(ba, gp_ref, gcol_ref, grow_ref, gend_ref):
    tm = ba.shape[0]
    nc = tm // CHUNK
    lane = lax.broadcasted_iota(jnp.int32, (tm, LANES), 1)
    beta = 1.0 / (1.0 + jnp.exp(-ba))
    xs = ba + gp_ref[0:1, :]
    softplus = jnp.maximum(xs, 0.0) + jnp.log(1.0 + jnp.exp(-jnp.abs(xs)))
    g = -jnp.exp(gp_ref[1:2, :]) * softplus
    g = jnp.where((lane >= 2 * NH) & (lane < 4 * NH), g, 0.0)
    g = pltpu.roll(g, LANES - 2 * NH, axis=1)

    rowc = lax.broadcasted_iota(jnp.int32, (tm, LANES), 0) % CHUNK
    p = g
    sh = 1
    while sh < CHUNK:
        p = p + jnp.where(rowc >= sh, pltpu.roll(p, sh, axis=0), 0.0)
        sh *= 2
    g3 = g.reshape(nc, CHUNK, LANES)
    tot = jnp.broadcast_to(jnp.sum(g3, axis=1, keepdims=True), (nc, CHUNK, LANES)).reshape(tm, LANES)
    is_bwd = (lane >= NH) & (lane < 2 * NH)
    gamma = jnp.where(is_bwd, tot - p + g, p)
    eg = jnp.exp(gamma)
    egr = jnp.exp(tot - gamma)
    m12 = lane < 2 * NH
    gcol = (jnp.where(m12, gamma, 0.0)
            + pltpu.roll(jnp.where(m12, beta, 0.0), GL_BETA, axis=1)
            + pltpu.roll(jnp.where(m12, eg, 0.0), GL_EG, axis=1)
            + pltpu.roll(jnp.where(m12, egr, 0.0), GL_EGR, axis=1))
    gcol_ref[...] = gcol
    gam_t = jnp.where(m12, gamma, 0.0).T[0:2 * SUBLANES, :]
    g_t = g.T[0:2 * SUBLANES, :]
    for c in range(nc):
        grow_ref[c] = gam_t[:, c * CHUNK:(c + 1) * CHUNK]
        tc = jnp.sum(g_t[:, c * CHUNK:(c + 1) * CHUNK], axis=-1, keepdims=True)
        gend_ref[c] = jnp.exp(jnp.broadcast_to(tc, (2 * SUBLANES, LANES)))


def inproj_call(first, last, x, mod, nw, wf, wqkv, wz, wba, cw, gp, *, seg, tm):
    t, d = x.shape
    nblk8 = t // SUBLANES
    r8 = tm // SUBLANES
    nc = tm // CHUNK
    row = lambda i, f, l: (i, 0)
    const = lambda i, f, l: (0, 0)
    chunked = lambda i, f, l: (i, 0, 0)
    single = pl.Buffered(1)
    grid_spec = pltpu.PrefetchScalarGridSpec(
        num_scalar_prefetch=2,
        grid=(t // tm,),
        in_specs=[
            pl.BlockSpec((tm, d), row),
            pl.BlockSpec((SUBLANES, d), lambda i, f, l: (jnp.maximum(i * r8 - 1, 0), 0)),
            pl.BlockSpec((SUBLANES, d), lambda i, f, l: (jnp.minimum((i + 1) * r8, nblk8 - 1), 0)),
            pl.BlockSpec((1, SUBLANES, d), lambda i, f, l: (i * tm // seg, 0, 0)),
            pl.BlockSpec((1, d), const),
            pl.BlockSpec(wf.shape, const, pipeline_mode=single),
            pl.BlockSpec(wqkv.shape, const, pipeline_mode=single),
            pl.BlockSpec(wz.shape, const, pipeline_mode=single),
            pl.BlockSpec(wba.shape, const, pipeline_mode=single),
            pl.BlockSpec((SUBLANES, QKV_DIM), const),
            pl.BlockSpec((SUBLANES, LANES), const),
        ],
        out_specs=[
            pl.BlockSpec((tm, F_DIM), row),
            pl.BlockSpec((tm, G_DIM), row),
            pl.BlockSpec((tm, G_DIM), row),
            pl.BlockSpec((tm, G_DIM), row),
            pl.BlockSpec((tm, G_DIM), row),
            pl.BlockSpec((tm, LANES), row),
            pl.BlockSpec((nc, 2 * SUBLANES, CHUNK), chunked),
            pl.BlockSpec((nc, 2 * SUBLANES, LANES), chunked),
        ],
        scratch_shapes=[pltpu.VMEM((tm + 2 * SUBLANES, QKV_DIM), F32)],
    )
    return pl.pallas_call(
        _inproj_kernel,
        grid_spec=grid_spec,
        out_shape=[
            jax.ShapeDtypeStruct((t, F_DIM), BF16),
            jax.ShapeDtypeStruct((t, G_DIM), BF16),
            jax.ShapeDtypeStruct((t, G_DIM), F32),
            jax.ShapeDtypeStruct((t, G_DIM), F32),
            jax.ShapeDtypeStruct((t, G_DIM), F32),
            jax.ShapeDtypeStruct((t, LANES), F32),
            jax.ShapeDtypeStruct((t // CHUNK, 2 * SUBLANES, CHUNK), F32),
            jax.ShapeDtypeStruct((t // CHUNK, 2 * SUBLANES, LANES), F32),
        ],
        compiler_params=_cparams(("arbitrary",)),
        name="inproj_conv",
    )(first, last, x, x, x, mod, nw, wf, wqkv, wz, wba, cw, gp)


def _gdn_kernel(rf_ref, rb_ref,
                qf_ref, kf_ref, vf_ref, gcf_ref, grf_ref, gef_ref,
                qb_ref, kb_ref, vb_ref, gcb_ref, grb_ref, geb_ref,
                of_ref, ob_ref, s_ref, wq_ref, ab_ref, u_ref):
    i = pl.program_id(0)
    tb = qf_ref.shape[0]
    nc = tb // CHUNK

    @pl.when(rf_ref[i] == 1)
    def _():
        s_ref[0:NH] = jnp.zeros((NH, HD, HD), F32)

    @pl.when(rb_ref[i] == 1)
    def _():
        s_ref[NH:2 * NH] = jnp.zeros((NH, HD, HD), F32)

    row = lax.broadcasted_iota(jnp.int32, (CHUNK, CHUNK), 0)
    col = lax.broadcasted_iota(jnp.int32, (CHUNK, CHUNK), 1)
    eye = jnp.where(row == col, 1.0, 0.0).astype(F32)
    level_masks = []
    b = 1
    while b < CHUNK:
        level_masks.append((row // (2 * b) == col // (2 * b)) & (row // b != col // b))
        b *= 2
    dirs = (
        (qf_ref, kf_ref, vf_ref, gcf_ref, grf_ref, gef_ref, of_ref, row >= col, row > col),
        (qb_ref, kb_ref, vb_ref, gcb_ref, grb_ref, geb_ref, ob_ref, row <= col, row < col),
    )

    hds = [(d, h) for d in range(2) for h in range(NH)]
    probs = [(u, d, h) for u in range(PREP_CHUNKS) for d, h in hds]
    npr = len(probs)

    def prep_step(it, carry):
        cs = [[it * PREP_CHUNKS + u for u in range(PREP_CHUNKS)],
              [nc - 1 - (it * PREP_CHUNKS + u) for u in range(PREP_CHUNKS)]]
        qs, ks, vs, cols, decs = [], [], [], [], []
        for u, d, h in probs:
            r0 = pl.multiple_of(cs[d][u] * CHUNK, CHUNK)
            q_ref, k_ref, v_ref, gc_ref, gr_ref = dirs[d][0:5]
            hd = d * NH + h
            lo = h * HD
            gc = gc_ref[pl.ds(r0, CHUNK), :]
            qs.append(q_ref[pl.ds(r0, CHUNK), lo:lo + HD])
            ks.append(k_ref[pl.ds(r0, CHUNK), lo:lo + HD])
            vs.append(v_ref[pl.ds(r0, CHUNK), lo:lo + HD])
            gam_c = gc[:, GL_GAMMA + hd:GL_GAMMA + hd + 1]
            cols.append((gc[:, GL_BETA + hd:GL_BETA + hd + 1], gc[:, GL_EG + hd:GL_EG + hd + 1],
                         gc[:, GL_EGR + hd:GL_EGR + hd + 1]))
            gam_r = gr_ref[cs[d][u]][hd:hd + 1, :]
            decs.append(jnp.exp(jnp.where(dirs[d][7], gam_c - gam_r, NEG_BIG)))
        kqs = [lax.dot_general(jnp.concatenate([ks[n], qs[n]], axis=0), ks[n],
                               (((1,), (1,)), ((), ())), preferred_element_type=F32)
               for n in range(npr)]
        a_s = [jnp.where(dirs[d][8], kqs[n][0:CHUNK] * cols[n][0] * decs[n], 0.0)
               for n, (u, d, h) in enumerate(probs)]
        for n, (u, d, h) in enumerate(probs):
            ab_ref[d, cs[d][u], h, 0:CHUNK, :] = kqs[n][CHUNK:] * decs[n]
            ab_ref[d, cs[d][u], h, CHUNK:, :] = (ks[n] * cols[n][2]).T
        ts = [eye - jnp.where(level_masks[0], a_s[n], 0.0) for n in range(npr)]
        for lm in level_masks[1:]:
            lts = [jnp.dot(jnp.where(lm, a_s[n], 0.0), ts[n], preferred_element_type=F32) for n in range(npr)]
            ts = [ts[n] - jnp.dot(ts[n], lts[n], preferred_element_type=F32) for n in range(npr)]
        uws =[jnp.dot(ts[n], jnp.concatenate([vs[n] * cols[n][0], ks[n] * (cols[n][0] * cols[n][1])], axis=1),
                       preferred_element_type=F32) for n in range(npr)]
        for n, (u, d, h) in enumerate(probs):
            u_ref[d, cs[d][u], h] = uws[n][:, 0:HD]
            wq_ref[d, cs[d][u], h, 0:CHUNK, :] = uws[n][:, HD:]
            wq_ref[d, cs[d][u], h, CHUNK:, :] = qs[n] * cols[n][1]
        return carry


    def scan_step(c, carry):
        ccs = (c, nc - 1 - c)
        sts = [s_ref[d * NH + h] for d, h in hds]
        wqs = [jnp.dot(wq_ref[d, ccs[d], h], sts[n], preferred_element_type=F32)
               for n, (d, h) in enumerate(hds)]
        vns = [u_ref[d, ccs[d], h] - wqs[n][0:CHUNK] for n, (d, h) in enumerate(hds)]
        avs = [jnp.dot(ab_ref[d, ccs[d], h], vns[n], preferred_element_type=F32)
               for n, (d, h) in enumerate(hds)]
        for n, (d, h) in enumerate(hds):
            r0 = pl.multiple_of(ccs[d] * CHUNK, CHUNK)
            dirs[d][6][pl.ds(r0, CHUNK), h * HD:(h + 1) * HD] = (
                wqs[n][CHUNK:] + avs[n][0:CHUNK]).astype(dirs[d][6].dtype)
            ge = dirs[d][5][ccs[d]]
            hd = d * NH + h
            s_ref[hd] = sts[n] * ge[hd:hd + 1, :] + avs[n][CHUNK:]
        return carry

    nit = nc // PREP_CHUNKS

    def scan_group(it):
        for u in range(PREP_CHUNKS):
            scan_step(it * PREP_CHUNKS + u, 0)

    def merged_step(it, carry):
        scan_group(it - 1)
        prep_step(it, 0)
        return carry

    prep_step(0, 0)
    lax.fori_loop(1, nit, merged_step, 0)
    scan_group(nit - 1)


def gdn_call(reset_f, reset_b, q, k, v, gcol, grow, gend, *, tb):
    t = q.shape[0]
    nb = t // tb
    nc = tb // CHUNK
    fwd2 = lambda i, a, b: (i, 0)
    bwd2 = lambda i, a, b: (nb - 1 - i, 0)
    fwd3 = lambda i, a, b: (i, 0, 0)
    bwd3 = lambda i, a, b: (nb - 1 - i, 0, 0)

    def specs(m2, m3):
        return [
            pl.BlockSpec((tb, G_DIM), m2), pl.BlockSpec((tb, G_DIM), m2), pl.BlockSpec((tb, G_DIM), m2),
            pl.BlockSpec((tb, LANES), m2),
            pl.BlockSpec((nc, 2 * SUBLANES, CHUNK), m3),
            pl.BlockSpec((nc, 2 * SUBLANES, LANES), m3),
        ]

    grid_spec = pltpu.PrefetchScalarGridSpec(
        num_scalar_prefetch=2,
        grid=(nb,),
        in_specs=specs(fwd2, fwd3) + specs(bwd2, bwd3),
        out_specs=[pl.BlockSpec((tb, G_DIM), fwd2), pl.BlockSpec((tb, G_DIM), bwd2)],
        scratch_shapes=[pltpu.VMEM((2 * NH, HD, HD), F32),
                        pltpu.VMEM((2, nc, NH, 2 * CHUNK, HD), F32),
                        pltpu.VMEM((2, nc, NH, CHUNK + HD, CHUNK), F32),
                        pltpu.VMEM((2, nc, NH, CHUNK, HD), F32)],
    )
    return pl.pallas_call(
        _gdn_kernel,
        grid_spec=grid_spec,
        out_shape=[jax.ShapeDtypeStruct((t, G_DIM), BF16), jax.ShapeDtypeStruct((t, G_DIM), BF16)],
        compiler_params=_cparams(("arbitrary",)),
        name="gdn",
    )(reset_f, reset_b, q, k, v, gcol, grow, gend, q, k, v, gcol, grow, gend)


def _fft1_kernel(x_ref, m1_ref, tc_ref, ts_ref, br_ref, bi_ref):
    s1 = x_ref.shape[1]
    a = jnp.dot(m1_ref[...], x_ref[0], preferred_element_type=F32)
    ar, ai = a[0:s1], a[s1:]
    tc, ts = tc_ref[...], ts_ref[...]
    br_ref[0] = (ar * tc + ai * ts).astype(br_ref.dtype)
    bi_ref[0] = (ai * tc - ar * ts).astype(bi_ref.dtype)


def fft1_call(x3, m1, twc, tws, *, tcol):
    nseq, s1, cols = x3.shape
    blk = pl.BlockSpec((1, s1, tcol), lambda j, b: (b, 0, j))
    tw = pl.BlockSpec((s1, tcol), lambda j, b: (0, j))
    return pl.pallas_call(
        _fft1_kernel,
        grid=(cols // tcol, nseq),
        in_specs=[blk, pl.BlockSpec(m1.shape, lambda j, b: (0, 0)), tw, tw],
        out_specs=[blk, blk],
        out_shape=[jax.ShapeDtypeStruct(x3.shape, BF16)] * 2,
        compiler_params=_cparams(("arbitrary", "arbitrary")),
        name="fft_stage1",
    )(x3, m1, twc, tws)


def _fft2_kernel(br_ref, bi_ref, mc_ref, m2_ref, o_ref):
    _, tk, s2, c = br_ref.shape
    nsplit = o_ref.shape[0]
    s2o = s2 // nsplit
    b = jnp.concatenate([br_ref[0].reshape(tk * s2, c), bi_ref[0].reshape(tk * s2, c)], axis=1)
    z = jnp.dot(b, mc_ref[...], preferred_element_type=F32)
    m2 = m2_ref[...]
    for kk in range(tk):
        zk = z[kk * s2:(kk + 1) * s2]
        x = jnp.dot(m2, jnp.concatenate([zk[:, 0:c], zk[:, c:]], axis=0),
                    preferred_element_type=F32)
        for sp in range(nsplit):
            o_ref[sp, kk] = x[sp * s2o:(sp + 1) * s2o].astype(o_ref.dtype)


def fft2_call(br4, bi4, mc, m2, *, tk, nsplit):
    nseq, s1, s2, c = br4.shape
    s2o = s2 // nsplit
    blk = pl.BlockSpec((1, tk, s2, c), lambda b, j: (b, j, 0, 0))
    return pl.pallas_call(
        _fft2_kernel,
        grid=(nseq, s1 // tk),
        in_specs=[blk, blk, pl.BlockSpec(mc.shape, lambda b, j: (0, 0)),
                  pl.BlockSpec(m2.shape, lambda b, j: (0, 0))],
        out_specs=pl.BlockSpec((nsplit, tk, s2o, c), lambda b, j: (b, j, 0, 0)),
        out_shape=jax.ShapeDtypeStruct((nseq * nsplit, s1, s2o, c), BF16),
        compiler_params=_cparams(("arbitrary", "arbitrary")),
        name="fft_stage2",
    )(br4, bi4, mc, m2)


def _dft_tables(s):
    s1 = DFT1
    s2 = s // s1
    k = np.arange(s1)
    ang1 = 2.0 * np.pi * ((k[:, None] * k[None, :]) % s1) / s1
    sc = 1.0 / math.sqrt(s)
    m1 = np.concatenate([np.cos(ang1), -np.sin(ang1)], axis=0) * sc
    n2 = np.arange(s2)
    angt = 2.0 * np.pi * ((k[:, None] * n2[None, :]) % s) / s
    twc = np.repeat(np.cos(angt), F_DIM, axis=1)
    tws = np.repeat(np.sin(angt), F_DIM, axis=1)
    ang2 = 2.0 * np.pi * ((n2[:, None] * n2[None, :]) % s2) / s2
    m2 = np.concatenate([np.cos(ang2), np.sin(ang2)], axis=1)
    return (jnp.asarray(m1, F32), jnp.asarray(np.cos(angt), F32), jnp.asarray(np.sin(angt), F32),
            jnp.asarray(m2, F32))


def _channel_dft_matrix():
    g = np.arange(GROUP_DIM)
    ang = 2.0 * np.pi * ((g[:, None] * g[None, :]) % GROUP_DIM) / GROUP_DIM
    cg = np.kron(np.eye(N_GROUPS), np.cos(ang)) / math.sqrt(GROUP_DIM)
    sg = np.kron(np.eye(N_GROUPS), np.sin(ang)) / math.sqrt(GROUP_DIM)
    return jnp.asarray(np.block([[cg, -sg], [sg, cg]]), F32)


def fourier_mix_call(f, *, nseq, s, tcol, tk, nsplit):
    s1 = DFT1
    s2 = s // s1
    m1, tcs, tss, m2 = _dft_tables(s)
    twc = jnp.broadcast_to(tcs[:, :, None], (s1, s2, F_DIM)).reshape(s1, s2 * F_DIM)
    tws = jnp.broadcast_to(tss[:, :, None], (s1, s2, F_DIM)).reshape(s1, s2 * F_DIM)
    x3 = f.reshape(nseq, s1, s2 * F_DIM)
    br, bi = fft1_call(x3, m1.astype(BF16), twc, tws, tcol=tcol)
    out = fft2_call(br.reshape(nseq, s1, s2, F_DIM), bi.reshape(nseq, s1, s2, F_DIM),
                    _channel_dft_matrix().astype(BF16), m2, tk=tk, nsplit=nsplit)
    return out.reshape(nseq * nsplit, s1, (s2 // nsplit) * F_DIM)


def _mixer_out(x_ref, mod_ref, of_ref, ob_ref, z_ref, fm_ref, gn_ref, w_ref):
    tm = x_ref.shape[0]
    o = of_ref[...].astype(F32) + ob_ref[...].astype(F32)
    z = z_ref[...].astype(F32)
    gn = gn_ref[...]
    parts = []
    fm = fm_ref[...]
    parts.append(jnp.concatenate(
        [fm[:, j * F_DIM:(j + 1) * F_DIM] for j in range(tm // DFT1)], axis=0).astype(BF16))
    for h in range(NH):
        oh = o[:, h * HD:(h + 1) * HD]
        ms = jnp.mean(oh * oh, axis=-1, keepdims=True)
        y = (oh * lax.rsqrt(ms + NORM_EPS)) * gn
        parts.append((y * _silu(z[:, h * HD:(h + 1) * HD])).astype(BF16))
    mixed = jnp.concatenate(parts, axis=1)
    proj = jnp.dot(mixed, w_ref[...], preferred_element_type=F32)
    return x_ref[...] + mod_ref[0, 2:3, :] * proj


def _swiglu_residual(x, mod_ref, nw_ref, wg_ref, wu_ref, wd_ref, nsplit):
    h = _mod_norm(x, nw_ref[...], mod_ref[0, 4:5, :], mod_ref[0, 3:4, :]).astype(BF16)
    cw = wg_ref.shape[1] // nsplit
    acc = None
    for c in range(nsplit):
        g = jnp.dot(h, wg_ref[:, c * cw:(c + 1) * cw], preferred_element_type=F32)
        u = jnp.dot(h, wu_ref[:, c * cw:(c + 1) * cw], preferred_element_type=F32)
        hid = (_silu(g) * u).astype(BF16)
        part = jnp.dot(hid, wd_ref[c * cw:(c + 1) * cw, :], preferred_element_type=F32)
        acc = part if acc is None else acc + part
    return x + mod_ref[0, 5:6, :] * acc


def _outproj_kernel(x_ref, mod_ref, of_ref, ob_ref, z_ref, fm_ref, gn_ref, w_ref, o_ref):
    o_ref[...] = _mixer_out(x_ref, mod_ref, of_ref, ob_ref, z_ref, fm_ref, gn_ref, w_ref)


def _outproj_ffn_kernel(x_ref, mod_ref, of_ref, ob_ref, z_ref, fm_ref, gn_ref, w_ref,
                        nw_ref, wg_ref, wu_ref, wd_ref, o_ref, *, nsplit):
    x_mid = _mixer_out(x_ref, mod_ref, of_ref, ob_ref, z_ref, fm_ref, gn_ref, w_ref)
    o_ref[...] = _swiglu_residual(x_mid, mod_ref, nw_ref, wg_ref, wu_ref, wd_ref, nsplit)


def outproj_call(x, mod, o_f, o_b, z, fm2, gn, w, ffn=None, *, seg, tm):
    t, d = x.shape
    row = lambda i: (i, 0)
    const = lambda i: (0, 0)
    per_seg = seg // tm
    single = pl.Buffered(1)
    in_specs = [
        pl.BlockSpec((tm, d), row),
        pl.BlockSpec((1, SUBLANES, d), lambda i: (i // per_seg, 0, 0)),
        pl.BlockSpec((tm, G_DIM), row),
        pl.BlockSpec((tm, G_DIM), row),
        pl.BlockSpec((tm, G_DIM), row),
        pl.BlockSpec((DFT1, (tm // DFT1) * F_DIM), lambda i: (i // per_seg, i % per_seg)),
        pl.BlockSpec((1, HD), const),
        pl.BlockSpec(w.shape, const, pipeline_mode=single),
    ]
    args = [x, mod, o_f, o_b, z, fm2, gn, w]
    body = _outproj_kernel
    if ffn is not None:
        nw, wg, wu, wd = ffn
        in_specs += [pl.BlockSpec((1, d), const)] + [
            pl.BlockSpec(a.shape, const, pipeline_mode=single) for a in (wg, wu, wd)]
        args += [nw, wg, wu, wd]
        body = functools.partial(_outproj_ffn_kernel, nsplit=wg.shape[1] // MXU_COLS)
    return pl.pallas_call(
        body,
        grid=(t // tm,),
        in_specs=in_specs,
        out_specs=pl.BlockSpec((tm, d), row),
        out_shape=jax.ShapeDtypeStruct((t, d), F32),
        compiler_params=_cparams(("arbitrary",)),
        name="outproj" if ffn is None else "outproj_ffn",
    )(*args)


def _router_kernel(x_ref, mod_ref, nw_ref, wr_ref, tri_ref, oi_ref, op_ref, cnt_ref, carry_ref):
    i = pl.program_id(0)
    tm = x_ref.shape[0]

    @pl.when(i == 0)
    def _():
        carry_ref[...] = jnp.zeros_like(carry_ref)

    h = _mod_norm(x_ref[...], nw_ref[...], mod_ref[0, 4:5, :], mod_ref[0, 3:4, :])
    logits = jnp.dot(h.astype(BF16), wr_ref[...], preferred_element_type=F32)
    lane = lax.broadcasted_iota(jnp.int32, (tm, LANES), 1)
    logits = jnp.where(lane < N_EXPERTS, logits, NEG_BIG)
    l1 = jnp.max(logits, axis=-1, keepdims=True)
    i1 = jnp.min(jnp.where(logits == l1, lane, LANES), axis=-1, keepdims=True)
    rest = jnp.where(lane == i1, NEG_BIG, logits)
    l2 = jnp.max(rest, axis=-1, keepdims=True)
    i2 = jnp.min(jnp.where(rest == l2, lane, LANES), axis=-1, keepdims=True)
    e21 = jnp.exp(l2 - l1)
    p1 = 1.0 / (1.0 + e21)
    p2 = e21 * p1
    oh1 = lane == i1
    oh2 = lane == i2
    oh = jnp.where(oh1 | oh2, 1.0, 0.0).astype(BF16)
    before = jnp.dot(tri_ref[...], oh, preferred_element_type=F32) + carry_ref[0:1, :]
    r1 = jnp.sum(jnp.where(oh1, before, 0.0), axis=-1, keepdims=True).astype(jnp.int32)
    r2 = jnp.sum(jnp.where(oh2, before, 0.0), axis=-1, keepdims=True).astype(jnp.int32)
    oi_ref[...] = jnp.where(lane == 0, i1, jnp.where(lane == 1, i2, jnp.where(lane == 2, r1, r2)))
    op_ref[...] = jnp.where(lane == 0, p1, p2)
    new_carry = carry_ref[0:1, :] + jnp.sum(oh.astype(F32), axis=0, keepdims=True)
    carry_ref[...] = jnp.broadcast_to(new_carry, carry_ref.shape)
    cnt_ref[...] = jnp.broadcast_to(new_carry, cnt_ref.shape)


def router_call(x, mod, nw, wr, tri, *, seg, tm):
    t, d = x.shape
    row = lambda i: (i, 0)
    const = lambda i: (0, 0)
    return pl.pallas_call(
        _router_kernel,
        grid=(t // tm,),
        in_specs=[
            pl.BlockSpec((tm, d), row),
            pl.BlockSpec((1, SUBLANES, d), lambda i: (i * tm // seg, 0, 0)),
            pl.BlockSpec((1, d), const),
            pl.BlockSpec(wr.shape, const),
            pl.BlockSpec(tri.shape, const),
        ],
        out_specs=[pl.BlockSpec((tm, LANES), row), pl.BlockSpec((tm, LANES), row),
                   pl.BlockSpec((SUBLANES, LANES), const)],
        out_shape=[jax.ShapeDtypeStruct((t, LANES), jnp.int32), jax.ShapeDtypeStruct((t, LANES), F32),
                   jax.ShapeDtypeStruct((SUBLANES, LANES), F32)],
        scratch_shapes=[pltpu.VMEM((SUBLANES, LANES), F32)],
        compiler_params=_cparams(("arbitrary",)),
        name="router",
    )(x, mod, nw, wr, tri)


def _dispatch_kernel(pos_ref, gaps_ref, x_ref, mod_ref, nw_ref, xs_ref, hbuf, zrow, sem, zsem):
    i = pl.program_id(0)
    last = pl.num_programs(0) - 1
    tm = x_ref.shape[0]
    buf = i % 2

    @pl.when(i == 0)
    def _():
        zrow[...] = jnp.zeros_like(zrow)

        def zero_copy(row):
            return pltpu.make_async_copy(zrow.at[pl.ds(0, 1), :], xs_ref.at[pl.ds(row, 1), :], zsem)

        for g in range(gaps_ref.shape[2] // 2):
            lo = gaps_ref[0, 0, 2 * g]
            n = gaps_ref[0, 0, 2 * g + 1]
            lax.fori_loop(0, n, lambda r, c: (zero_copy(lo + r).start(), c)[1], 0)
        for g in range(gaps_ref.shape[2] // 2):
            lo = gaps_ref[0, 0, 2 * g]
            n = gaps_ref[0, 0, 2 * g + 1]
            lax.fori_loop(0, n, lambda r, c: (zero_copy(lo + r).wait(), c)[1], 0)

    hbuf[buf] = _mod_norm(x_ref[...], nw_ref[...], mod_ref[0, 4:5, :], mod_ref[0, 3:4, :])

    def start(r, c):
        for slot in range(2):
            pltpu.make_async_copy(hbuf.at[buf, pl.ds(r, 1), :],
                                  xs_ref.at[pl.ds(pos_ref[0, 0, slot * tm + r], 1), :],
                                  sem.at[buf]).start(priority=slot)
        return c

    lax.fori_loop(0, tm, start, 0, unroll=8)

    def wait_rows(b):
        for _ in range(2):
            pltpu.make_async_copy(hbuf.at[b], xs_ref.at[pl.ds(0, tm), :], sem.at[b]).wait()

    @pl.when(i > 0)
    def _():
        wait_rows(1 - buf)

    @pl.when(i == last)
    def _():
        wait_rows(buf)


def dispatch_call(pos3, gaps3, x, mod, nw, *, nr, seg, tm):
    t, d = x.shape
    return pl.pallas_call(
        _dispatch_kernel,
        grid=(t // tm,),
        in_specs=[
            pl.BlockSpec((1, 1, 2 * tm), lambda i: (i, 0, 0), memory_space=pltpu.SMEM),
            pl.BlockSpec(gaps3.shape, lambda i: (0, 0, 0), memory_space=pltpu.SMEM),
            pl.BlockSpec((tm, d), lambda i: (i, 0)),
            pl.BlockSpec((1, SUBLANES, d), lambda i: (i * tm // seg, 0, 0)),
            pl.BlockSpec((1, d), lambda i: (0, 0)),
        ],
        out_specs=pl.BlockSpec(memory_space=pl.ANY),
        out_shape=jax.ShapeDtypeStruct((nr, d), F32),
        scratch_shapes=[pltpu.VMEM((2, tm, d), F32), pltpu.VMEM((SUBLANES, d), F32),
                        pltpu.SemaphoreType.DMA((2,)), pltpu.SemaphoreType.DMA(())],
        compiler_params=_cparams(("arbitrary",)),
        name="moe_dispatch",
    )(pos3, gaps3, x, mod, nw)


def _expert_kernel(te_ref, nu_ref, xs_ref, wg_ref, wu_ref, wd_ref, y_ref, acc_ref):
    j = pl.program_id(0)
    half = pl.program_id(1)

    @pl.when(j < nu_ref[0])
    def _():
        h = xs_ref[...].astype(BF16)
        g = jnp.dot(h, wg_ref[0], preferred_element_type=F32)
        u = jnp.dot(h, wu_ref[0], preferred_element_type=F32)
        hid = (_silu(g) * u).astype(BF16)
        part = jnp.dot(hid, wd_ref[0], preferred_element_type=F32)

        @pl.when(half == 0)
        def _():
            acc_ref[...] = part

        @pl.when(half == 1)
        def _():
            y_ref[...] = acc_ref[...] + part

    @pl.when((j >= nu_ref[0]) & (half == 1))
    def _():
        y_ref[...] = jnp.zeros_like(y_ref)


def expert_call(tile_expert, n_used, xs, wg, wu, wd, *, tme):
    nr, d = xs.shape
    ntiles = nr // tme
    fh = wg.shape[2] // 2

    def jj(j, nu):
        return jnp.minimum(j, nu[0] - 1)

    def hh(j, hf, nu):
        return jnp.where(j < nu[0], hf, 1)

    grid_spec = pltpu.PrefetchScalarGridSpec(
        num_scalar_prefetch=2,
        grid=(ntiles, 2),
        in_specs=[
            pl.BlockSpec((tme, d), lambda j, hf, te, nu: (jj(j, nu), 0)),
            pl.BlockSpec((1, d, fh), lambda j, hf, te, nu: (te[jj(j, nu)], 0, hh(j, hf, nu))),
            pl.BlockSpec((1, d, fh), lambda j, hf, te, nu: (te[jj(j, nu)], 0, hh(j, hf, nu))),
            pl.BlockSpec((1, fh, d), lambda j, hf, te, nu: (te[jj(j, nu)], hh(j, hf, nu), 0)),
        ],
        out_specs=pl.BlockSpec((tme, d), lambda j, hf, te, nu: (j, 0)),
        scratch_shapes=[pltpu.VMEM((tme, d), F32)],
    )
    return pl.pallas_call(
        _expert_kernel,
        grid_spec=grid_spec,
        out_shape=jax.ShapeDtypeStruct((nr, d), F32),
        compiler_params=_cparams(("arbitrary", "arbitrary")),
        name="moe_experts",
    )(tile_expert, n_used, xs, wg, wu, wd)


def _combine_kernel(pos_ref, posn_ref, x_ref, mod_ref, p_ref, y_ref, o_ref, ybuf, sem):
    i = pl.program_id(0)
    last = pl.num_programs(0) - 1
    tm = x_ref.shape[0]
    buf = i % 2

    def gather(idx_ref, b):
        def start(r, c):
            for slot in range(2):
                pltpu.make_async_copy(y_ref.at[pl.ds(idx_ref[0, 0, slot * tm + r], 1), :],
                                      ybuf.at[b, slot, pl.ds(r, 1), :], sem.at[b]).start(priority=slot)
            return c

        lax.fori_loop(0, tm, start, 0, unroll=8)

    @pl.when(i == 0)
    def _():
        gather(pos_ref, buf)

    @pl.when(i < last)
    def _():
        gather(posn_ref, 1 - buf)

    for slot in range(2):
        pltpu.make_async_copy(y_ref.at[pl.ds(0, tm), :], ybuf.at[buf, slot], sem.at[buf]).wait()
    p = p_ref[...]
    f = p[:, 0:1] * ybuf[buf, 0] + p[:, 1:2] * ybuf[buf, 1]
    o_ref[...] = x_ref[...] + mod_ref[0, 5:6, :] * f


def combine_call(pos3, x, mod, p, y, *, seg, tm):
    t, d = x.shape
    nt = t // tm
    return pl.pallas_call(
        _combine_kernel,
        grid=(nt,),
        in_specs=[
            pl.BlockSpec((1, 1, 2 * tm), lambda i: (i, 0, 0), memory_space=pltpu.SMEM),
            pl.BlockSpec((1, 1, 2 * tm), lambda i: (jnp.minimum(i + 1, nt - 1), 0, 0),
                         memory_space=pltpu.SMEM),
            pl.BlockSpec((tm, d), lambda i: (i, 0)),
            pl.BlockSpec((1, SUBLANES, d), lambda i: (i * tm // seg, 0, 0)),
            pl.BlockSpec((tm, LANES), lambda i: (i, 0)),
            pl.BlockSpec(memory_space=pl.ANY),
        ],
        out_specs=pl.BlockSpec((tm, d), lambda i: (i, 0)),
        out_shape=jax.ShapeDtypeStruct((t, d), F32),
        scratch_shapes=[pltpu.VMEM((2, 2, tm, d), F32), pltpu.SemaphoreType.DMA((2,))],
        compiler_params=_cparams(("arbitrary",)),
        name="moe_combine",
    )(pos3, pos3, x, mod, p, y)


def moe_block(x, mod, nw, wr, wg, wu, wd, *, seg, tm, tmd, tme):
    t, d = x.shape
    wr_pad = jnp.zeros((d, LANES), BF16).at[:, :N_EXPERTS].set(wr.astype(BF16))
    tri = jnp.asarray(np.tril(np.ones((tm, tm), np.float32), k=-1), BF16)
    oi, op, cnt = router_call(x, mod, nw, wr_pad, tri, seg=seg, tm=tm)
    counts = cnt[0, :N_EXPERTS].astype(jnp.int32)
    padded = ((counts + tme - 1) // tme) * tme
    ends = jnp.cumsum(padded)
    starts = ends - padded
    nr = ((2 * t + N_EXPERTS * (tme - 1)) // tme) * tme
    ntiles = nr // tme
    n_used = (ends[-1] // tme).astype(jnp.int32).reshape(1)
    tile_start = jnp.arange(ntiles, dtype=jnp.int32) * tme
    tile_expert = jnp.minimum(jnp.sum(tile_start[:, None] >= ends[None, :], axis=1),
                              N_EXPERTS - 1).astype(jnp.int32)
    pos1 = starts[oi[:, 0]] + oi[:, 2]
    pos2 = starts[oi[:, 1]] + oi[:, 3]
    pos3 = jnp.concatenate([pos1.reshape(t // tmd, 1, tmd), pos2.reshape(t // tmd, 1, tmd)], axis=2)
    gap_lo = jnp.concatenate([starts + counts, ends[-1:]])
    gap_n = jnp.concatenate([padded - counts, nr - ends[-1:]])
    gaps3 = jnp.stack([gap_lo, gap_n], axis=1).reshape(1, 1, -1).astype(jnp.int32)
    xs = dispatch_call(pos3, gaps3, x, mod, nw, nr=nr, seg=seg, tm=tmd)
    y = expert_call(tile_expert, n_used, xs, wg, wu, wd, tme=tme)
    return combine_call(pos3, x, mod, op, y, seg=seg, tm=tmd)


def _final_kernel(x_ref, w_ref, o_ref):
    x = x_ref[...]
    ms = jnp.mean(x * x, axis=-1, keepdims=True)
    o_ref[...] = (x * lax.rsqrt(ms + NORM_EPS)) * w_ref[...]


def final_call(x, w, *, row0, nrows, tm):
    d = x.shape[1]
    off = row0 // tm
    return pl.pallas_call(
        _final_kernel,
        grid=(nrows // tm,),
        in_specs=[pl.BlockSpec((tm, d), lambda i: (i + off, 0)), pl.BlockSpec((1, d), lambda i: (0, 0))],
        out_specs=pl.BlockSpec((tm, d), lambda i: (i, 0)),
        out_shape=jax.ShapeDtypeStruct((nrows, d), F32),
        compiler_params=_cparams(("arbitrary",)),
        name="final_norm",
    )(x, w)


def _segment_flags(prompt_segs, sample_segs, seg, tile):
    per_seg = seg // tile
    nseg = prompt_segs + sample_segs
    first = np.zeros(nseg * per_seg, np.int32)
    last = np.zeros(nseg * per_seg, np.int32)
    first[0] = 1
    last[prompt_segs * per_seg - 1] = 1
    for s in range(prompt_segs, nseg):
        first[s * per_seg] = 1
        last[(s + 1) * per_seg - 1] = 1
    return first, last


def encoder_pair(x_prompt, x_sample, c_prompt, c_sample, w_ada, b_ada, norm_mix, norm_ffn, w_in, conv_w,
                 a_log, dt_bias, gdn_norm, w_out, w_ffn_gate, w_ffn_up, w_ffn_down, w_router,
                 w_exp_gate, w_exp_up, w_exp_down, norm_final, *, tm=512, tb=512, tmd=256, tme=512):
    bp, sp, d = x_prompt.shape
    bs, seg, _ = x_sample.shape
    assert bp == 1 and sp % seg == 0 and seg % tm == 0 and tm % DFT1 == 0
    depth = w_ada.shape[0]
    prompt_segs = sp // seg
    nseg = prompt_segs + bs
    t = nseg * seg
    x = jnp.concatenate([x_prompt.reshape(sp, d), x_sample.reshape(bs * seg, d)], axis=0)

    nrow = -(-(1 + bs) // SUBLANES) * SUBLANES
    c_all = jnp.zeros((nrow, d), F32).at[0:1].set(c_prompt).at[1:1 + bs].set(c_sample)
    ada = ada_call(c_all, w_ada, b_ada).reshape(depth, nrow, 6, d)
    seg_row = np.concatenate([np.zeros(prompt_segs, np.int32), 1 + np.arange(bs, dtype=np.int32)])
    mod_all = jnp.pad(ada[:, seg_row], ((0, 0), (0, 0), (0, SUBLANES - 6), (0, 0)))

    first_c, last_c = _segment_flags(prompt_segs, bs, seg, tm)
    first_g, last_g = _segment_flags(prompt_segs, bs, seg, tb)
    reset_f = jnp.asarray(first_g)
    reset_b = jnp.asarray(last_g[::-1].copy())
    first_c, last_c = jnp.asarray(first_c), jnp.asarray(last_c)

    off_f, off_qkv, off_z = F_DIM, F_DIM + QKV_DIM, F_DIM + QKV_DIM + G_DIM
    off_b = off_z + 2 * NH
    for l in range(depth):
        mod = mod_all[l]
        wl = w_in[l]
        wf = wl[:, :off_f].astype(BF16)
        wqkv = wl[:, off_f:off_qkv].astype(BF16)
        wz = wl[:, off_qkv:off_z].astype(BF16)
        wba = jnp.pad(wl[:, off_z:], ((0, 0), (0, LANES - 4 * NH))).astype(BF16)
        cw = jnp.pad(conv_w[l], ((0, SUBLANES - CONV_K), (0, 0)))
        gp = jnp.zeros((SUBLANES, LANES), F32)
        gp = gp.at[0, 2 * NH:4 * NH].set(dt_bias[l].reshape(-1)).at[1, 2 * NH:4 * NH].set(a_log[l].reshape(-1))
        f, z, q, k, v, gcol, grow, gend = inproj_call(first_c, last_c, x, mod, norm_mix[l].reshape(1, d),
                                                      wf, wqkv, wz, wba, cw, gp, seg=seg, tm=tm)
        o_f, o_b = gdn_call(reset_f, reset_b, q, k, v, gcol, grow, gend, tb=tb)

        s2p = sp // DFT1
        fm_p = fourier_mix_call(f[:sp], nseq=1, s=sp, tcol=min(2048, s2p * F_DIM),
                                tk=min(16, DFT1), nsplit=prompt_segs)
        s2s = seg // DFT1
        fm_s = fourier_mix_call(f[sp:], nseq=bs, s=seg, tcol=min(2048, s2s * F_DIM),
                                tk=min(64, DFT1), nsplit=1)
        fm2 = jnp.concatenate([fm_p, fm_s], axis=0).reshape(nseg * DFT1, (seg // DFT1) * F_DIM)

        nw = norm_ffn[l].reshape(1, d)
        i = l // 2
        ffn = None
        if l % 2 == 0:
            ffn = (nw, w_ffn_gate[i].astype(BF16), w_ffn_up[i].astype(BF16), w_ffn_down[i].astype(BF16))
        x = outproj_call(x, mod, o_f, o_b, z, fm2, gdn_norm[l].reshape(1, HD), w_out[l].astype(BF16), ffn,
                         seg=seg, tm=tm)
        if l % 2 == 1:
            x = moe_block(x, mod, nw, w_router[i], w_exp_gate[i].astype(BF16), w_exp_up[i].astype(BF16),
                          w_exp_down[i].astype(BF16), seg=seg, tm=tm, tmd=tmd, tme=tme)

    wn = norm_final.reshape(1, d)
    y_p = final_call(x, wn, row0=0, nrows=sp, tm=tm).reshape(bp, sp, d)
    y_s = final_call(x, wn, row0=sp, nrows=bs * seg, tm=tm).reshape(bs, seg, d)
    return y_p, y_s


def kernel(x_prompt, x_sample, c_prompt, c_sample, w_ada, b_ada, norm_mix, norm_ffn, w_in, conv_w, a_log,
           dt_bias, gdn_norm, w_out, w_ffn_gate, w_ffn_up, w_ffn_down, w_router, w_exp_gate, w_exp_up,
           w_exp_down, norm_final):
    return encoder_pair(x_prompt, x_sample, c_prompt, c_sample, w_ada, b_ada, norm_mix, norm_ffn, w_in,
                        conv_w, a_log, dt_bias, gdn_norm, w_out, w_ffn_gate, w_ffn_up, w_ffn_down,
                        w_router, w_exp_gate, w_exp_up, w_exp_down, norm_final)
```

```python
import functools
import math

import numpy as np
import jax
import jax.numpy as jnp
from jax import lax
from jax.experimental import pallas as pl
from jax.experimental.pallas import tpu as pltpu

F32 = jnp.float32
BF16 = jnp.bfloat16

D_MODEL = 1024
DEPTH = 4
N_GROUPS = 4
GROUP_DIM = 64
F_DIM = N_GROUPS * GROUP_DIM
HD = 128
NH = 6
G_DIM = NH * HD
QKV_DIM = 3 * G_DIM
CONV_K = 5
CHUNK = 64
PREP_CHUNKS = 2
D_FF = 2816
N_EXPERTS = 8
D_EXPERT = 3584
NORM_EPS = 1e-6

LANES = 128
SUBLANES = 8
MXU_COLS = 256
VMEM_LIMIT = 56 * 1024 * 1024
DFT1 = 128

GL_GAMMA, GL_BETA, GL_EG, GL_EGR = 0, 16, 32, 48
NEG_BIG = -1e30


def _cparams(sem):
    return pltpu.CompilerParams(dimension_semantics=sem, vmem_limit_bytes=VMEM_LIMIT)


def _mod_norm(x, nw, sc, sh):
    ms = jnp.mean(x * x, axis=-1, keepdims=True)
    y = x * lax.rsqrt(ms + NORM_EPS)
    return (y * nw) * (1.0 + sc) + sh


def _silu(x):
    hx = 0.5 * x
    return hx + hx * jnp.tanh(hx)


def _ada_kernel(c_ref, w_ref, b_ref, o_ref):
    c = _silu(c_ref[...])
    o_ref[0] = jnp.dot(c.astype(BF16), w_ref[0].astype(BF16), preferred_element_type=F32) + b_ref[0]


def ada_call(c_all, w_ada, b_ada):
    nrow = c_all.shape[0]
    depth, d, d6 = w_ada.shape
    tn = 1024
    return pl.pallas_call(
        _ada_kernel,
        grid=(depth, d6 // tn),
        in_specs=[
            pl.BlockSpec((nrow, d), lambda l, j: (0, 0)),
            pl.BlockSpec((1, d, tn), lambda l, j: (l, 0, j)),
            pl.BlockSpec((1, 1, tn), lambda l, j: (l, 0, j)),
        ],
        out_specs=pl.BlockSpec((1, nrow, tn), lambda l, j: (l, 0, j)),
        out_shape=jax.ShapeDtypeStruct((depth, nrow, d6), F32),
        compiler_params=_cparams(("arbitrary", "arbitrary")),
        name="ada",
    )(c_all, w_ada, b_ada.reshape(depth, 1, d6))


def _inproj_kernel(first_ref, last_ref,
                   x_ref, xp_ref, xn_ref, mod_ref, nw_ref, wf_ref, wqkv_ref, wz_ref, wba_ref, cw_ref, gp_ref,
                   f_ref, z_ref, q_ref, k_ref, v_ref, gcol_ref, grow_ref, gend_ref,
                   xe_ref):
    i = pl.program_id(0)
    tm = x_ref.shape[0]
    nc = tm // CHUNK
    halo = SUBLANES
    xcat = jnp.concatenate([xp_ref[...], x_ref[...], xn_ref[...]], axis=0)
    h = _mod_norm(xcat, nw_ref[...], mod_ref[0, 1:2, :], mod_ref[0, 0:1, :]).astype(BF16)
    pm = jnp.where(first_ref[i] == 1, 0.0, 1.0)
    nm = jnp.where(last_ref[i] == 1, 0.0, 1.0)
    outs = (q_ref, k_ref, v_ref)

    def side_work(cb):
        nz = G_DIM // MXU_COLS
        last = QKV_DIM // MXU_COLS - 1
        if cb == 0:
            ba = jnp.dot(h, wba_ref[...], preferred_element_type=F32)[halo:halo + tm]
            _gates(ba, gp_ref, gcol_ref, grow_ref, gend_ref)
        elif last - nz <= cb < last:
            z0 = (cb - (last - nz)) * MXU_COLS
            z_ref[:, z0:z0 + MXU_COLS] = jnp.dot(h, wz_ref[:, z0:z0 + MXU_COLS], preferred_element_type=F32)[
                halo:halo + tm].astype(z_ref.dtype)
        elif cb == last - nz - 1:
            f_ref[...] = jnp.dot(h, wf_ref[...], preferred_element_type=F32)[halo:halo + tm].astype(f_ref.dtype)

    for cb in range(QKV_DIM // MXU_COLS):
        c0 = cb * MXU_COLS
        res = jnp.dot(h, wqkv_ref[:, c0:c0 + MXU_COLS], preferred_element_type=F32)
        xe_ref[0:halo, c0:c0 + MXU_COLS] = res[0:halo] * pm
        xe_ref[halo:halo + tm, c0:c0 + MXU_COLS] = res[halo:halo + tm]
        xe_ref[halo + tm:, c0:c0 + MXU_COLS] = res[halo + tm:] * nm
        for s in range(cb * (MXU_COLS // HD), (cb + 1) * (MXU_COLS // HD)):
            lo = s * HD
            xs = xe_ref[:, lo:lo + HD]
            acc = None
            for j in range(CONV_K):
                shift = (CONV_K // 2 - j) % (tm + 2 * halo)
                xj = xs if shift == 0 else pltpu.roll(xs, shift, axis=0)
                term = xj[halo:halo + tm] * cw_ref[j:j + 1, lo:lo + HD]
                acc = term if acc is None else acc + term
            y = _silu(acc)
            which, head = divmod(s, NH)
            if which < 2:
                y = y * lax.rsqrt(jnp.sum(y * y, axis=-1, keepdims=True) + 1e-6)
            if which == 0:
                y = y * (HD ** -0.5)
            outs[which][:, head * HD:(head + 1) * HD] = y
        side_work(cb)


def _gates(ba, gp_ref, gcol_ref, grow_ref, gend_ref):
    tm = ba.shape[0]
    nc = tm // CHUNK
    lane = lax.broadcasted_iota(jnp.int32, (tm, LANES), 1)
    beta = 1.0 / (1.0 + jnp.exp(-ba))
    xs = ba + gp_ref[0:1, :]
    softplus = jnp.maximum(xs, 0.0) + jnp.log(1.0 + jnp.exp(-jnp.abs(xs)))
    g = -jnp.exp(gp_ref[1:2, :]) * softplus
    g = jnp.where((lane >= 2 * NH) & (lane < 4 * NH), g, 0.0)
    g = pltpu.roll(g, LANES - 2 * NH, axis=1)

    rowc = lax.broadcasted_iota(jnp.int32, (tm, LANES), 0) % CHUNK
    p = g
    sh = 1
    while sh < CHUNK:
        p = p + jnp.where(rowc >= sh, pltpu.roll(p, sh, axis=0), 0.0)
        sh *= 2
    g3 = g.reshape(nc, CHUNK, LANES)
    tot = jnp.broadcast_to(jnp.sum(g3, axis=1, keepdims=True), (nc, CHUNK, LANES)).reshape(tm, LANES)
    is_bwd = (lane >= NH) & (lane < 2 * NH)
    gamma = jnp.where(is_bwd, tot - p + g, p)
    eg = jnp.exp(gamma)
    egr = jnp.exp(tot - gamma)
    m12 = lane < 2 * NH
    gcol = (jnp.where(m12, gamma, 0.0)
            + pltpu.roll(jnp.where(m12, beta, 0.0), GL_BETA, axis=1)
            + pltpu.roll(jnp.where(m12, eg, 0.0), GL_EG, axis=1)
            + pltpu.roll(jnp.where(m12, egr, 0.0), GL_EGR, axis=1))
    gcol_ref[...] = gcol
    gam_t = jnp.where(m12, gamma, 0.0).T[0:2 * SUBLANES, :]
    g_t = g.T[0:2 * SUBLANES, :]
    for c in range(nc):
        grow_ref[c] = gam_t[:, c * CHUNK:(c + 1) * CHUNK]
        tc = jnp.sum(g_t[:, c * CHUNK:(c + 1) * CHUNK], axis=-1, keepdims=True)
        gend_ref[c] = jnp.exp(jnp.broadcast_to(tc, (2 * SUBLANES, LANES)))


def inproj_call(first, last, x, mod, nw, wf, wqkv, wz, wba, cw, gp, *, seg, tm):
    t, d = x.shape
    nblk8 = t // SUBLANES
    r8 = tm // SUBLANES
    nc = tm // CHUNK
    row = lambda i, f, l: (i, 0)
    const = lambda i, f, l: (0, 0)
    chunked = lambda i, f, l: (i, 0, 0)
    single = pl.Buffered(1)
    grid_spec = pltpu.PrefetchScalarGridSpec(
        num_scalar_prefetch=2,
        grid=(t // tm,),
        in_specs=[
            pl.BlockSpec((tm, d), row),
            pl.BlockSpec((SUBLANES, d), lambda i, f, l: (jnp.maximum(i * r8 - 1, 0), 0)),
            pl.BlockSpec((SUBLANES, d), lambda i, f, l: (jnp.minimum((i + 1) * r8, nblk8 - 1), 0)),
            pl.BlockSpec((1, SUBLANES, d), lambda i, f, l: (i * tm // seg, 0, 0)),
            pl.BlockSpec((1, d), const),
            pl.BlockSpec(wf.shape, const, pipeline_mode=single),
            pl.BlockSpec(wqkv.shape, const, pipeline_mode=single),
            pl.BlockSpec(wz.shape, const, pipeline_mode=single),
            pl.BlockSpec(wba.shape, const, pipeline_mode=single),
            pl.BlockSpec((SUBLANES, QKV_DIM), const),
            pl.BlockSpec((SUBLANES, LANES), const),
        ],
        out_specs=[
            pl.BlockSpec((tm, F_DIM), row),
            pl.BlockSpec((tm, G_DIM), row),
            pl.BlockSpec((tm, G_DIM), row),
            pl.BlockSpec((tm, G_DIM), row),
            pl.BlockSpec((tm, G_DIM), row),
            pl.BlockSpec((tm, LANES), row),
            pl.BlockSpec((nc, 2 * SUBLANES, CHUNK), chunked),
            pl.BlockSpec((nc, 2 * SUBLANES, LANES), chunked),
        ],
        scratch_shapes=[pltpu.VMEM((tm + 2 * SUBLANES, QKV_DIM), F32)],
    )
    return pl.pallas_call(
        _inproj_kernel,
        grid_spec=grid_spec,
        out_shape=[
            jax.ShapeDtypeStruct((t, F_DIM), BF16),
            jax.ShapeDtypeStruct((t, G_DIM), BF16),
            jax.ShapeDtypeStruct((t, G_DIM), F32),
            jax.ShapeDtypeStruct((t, G_DIM), F32),
            jax.ShapeDtypeStruct((t, G_DIM), F32),
            jax.ShapeDtypeStruct((t, LANES), F32),
            jax.ShapeDtypeStruct((t // CHUNK, 2 * SUBLANES, CHUNK), F32),
            jax.ShapeDtypeStruct((t // CHUNK, 2 * SUBLANES, LANES), F32),
        ],
        compiler_params=_cparams(("arbitrary",)),
        name="inproj_conv",
    )(first, last, x, x, x, mod, nw, wf, wqkv, wz, wba, cw, gp)


def _gdn_kernel(rf_ref, rb_ref,
                qf_ref, kf_ref, vf_ref, gcf_ref, grf_ref, gef_ref,
                qb_ref, kb_ref, vb_ref, gcb_ref, grb_ref, geb_ref,
                of_ref, ob_ref, s_ref, wq_ref, ab_ref, u_ref):
    i = pl.program_id(0)
    tb = qf_ref.shape[0]
    nc = tb // CHUNK

    @pl.when(rf_ref[i] == 1)
    def _():
        s_ref[0:NH] = jnp.zeros((NH, HD, HD), F32)

    @pl.when(rb_ref[i] == 1)
    def _():
        s_ref[NH:2 * NH] = jnp.zeros((NH, HD, HD), F32)

    row = lax.broadcasted_iota(jnp.int32, (CHUNK, CHUNK), 0)
    col = lax.broadcasted_iota(jnp.int32, (CHUNK, CHUNK), 1)
    eye = jnp.where(row == col, 1.0, 0.0).astype(F32)
    level_masks = []
    b = 1
    while b < CHUNK:
        level_masks.append((row // (2 * b) == col // (2 * b)) & (row // b != col // b))
        b *= 2
    dirs = (
        (qf_ref, kf_ref, vf_ref, gcf_ref, grf_ref, gef_ref, of_ref, row >= col, row > col),
        (qb_ref, kb_ref, vb_ref, gcb_ref, grb_ref, geb_ref, ob_ref, row <= col, row < col),
    )

    hds = [(d, h) for d in range(2) for h in range(NH)]
    probs = [(u, d, h) for u in range(PREP_CHUNKS) for d, h in hds]
    npr = len(probs)

    def prep_step(it, carry):
        cs = [[it * PREP_CHUNKS + u for u in range(PREP_CHUNKS)],
              [nc - 1 - (it * PREP_CHUNKS + u) for u in range(PREP_CHUNKS)]]
        qs, ks, vs, cols, decs = [], [], [], [], []
        for u, d, h in probs:
            r0 = pl.multiple_of(cs[d][u] * CHUNK, CHUNK)
            q_ref, k_ref, v_ref, gc_ref, gr_ref = dirs[d][0:5]
            hd = d * NH + h
            lo = h * HD
            gc = gc_ref[pl.ds(r0, CHUNK), :]
            qs.append(q_ref[pl.ds(r0, CHUNK), lo:lo + HD])
            ks.append(k_ref[pl.ds(r0, CHUNK), lo:lo + HD])
            vs.append(v_ref[pl.ds(r0, CHUNK), lo:lo + HD])
            gam_c = gc[:, GL_GAMMA + hd:GL_GAMMA + hd + 1]
            cols.append((gc[:, GL_BETA + hd:GL_BETA + hd + 1], gc[:, GL_EG + hd:GL_EG + hd + 1],
                         gc[:, GL_EGR + hd:GL_EGR + hd + 1]))
            gam_r = gr_ref[cs[d][u]][hd:hd + 1, :]
            decs.append(jnp.exp(jnp.where(dirs[d][7], gam_c - gam_r, NEG_BIG)))
        kqs = [lax.dot_general(jnp.concatenate([ks[n], qs[n]], axis=0), ks[n],
                               (((1,), (1,)), ((), ())), preferred_element_type=F32)
               for n in range(npr)]
        a_s = [jnp.where(dirs[d][8], kqs[n][0:CHUNK] * cols[n][0] * decs[n], 0.0)
               for n, (u, d, h) in enumerate(probs)]
        for n, (u, d, h) in enumerate(probs):
            ab_ref[d, cs[d][u], h, 0:CHUNK, :] = kqs[n][CHUNK:] * decs[n]
            ab_ref[d, cs[d][u], h, CHUNK:, :] = (ks[n] * cols[n][2]).T
        ts = [eye - jnp.where(level_masks[0], a_s[n], 0.0) for n in range(npr)]
        for lm in level_masks[1:]:
            lts = [jnp.dot(jnp.where(lm, a_s[n], 0.0), ts[n], preferred_element_type=F32) for n in range(npr)]
            ts = [ts[n] - jnp.dot(ts[n], lts[n], preferred_element_type=F32) for n in range(npr)]
        uws =[jnp.dot(ts[n], jnp.concatenate([vs[n] * cols[n][0], ks[n] * (cols[n][0] * cols[n][1])], axis=1),
                       preferred_element_type=F32) for n in range(npr)]
        for n, (u, d, h) in enumerate(probs):
            u_ref[d, cs[d][u], h] = uws[n][:, 0:HD]
            wq_ref[d, cs[d][u], h, 0:CHUNK, :] = uws[n][:, HD:]
            wq_ref[d, cs[d][u], h, CHUNK:, :] = qs[n] * cols[n][1]
        return carry


    def scan_step(c, carry):
        ccs = (c, nc - 1 - c)
        sts = [s_ref[d * NH + h] for d, h in hds]
        wqs = [jnp.dot(wq_ref[d, ccs[d], h], sts[n], preferred_element_type=F32)
               for n, (d, h) in enumerate(hds)]
        vns = [u_ref[d, ccs[d], h] - wqs[n][0:CHUNK] for n, (d, h) in enumerate(hds)]
        avs = [jnp.dot(ab_ref[d, ccs[d], h], vns[n], preferred_element_type=F32)
               for n, (d, h) in enumerate(hds)]
        for n, (d, h) in enumerate(hds):
            r0 = pl.multiple_of(ccs[d] * CHUNK, CHUNK)
            dirs[d][6][pl.ds(r0, CHUNK), h * HD:(h + 1) * HD] = (
                wqs[n][CHUNK:] + avs[n][0:CHUNK]).astype(dirs[d][6].dtype)
            ge = dirs[d][5][ccs[d]]
            hd = d * NH + h
            s_ref[hd] = sts[n] * ge[hd:hd + 1, :] + avs[n][CHUNK:]
        return carry

    nit = nc // PREP_CHUNKS

    def scan_group(it):
        for u in range(PREP_CHUNKS):
            scan_step(it * PREP_CHUNKS + u, 0)

    def merged_step(it, carry):
        scan_group(it - 1)
        prep_step(it, 0)
        return carry

    prep_step(0, 0)
    lax.fori_loop(1, nit, merged_step, 0)
    scan_group(nit - 1)


def gdn_call(reset_f, reset_b, q, k, v, gcol, grow, gend, *, tb):
    t = q.shape[0]
    nb = t // tb
    nc = tb // CHUNK
    fwd2 = lambda i, a, b: (i, 0)
    bwd2 = lambda i, a, b: (nb - 1 - i, 0)
    fwd3 = lambda i, a, b: (i, 0, 0)
    bwd3 = lambda i, a, b: (nb - 1 - i, 0, 0)

    def specs(m2, m3):
        return [
            pl.BlockSpec((tb, G_DIM), m2), pl.BlockSpec((tb, G_DIM), m2), pl.BlockSpec((tb, G_DIM), m2),
            pl.BlockSpec((tb, LANES), m2),
            pl.BlockSpec((nc, 2 * SUBLANES, CHUNK), m3),
            pl.BlockSpec((nc, 2 * SUBLANES, LANES), m3),
        ]

    grid_spec = pltpu.PrefetchScalarGridSpec(
        num_scalar_prefetch=2,
        grid=(nb,),
        in_specs=specs(fwd2, fwd3) + specs(bwd2, bwd3),
        out_specs=[pl.BlockSpec((tb, G_DIM), fwd2), pl.BlockSpec((tb, G_DIM), bwd2)],
        scratch_shapes=[pltpu.VMEM((2 * NH, HD, HD), F32),
                        pltpu.VMEM((2, nc, NH, 2 * CHUNK, HD), F32),
                        pltpu.VMEM((2, nc, NH, CHUNK + HD, CHUNK), F32),
                        pltpu.VMEM((2, nc, NH, CHUNK, HD), F32)],
    )
    return pl.pallas_call(
        _gdn_kernel,
        grid_spec=grid_spec,
        out_shape=[jax.ShapeDtypeStruct((t, G_DIM), BF16), jax.ShapeDtypeStruct((t, G_DIM), BF16)],
        compiler_params=_cparams(("arbitrary",)),
        name="gdn",
    )(reset_f, reset_b, q, k, v, gcol, grow, gend, q, k, v, gcol, grow, gend)


def _fft1_kernel(x_ref, m1_ref, tc_ref, ts_ref, br_ref, bi_ref):
    s1 = x_ref.shape[1]
    a = jnp.dot(m1_ref[...], x_ref[0], preferred_element_type=F32)
    ar, ai = a[0:s1], a[s1:]
    tc, ts = tc_ref[...], ts_ref[...]
    br_ref[0] = (ar * tc + ai * ts).astype(br_ref.dtype)
    bi_ref[0] = (ai * tc - ar * ts).astype(bi_ref.dtype)


def fft1_call(x3, m1, twc, tws, *, tcol):
    nseq, s1, cols = x3.shape
    blk = pl.BlockSpec((1, s1, tcol), lambda j, b: (b, 0, j))
    tw = pl.BlockSpec((s1, tcol), lambda j, b: (0, j))
    return pl.pallas_call(
        _fft1_kernel,
        grid=(cols // tcol, nseq),
        in_specs=[blk, pl.BlockSpec(m1.shape, lambda j, b: (0, 0)), tw, tw],
        out_specs=[blk, blk],
        out_shape=[jax.ShapeDtypeStruct(x3.shape, BF16)] * 2,
        compiler_params=_cparams(("arbitrary", "arbitrary")),
        name="fft_stage1",
    )(x3, m1, twc, tws)


def _fft2_kernel(br_ref, bi_ref, mc_ref, m2_ref, o_ref):
    _, tk, s2, c = br_ref.shape
    nsplit = o_ref.shape[0]
    s2o = s2 // nsplit
    b = jnp.concatenate([br_ref[0].reshape(tk * s2, c), bi_ref[0].reshape(tk * s2, c)], axis=1)
    z = jnp.dot(b, mc_ref[...], preferred_element_type=F32)
    m2 = m2_ref[...]
    for kk in range(tk):
        zk = z[kk * s2:(kk + 1) * s2]
        x = jnp.dot(m2, jnp.concatenate([zk[:, 0:c], zk[:, c:]], axis=0),
                    preferred_element_type=F32)
        for sp in range(nsplit):
            o_ref[sp, kk] = x[sp * s2o:(sp + 1) * s2o].astype(o_ref.dtype)


def fft2_call(br4, bi4, mc, m2, *, tk, nsplit):
    nseq, s1, s2, c = br4.shape
    s2o = s2 // nsplit
    blk = pl.BlockSpec((1, tk, s2, c), lambda b, j: (b, j, 0, 0))
    return pl.pallas_call(
        _fft2_kernel,
        grid=(nseq, s1 // tk),
        in_specs=[blk, blk, pl.BlockSpec(mc.shape, lambda b, j: (0, 0)),
                  pl.BlockSpec(m2.shape, lambda b, j: (0, 0))],
        out_specs=pl.BlockSpec((nsplit, tk, s2o, c), lambda b, j: (b, j, 0, 0)),
        out_shape=jax.ShapeDtypeStruct((nseq * nsplit, s1, s2o, c), BF16),
        compiler_params=_cparams(("arbitrary", "arbitrary")),
        name="fft_stage2",
    )(br4, bi4, mc, m2)


def _dft_tables(s):
    s1 = DFT1
    s2 = s // s1
    k = np.arange(s1)
    ang1 = 2.0 * np.pi * ((k[:, None] * k[None, :]) % s1) / s1
    sc = 1.0 / math.sqrt(s)
    m1 = np.concatenate([np.cos(ang1), -np.sin(ang1)], axis=0) * sc
    n2 = np.arange(s2)
    angt = 2.0 * np.pi * ((k[:, None] * n2[None, :]) % s) / s
    twc = np.repeat(np.cos(angt), F_DIM, axis=1)
    tws = np.repeat(np.sin(angt), F_DIM, axis=1)
    ang2 = 2.0 * np.pi * ((n2[:, None] * n2[None, :]) % s2) / s2
    m2 = np.concatenate([np.cos(ang2), np.sin(ang2)], axis=1)
    return (jnp.asarray(m1, F32), jnp.asarray(np.cos(angt), F32), jnp.asarray(np.sin(angt), F32),
            jnp.asarray(m2, F32))


def _channel_dft_matrix():
    g = np.arange(GROUP_DIM)
    ang = 2.0 * np.pi * ((g[:, None] * g[None, :]) % GROUP_DIM) / GROUP_DIM
    cg = np.kron(np.eye(N_GROUPS), np.cos(ang)) / math.sqrt(GROUP_DIM)
    sg = np.kron(np.eye(N_GROUPS), np.sin(ang)) / math.sqrt(GROUP_DIM)
    return jnp.asarray(np.block([[cg, -sg], [sg, cg]]), F32)


def fourier_mix_call(f, *, nseq, s, tcol, tk, nsplit):
    s1 = DFT1
    s2 = s // s1
    m1, tcs, tss, m2 = _dft_tables(s)
    twc = jnp.broadcast_to(tcs[:, :, None], (s1, s2, F_DIM)).reshape(s1, s2 * F_DIM)
    tws = jnp.broadcast_to(tss[:, :, None], (s1, s2, F_DIM)).reshape(s1, s2 * F_DIM)
    x3 = f.reshape(nseq, s1, s2 * F_DIM)
    br, bi = fft1_call(x3, m1.astype(BF16), twc, tws, tcol=tcol)
    out = fft2_call(br.reshape(nseq, s1, s2, F_DIM), bi.reshape(nseq, s1, s2, F_DIM),
                    _channel_dft_matrix().astype(BF16), m2, tk=tk, nsplit=nsplit)
    return out.reshape(nseq * nsplit, s1, (s2 // nsplit) * F_DIM)


def _mixer_out(x_ref, mod_ref, of_ref, ob_ref, z_ref, fm_ref, gn_ref, w_ref):
    tm = x_ref.shape[0]
    o = of_ref[...].astype(F32) + ob_ref[...].astype(F32)
    z = z_ref[...].astype(F32)
    gn = gn_ref[...]
    parts = []
    fm = fm_ref[...]
    parts.append(jnp.concatenate(
        [fm[:, j * F_DIM:(j + 1) * F_DIM] for j in range(tm // DFT1)], axis=0).astype(BF16))
    for h in range(NH):
        oh = o[:, h * HD:(h + 1) * HD]
        ms = jnp.mean(oh * oh, axis=-1, keepdims=True)
        y = (oh * lax.rsqrt(ms + NORM_EPS)) * gn
        parts.append((y * _silu(z[:, h * HD:(h + 1) * HD])).astype(BF16))
    mixed = jnp.concatenate(parts, axis=1)
    proj = jnp.dot(mixed, w_ref[...], preferred_element_type=F32)
    return x_ref[...] + mod_ref[0, 2:3, :] * proj


def _swiglu_residual(x, mod_ref, nw_ref, wg_ref, wu_ref, wd_ref, nsplit):
    h = _mod_norm(x, nw_ref[...], mod_ref[0, 4:5, :], mod_ref[0, 3:4, :]).astype(BF16)
    cw = wg_ref.shape[1] // nsplit
    acc = None
    for c in range(nsplit):
        g = jnp.dot(h, wg_ref[:, c * cw:(c + 1) * cw], preferred_element_type=F32)
        u = jnp.dot(h, wu_ref[:, c * cw:(c + 1) * cw], preferred_element_type=F32)
        hid = (_silu(g) * u).astype(BF16)
        part = jnp.dot(hid, wd_ref[c * cw:(c + 1) * cw, :], preferred_element_type=F32)
        acc = part if acc is None else acc + part
    return x + mod_ref[0, 5:6, :] * acc


def _outproj_kernel(x_ref, mod_ref, of_ref, ob_ref, z_ref, fm_ref, gn_ref, w_ref, o_ref):
    o_ref[...] = _mixer_out(x_ref, mod_ref, of_ref, ob_ref, z_ref, fm_ref, gn_ref, w_ref)


def _outproj_ffn_kernel(x_ref, mod_ref, of_ref, ob_ref, z_ref, fm_ref, gn_ref, w_ref,
                        nw_ref, wg_ref, wu_ref, wd_ref, o_ref, *, nsplit):
    x_mid = _mixer_out(x_ref, mod_ref, of_ref, ob_ref, z_ref, fm_ref, gn_ref, w_ref)
    o_ref[...] = _swiglu_residual(x_mid, mod_ref, nw_ref, wg_ref, wu_ref, wd_ref, nsplit)


def _outproj_router_kernel(x_ref, mod_ref, of_ref, ob_ref, z_ref, fm_ref, gn_ref, w_ref,
                           nw_ref, wr_ref, tri_ref, o_ref, oi_ref, op_ref, cnt_ref, carry_ref):
    x_mid = _mixer_out(x_ref, mod_ref, of_ref, ob_ref, z_ref, fm_ref, gn_ref, w_ref)
    o_ref[...] = x_mid
    _route(x_mid, mod_ref, nw_ref, wr_ref, tri_ref, oi_ref, op_ref, cnt_ref, carry_ref)


def outproj_call(x, mod, o_f, o_b, z, fm2, gn, w, ffn=None, router=None, *, seg, tm):
    t, d = x.shape
    row = lambda i: (i, 0)
    const = lambda i: (0, 0)
    per_seg = seg // tm
    single = pl.Buffered(1)
    in_specs = [
        pl.BlockSpec((tm, d), row),
        pl.BlockSpec((1, SUBLANES, d), lambda i: (i // per_seg, 0, 0)),
        pl.BlockSpec((tm, G_DIM), row),
        pl.BlockSpec((tm, G_DIM), row),
        pl.BlockSpec((tm, G_DIM), row),
        pl.BlockSpec((DFT1, (tm // DFT1) * F_DIM), lambda i: (i // per_seg, i % per_seg)),
        pl.BlockSpec((1, HD), const),
        pl.BlockSpec(w.shape, const, pipeline_mode=single),
    ]
    args = [x, mod, o_f, o_b, z, fm2, gn, w]
    body = _outproj_kernel
    if ffn is not None:
        nw, wg, wu, wd = ffn
        in_specs += [pl.BlockSpec((1, d), const)] + [
            pl.BlockSpec(a.shape, const, pipeline_mode=single) for a in (wg, wu, wd)]
        args += [nw, wg, wu, wd]
        body = functools.partial(_outproj_ffn_kernel, nsplit=wg.shape[1] // MXU_COLS)
    out_specs = pl.BlockSpec((tm, d), row)
    out_shape = jax.ShapeDtypeStruct((t, d), F32)
    scratch = []
    name = "outproj" if ffn is None else "outproj_ffn"
    if router is not None:
        nw, wr, tri = router
        in_specs += [pl.BlockSpec((1, d), const), pl.BlockSpec(wr.shape, const), pl.BlockSpec(tri.shape, const)]
        args += [nw, wr, tri]
        body = _outproj_router_kernel
        out_specs = [out_specs, pl.BlockSpec((tm, LANES), row), pl.BlockSpec((tm, LANES), row),
                     pl.BlockSpec((SUBLANES, LANES), const)]
        out_shape = [out_shape, jax.ShapeDtypeStruct((t, LANES), jnp.int32),
                     jax.ShapeDtypeStruct((t, LANES), F32), jax.ShapeDtypeStruct((SUBLANES, LANES), F32)]
        scratch = [pltpu.VMEM((SUBLANES, LANES), F32)]
        name = "outproj_router"
    return pl.pallas_call(
        body,
        grid=(t // tm,),
        in_specs=in_specs,
        out_specs=out_specs,
        out_shape=out_shape,
        scratch_shapes=scratch,
        compiler_params=_cparams(("arbitrary",)),
        name=name,
    )(*args)


def _route(x, mod_ref, nw_ref, wr_ref, tri_ref, oi_ref, op_ref, cnt_ref, carry_ref):
    i = pl.program_id(0)
    tm = x.shape[0]

    @pl.when(i == 0)
    def _():
        carry_ref[...] = jnp.zeros_like(carry_ref)

    h = _mod_norm(x, nw_ref[...], mod_ref[0, 4:5, :], mod_ref[0, 3:4, :])
    logits = jnp.dot(h.astype(BF16), wr_ref[...], preferred_element_type=F32)
    lane = lax.broadcasted_iota(jnp.int32, (tm, LANES), 1)
    logits = jnp.where(lane < N_EXPERTS, logits, NEG_BIG)
    l1 = jnp.max(logits, axis=-1, keepdims=True)
    i1 = jnp.min(jnp.where(logits == l1, lane, LANES), axis=-1, keepdims=True)
    rest = jnp.where(lane == i1, NEG_BIG, logits)
    l2 = jnp.max(rest, axis=-1, keepdims=True)
    i2 = jnp.min(jnp.where(rest == l2, lane, LANES), axis=-1, keepdims=True)
    e21 = jnp.exp(l2 - l1)
    p1 = 1.0 / (1.0 + e21)
    p2 = e21 * p1
    oh1 = lane == i1
    oh2 = lane == i2
    oh = jnp.where(oh1 | oh2, 1.0, 0.0).astype(BF16)
    before = jnp.dot(tri_ref[...], oh, preferred_element_type=F32) + carry_ref[0:1, :]
    r1 = jnp.sum(jnp.where(oh1, before, 0.0), axis=-1, keepdims=True).astype(jnp.int32)
    r2 = jnp.sum(jnp.where(oh2, before, 0.0), axis=-1, keepdims=True).astype(jnp.int32)
    oi_ref[...] = jnp.where(lane == 0, i1, jnp.where(lane == 1, i2, jnp.where(lane == 2, r1, r2)))
    op_ref[...] = jnp.where(lane == 0, p1, p2)
    new_carry = carry_ref[0:1, :] + jnp.sum(oh.astype(F32), axis=0, keepdims=True)
    carry_ref[...] = jnp.broadcast_to(new_carry, carry_ref.shape)
    cnt_ref[...] = jnp.broadcast_to(new_carry, cnt_ref.shape)


def _dispatch_kernel(pos_ref, gaps_ref, x_ref, mod_ref, nw_ref, xs_ref, hbuf, zrow, sem, zsem):
    i = pl.program_id(0)
    last = pl.num_programs(0) - 1
    tm = x_ref.shape[0]
    buf = i % 2

    @pl.when(i == 0)
    def _():
        zrow[...] = jnp.zeros_like(zrow)

        def zero_copy(row):
            return pltpu.make_async_copy(zrow.at[pl.ds(0, 1), :], xs_ref.at[pl.ds(row, 1), :], zsem)

        for g in range(gaps_ref.shape[2] // 2):
            lo = gaps_ref[0, 0, 2 * g]
            n = gaps_ref[0, 0, 2 * g + 1]
            lax.fori_loop(0, n, lambda r, c: (zero_copy(lo + r).start(), c)[1], 0)
        for g in range(gaps_ref.shape[2] // 2):
            lo = gaps_ref[0, 0, 2 * g]
            n = gaps_ref[0, 0, 2 * g + 1]
            lax.fori_loop(0, n, lambda r, c: (zero_copy(lo + r).wait(), c)[1], 0)

    hbuf[buf] = _mod_norm(x_ref[...], nw_ref[...], mod_ref[0, 4:5, :], mod_ref[0, 3:4, :])

    def start(r, c):
        for slot in range(2):
            pltpu.make_async_copy(hbuf.at[buf, pl.ds(r, 1), :],
                                  xs_ref.at[pl.ds(pos_ref[0, 0, slot * tm + r], 1), :],
                                  sem.at[buf]).start(priority=slot)
        return c

    lax.fori_loop(0, tm, start, 0, unroll=8)

    def wait_rows(b):
        for _ in range(2):
            pltpu.make_async_copy(hbuf.at[b], xs_ref.at[pl.ds(0, tm), :], sem.at[b]).wait()

    @pl.when(i > 0)
    def _():
        wait_rows(1 - buf)

    @pl.when(i == last)
    def _():
        wait_rows(buf)


def dispatch_call(pos3, gaps3, x, mod, nw, *, nr, seg, tm):
    t, d = x.shape
    return pl.pallas_call(
        _dispatch_kernel,
        grid=(t // tm,),
        in_specs=[
            pl.BlockSpec((1, 1, 2 * tm), lambda i: (i, 0, 0), memory_space=pltpu.SMEM),
            pl.BlockSpec(gaps3.shape, lambda i: (0, 0, 0), memory_space=pltpu.SMEM),
            pl.BlockSpec((tm, d), lambda i: (i, 0)),
            pl.BlockSpec((1, SUBLANES, d), lambda i: (i * tm // seg, 0, 0)),
            pl.BlockSpec((1, d), lambda i: (0, 0)),
        ],
        out_specs=pl.BlockSpec(memory_space=pl.ANY),
        out_shape=jax.ShapeDtypeStruct((nr, d), F32),
        scratch_shapes=[pltpu.VMEM((2, tm, d), F32), pltpu.VMEM((SUBLANES, d), F32),
                        pltpu.SemaphoreType.DMA((2,)), pltpu.SemaphoreType.DMA(())],
        compiler_params=_cparams(("arbitrary",)),
        name="moe_dispatch",
    )(pos3, gaps3, x, mod, nw)


def _expert_kernel(te_ref, nu_ref, xs_ref, wg_ref, wu_ref, wd_ref, y_ref, acc_ref):
    j = pl.program_id(0)
    half = pl.program_id(1)

    @pl.when(j < nu_ref[0])
    def _():
        h = xs_ref[...].astype(BF16)
        g = jnp.dot(h, wg_ref[0], preferred_element_type=F32)
        u = jnp.dot(h, wu_ref[0], preferred_element_type=F32)
        hid = (_silu(g) * u).astype(BF16)
        part = jnp.dot(hid, wd_ref[0], preferred_element_type=F32)

        @pl.when(half == 0)
        def _():
            acc_ref[...] = part

        @pl.when(half == 1)
        def _():
            y_ref[...] = acc_ref[...] + part

    @pl.when((j >= nu_ref[0]) & (half == 1))
    def _():
        y_ref[...] = jnp.zeros_like(y_ref)


def expert_call(tile_expert, n_used, xs, wg, wu, wd, *, tme):
    nr, d = xs.shape
    ntiles = nr // tme
    fh = wg.shape[2] // 2

    def jj(j, nu):
        return jnp.minimum(j, nu[0] - 1)

    def hh(j, hf, nu):
        return jnp.where(j < nu[0], hf, 1)

    grid_spec = pltpu.PrefetchScalarGridSpec(
        num_scalar_prefetch=2,
        grid=(ntiles, 2),
        in_specs=[
            pl.BlockSpec((tme, d), lambda j, hf, te, nu: (jj(j, nu), 0)),
            pl.BlockSpec((1, d, fh), lambda j, hf, te, nu: (te[jj(j, nu)], 0, hh(j, hf, nu))),
            pl.BlockSpec((1, d, fh), lambda j, hf, te, nu: (te[jj(j, nu)], 0, hh(j, hf, nu))),
            pl.BlockSpec((1, fh, d), lambda j, hf, te, nu: (te[jj(j, nu)], hh(j, hf, nu), 0)),
        ],
        out_specs=pl.BlockSpec((tme, d), lambda j, hf, te, nu: (j, 0)),
        scratch_shapes=[pltpu.VMEM((tme, d), F32)],
    )
    return pl.pallas_call(
        _expert_kernel,
        grid_spec=grid_spec,
        out_shape=jax.ShapeDtypeStruct((nr, d), F32),
        compiler_params=_cparams(("arbitrary", "arbitrary")),
        name="moe_experts",
    )(tile_expert, n_used, xs, wg, wu, wd)


def _combine_kernel(pos_ref, posn_ref, x_ref, mod_ref, p_ref, y_ref, o_ref, ybuf, sem):
    i = pl.program_id(0)
    last = pl.num_programs(0) - 1
    tm = x_ref.shape[0]
    buf = i % 2

    def gather(idx_ref, b):
        def start(r, c):
            for slot in range(2):
                pltpu.make_async_copy(y_ref.at[pl.ds(idx_ref[0, 0, slot * tm + r], 1), :],
                                      ybuf.at[b, slot, pl.ds(r, 1), :], sem.at[b]).start(priority=slot)
            return c

        lax.fori_loop(0, tm, start, 0, unroll=8)

    @pl.when(i == 0)
    def _():
        gather(pos_ref, buf)

    @pl.when(i < last)
    def _():
        gather(posn_ref, 1 - buf)

    for slot in range(2):
        pltpu.make_async_copy(y_ref.at[pl.ds(0, tm), :], ybuf.at[buf, slot], sem.at[buf]).wait()
    p = p_ref[...]
    f = p[:, 0:1] * ybuf[buf, 0] + p[:, 1:2] * ybuf[buf, 1]
    o_ref[...] = x_ref[...] + mod_ref[0, 5:6, :] * f


def combine_call(pos3, x, mod, p, y, *, seg, tm):
    t, d = x.shape
    nt = t // tm
    return pl.pallas_call(
        _combine_kernel,
        grid=(nt,),
        in_specs=[
            pl.BlockSpec((1, 1, 2 * tm), lambda i: (i, 0, 0), memory_space=pltpu.SMEM),
            pl.BlockSpec((1, 1, 2 * tm), lambda i: (jnp.minimum(i + 1, nt - 1), 0, 0),
                         memory_space=pltpu.SMEM),
            pl.BlockSpec((tm, d), lambda i: (i, 0)),
            pl.BlockSpec((1, SUBLANES, d), lambda i: (i * tm // seg, 0, 0)),
            pl.BlockSpec((tm, LANES), lambda i: (i, 0)),
            pl.BlockSpec(memory_space=pl.ANY),
        ],
        out_specs=pl.BlockSpec((tm, d), lambda i: (i, 0)),
        out_shape=jax.ShapeDtypeStruct((t, d), F32),
        scratch_shapes=[pltpu.VMEM((2, 2, tm, d), F32), pltpu.SemaphoreType.DMA((2,))],
        compiler_params=_cparams(("arbitrary",)),
        name="moe_combine",
    )(pos3, pos3, x, mod, p, y)


def router_operands(wr, tm):
    d = wr.shape[0]
    wr_pad = jnp.zeros((d, LANES), BF16).at[:, :N_EXPERTS].set(wr.astype(BF16))
    tri = jnp.asarray(np.tril(np.ones((tm, tm), np.float32), k=-1), BF16)
    return wr_pad, tri


def moe_block(x, routed, mod, nw, wg, wu, wd, *, seg, tmd, tme):
    t, d = x.shape
    oi, op, cnt = routed
    counts = cnt[0, :N_EXPERTS].astype(jnp.int32)
    padded = ((counts + tme - 1) // tme) * tme
    ends = jnp.cumsum(padded)
    starts = ends - padded
    nr = ((2 * t + N_EXPERTS * (tme - 1)) // tme) * tme
    ntiles = nr // tme
    n_used = (ends[-1] // tme).astype(jnp.int32).reshape(1)
    tile_start = jnp.arange(ntiles, dtype=jnp.int32) * tme
    tile_expert = jnp.minimum(jnp.sum(tile_start[:, None] >= ends[None, :], axis=1),
                              N_EXPERTS - 1).astype(jnp.int32)
    pos1 = starts[oi[:, 0]] + oi[:, 2]
    pos2 = starts[oi[:, 1]] + oi[:, 3]
    pos3 = jnp.concatenate([pos1.reshape(t // tmd, 1, tmd), pos2.reshape(t // tmd, 1, tmd)], axis=2)
    gap_lo = jnp.concatenate([starts + counts, ends[-1:]])
    gap_n = jnp.concatenate([padded - counts, nr - ends[-1:]])
    gaps3 = jnp.stack([gap_lo, gap_n], axis=1).reshape(1, 1, -1).astype(jnp.int32)
    xs = dispatch_call(pos3, gaps3, x, mod, nw, nr=nr, seg=seg, tm=tmd)
    y = expert_call(tile_expert, n_used, xs, wg, wu, wd, tme=tme)
    return combine_call(pos3, x, mod, op, y, seg=seg, tm=tmd)


def _final_kernel(x_ref, w_ref, o_ref):
    x = x_ref[...]
    ms = jnp.mean(x * x, axis=-1, keepdims=True)
    o_ref[...] = (x * lax.rsqrt(ms + NORM_EPS)) * w_ref[...]


def final_call(x, w, *, row0, nrows, tm):
    d = x.shape[1]
    off = row0 // tm
    return pl.pallas_call(
        _final_kernel,
        grid=(nrows // tm,),
        in_specs=[pl.BlockSpec((tm, d), lambda i: (i + off, 0)), pl.BlockSpec((1, d), lambda i: (0, 0))],
        out_specs=pl.BlockSpec((tm, d), lambda i: (i, 0)),
        out_shape=jax.ShapeDtypeStruct((nrows, d), F32),
        compiler_params=_cparams(("arbitrary",)),
        name="final_norm",
    )(x, w)


def _segment_flags(prompt_segs, sample_segs, seg, tile):
    per_seg = seg // tile
    nseg = prompt_segs + sample_segs
    first = np.zeros(nseg * per_seg, np.int32)
    last = np.zeros(nseg * per_seg, np.int32)
    first[0] = 1
    last[prompt_segs * per_seg - 1] = 1
    for s in range(prompt_segs, nseg):
        first[s * per_seg] = 1
        last[(s + 1) * per_seg - 1] = 1
    return first, last


def encoder_pair(x_prompt, x_sample, c_prompt, c_sample, w_ada, b_ada, norm_mix, norm_ffn, w_in, conv_w,
                 a_log, dt_bias, gdn_norm, w_out, w_ffn_gate, w_ffn_up, w_ffn_down, w_router,
                 w_exp_gate, w_exp_up, w_exp_down, norm_final, *, tm=512, tb=512, tmd=256, tme=512):
    bp, sp, d = x_prompt.shape
    bs, seg, _ = x_sample.shape
    assert bp == 1 and sp % seg == 0 and seg % tm == 0 and tm % DFT1 == 0
    depth = w_ada.shape[0]
    prompt_segs = sp // seg
    nseg = prompt_segs + bs
    t = nseg * seg
    x = jnp.concatenate([x_prompt.reshape(sp, d), x_sample.reshape(bs * seg, d)], axis=0)

    nrow = -(-(1 + bs) // SUBLANES) * SUBLANES
    c_all = jnp.zeros((nrow, d), F32).at[0:1].set(c_prompt).at[1:1 + bs].set(c_sample)
    ada = ada_call(c_all, w_ada, b_ada).reshape(depth, nrow, 6, d)
    seg_row = np.concatenate([np.zeros(prompt_segs, np.int32), 1 + np.arange(bs, dtype=np.int32)])
    mod_all = jnp.pad(ada[:, seg_row], ((0, 0), (0, 0), (0, SUBLANES - 6), (0, 0)))

    first_c, last_c = _segment_flags(prompt_segs, bs, seg, tm)
    first_g, last_g = _segment_flags(prompt_segs, bs, seg, tb)
    reset_f = jnp.asarray(first_g)
    reset_b = jnp.asarray(last_g[::-1].copy())
    first_c, last_c = jnp.asarray(first_c), jnp.asarray(last_c)

    off_f, off_qkv, off_z = F_DIM, F_DIM + QKV_DIM, F_DIM + QKV_DIM + G_DIM
    off_b = off_z + 2 * NH
    for l in range(depth):
        mod = mod_all[l]
        wl = w_in[l]
        wf = wl[:, :off_f].astype(BF16)
        wqkv = wl[:, off_f:off_qkv].astype(BF16)
        wz = wl[:, off_qkv:off_z].astype(BF16)
        wba = jnp.pad(wl[:, off_z:], ((0, 0), (0, LANES - 4 * NH))).astype(BF16)
        cw = jnp.pad(conv_w[l], ((0, SUBLANES - CONV_K), (0, 0)))
        gp = jnp.zeros((SUBLANES, LANES), F32)
        gp = gp.at[0, 2 * NH:4 * NH].set(dt_bias[l].reshape(-1)).at[1, 2 * NH:4 * NH].set(a_log[l].reshape(-1))
        f, z, q, k, v, gcol, grow, gend = inproj_call(first_c, last_c, x, mod, norm_mix[l].reshape(1, d),
                                                      wf, wqkv, wz, wba, cw, gp, seg=seg, tm=tm)
        o_f, o_b = gdn_call(reset_f, reset_b, q, k, v, gcol, grow, gend, tb=tb)

        s2p = sp // DFT1
        fm_p = fourier_mix_call(f[:sp], nseq=1, s=sp, tcol=min(2048, s2p * F_DIM),
                                tk=min(16, DFT1), nsplit=prompt_segs)
        s2s = seg // DFT1
        fm_s = fourier_mix_call(f[sp:], nseq=bs, s=seg, tcol=min(2048, s2s * F_DIM),
                                tk=min(64, DFT1), nsplit=1)
        fm2 = jnp.concatenate([fm_p, fm_s], axis=0).reshape(nseg * DFT1, (seg // DFT1) * F_DIM)

        nw = norm_ffn[l].reshape(1, d)
        i = l // 2
        gn = gdn_norm[l].reshape(1, HD)
        if l % 2 == 0:
            ffn = (nw, w_ffn_gate[i].astype(BF16), w_ffn_up[i].astype(BF16), w_ffn_down[i].astype(BF16))
            x = outproj_call(x, mod, o_f, o_b, z, fm2, gn, w_out[l].astype(BF16), ffn=ffn, seg=seg, tm=tm)
        else:
            x, *routed = outproj_call(x, mod, o_f, o_b, z, fm2, gn, w_out[l].astype(BF16),
                                      router=(nw,) + router_operands(w_router[i], tm), seg=seg, tm=tm)
            x = moe_block(x, routed, mod, nw, w_exp_gate[i].astype(BF16), w_exp_up[i].astype(BF16),
                          w_exp_down[i].astype(BF16), seg=seg, tmd=tmd, tme=tme)

    wn = norm_final.reshape(1, d)
    y_p = final_call(x, wn, row0=0, nrows=sp, tm=tm).reshape(bp, sp, d)
    y_s = final_call(x, wn, row0=sp, nrows=bs * seg, tm=tm).reshape(bs, seg, d)
    return y_p, y_s


def kernel(x_prompt, x_sample, c_prompt, c_sample, w_ada, b_ada, norm_mix, norm_ffn, w_in, conv_w, a_log,
           dt_bias, gdn_norm, w_out, w_ffn_gate, w_ffn_up, w_ffn_down, w_router, w_exp_gate, w_exp_up,
           w_exp_down, norm_final):
    return encoder_pair(x_prompt, x_sample, c_prompt, c_sample, w_ada, b_ada, norm_mix, norm_ffn, w_in,
                        conv_w, a_log, dt_bias, gdn_norm, w_out, w_ffn_gate, w_ffn_up, w_ffn_down,
                        w_router, w_exp_gate, w_exp_up, w_exp_down, norm_final)
```

```python
import functools
import math

import numpy as np
import jax
import jax.numpy as jnp
from jax import lax
from jax.experimental import pallas as pl
from jax.experimental.pallas import tpu as pltpu

F32 = jnp.float32
BF16 = jnp.bfloat16

D_MODEL = 1024
DEPTH = 4
N_GROUPS = 4
GROUP_DIM = 64
F_DIM = N_GROUPS * GROUP_DIM
HD = 128
NH = 6
G_DIM = NH * HD
QKV_DIM = 3 * G_DIM
CONV_K = 5
CHUNK = 64
PREP_CHUNKS = 2
D_FF = 2816
N_EXPERTS = 8
D_EXPERT = 3584
NORM_EPS = 1e-6

LANES = 128
SUBLANES = 8
MXU_COLS = 256
VMEM_LIMIT = 56 * 1024 * 1024
DFT1 = 128

GL_GAMMA, GL_BETA, GL_EG, GL_EGR = 0, 16, 32, 48
NEG_BIG = -1e30


def _cparams(sem):
    return pltpu.CompilerParams(dimension_semantics=sem, vmem_limit_bytes=VMEM_LIMIT)


def _mod_norm(x, nw, sc, sh):
    ms = jnp.mean(x * x, axis=-1, keepdims=True)
    y = x * lax.rsqrt(ms + NORM_EPS)
    return (y * nw) * (1.0 + sc) + sh


def _silu(x):
    hx = 0.5 * x
    return hx + hx * jnp.tanh(hx)


def _ada_kernel(c_ref, w_ref, b_ref, o_ref):
    c = _silu(c_ref[...])
    o_ref[0] = jnp.dot(c.astype(BF16), w_ref[0].astype(BF16), preferred_element_type=F32) + b_ref[0]


def ada_call(c_all, w_ada, b_ada):
    nrow = c_all.shape[0]
    depth, d, d6 = w_ada.shape
    tn = 1024
    return pl.pallas_call(
        _ada_kernel,
        grid=(depth, d6 // tn),
        in_specs=[
            pl.BlockSpec((nrow, d), lambda l, j: (0, 0)),
            pl.BlockSpec((1, d, tn), lambda l, j: (l, 0, j)),
            pl.BlockSpec((1, 1, tn), lambda l, j: (l, 0, j)),
        ],
        out_specs=pl.BlockSpec((1, nrow, tn), lambda l, j: (l, 0, j)),
        out_shape=jax.ShapeDtypeStruct((depth, nrow, d6), F32),
        compiler_params=_cparams(("arbitrary", "arbitrary")),
        name="ada",
    )(c_all, w_ada, b_ada.reshape(depth, 1, d6))


def _inproj_kernel(first_ref, last_ref,
                   x_ref, xp_ref, xn_ref, mod_ref, nw_ref, wf_ref, wqkv_ref, wz_ref, wba_ref, cw_ref, gp_ref,
                   f_ref, z_ref, q_ref, k_ref, v_ref, gcol_ref, grow_ref, gend_ref,
                   xe_ref):
    i = pl.program_id(0)
    tm = x_ref.shape[0]
    nc = tm // CHUNK
    halo = SUBLANES
    xcat = jnp.concatenate([xp_ref[...], x_ref[...], xn_ref[...]], axis=0)
    h = _mod_norm(xcat, nw_ref[...], mod_ref[0, 1:2, :], mod_ref[0, 0:1, :]).astype(BF16)
    pm = jnp.where(first_ref[i] == 1, 0.0, 1.0)
    nm = jnp.where(last_ref[i] == 1, 0.0, 1.0)
    outs = (q_ref, k_ref, v_ref)

    def side_work(cb):
        nz = G_DIM // MXU_COLS
        last = QKV_DIM // MXU_COLS - 1
        if cb == 0:
            ba = jnp.dot(h, wba_ref[...], preferred_element_type=F32)[halo:halo + tm]
            _gates(ba, gp_ref, gcol_ref, grow_ref, gend_ref)
        elif last - nz <= cb < last:
            z0 = (cb - (last - nz)) * MXU_COLS
            z_ref[:, z0:z0 + MXU_COLS] = jnp.dot(h, wz_ref[:, z0:z0 + MXU_COLS], preferred_element_type=F32)[
                halo:halo + tm].astype(z_ref.dtype)
        elif cb == last - nz - 1:
            f_ref[...] = jnp.dot(h, wf_ref[...], preferred_element_type=F32)[halo:halo + tm].astype(f_ref.dtype)

    for cb in range(QKV_DIM // MXU_COLS):
        c0 = cb * MXU_COLS
        res = jnp.dot(h, wqkv_ref[:, c0:c0 + MXU_COLS], preferred_element_type=F32)
        xe_ref[0:halo, c0:c0 + MXU_COLS] = res[0:halo] * pm
        xe_ref[halo:halo + tm, c0:c0 + MXU_COLS] = res[halo:halo + tm]
        xe_ref[halo + tm:, c0:c0 + MXU_COLS] = res[halo + tm:] * nm
        for s in range(cb * (MXU_COLS // HD), (cb + 1) * (MXU_COLS // HD)):
            lo = s * HD
            xs = xe_ref[:, lo:lo + HD]
            acc = None
            for j in range(CONV_K):
                shift = (CONV_K // 2 - j) % (tm + 2 * halo)
                xj = xs if shift == 0 else pltpu.roll(xs, shift, axis=0)
                term = xj[halo:halo + tm] * cw_ref[j:j + 1, lo:lo + HD]
                acc = term if acc is None else acc + term
            y = _silu(acc)
            which, head = divmod(s, NH)
            if which < 2:
                y = y * lax.rsqrt(jnp.sum(y * y, axis=-1, keepdims=True) + 1e-6)
            if which == 0:
                y = y * (HD ** -0.5)
            outs[which][:, head * HD:(head + 1) * HD] = y
        side_work(cb)


def _gates(ba, gp_ref, gcol_ref, grow_ref, gend_ref):
    tm = ba.shape[0]
    nc = tm // CHUNK
    lane = lax.broadcasted_iota(jnp.int32, (tm, LANES), 1)
    beta = 1.0 / (1.0 + jnp.exp(-ba))
    xs = ba + gp_ref[0:1, :]
    softplus = jnp.maximum(xs, 0.0) + jnp.log(1.0 + jnp.exp(-jnp.abs(xs)))
    g = -jnp.exp(gp_ref[1:2, :]) * softplus
    g = jnp.where((lane >= 2 * NH) & (lane < 4 * NH), g, 0.0)
    g = pltpu.roll(g, LANES - 2 * NH, axis=1)

    rowc = lax.broadcasted_iota(jnp.int32, (tm, LANES), 0) % CHUNK
    p = g
    sh = 1
    while sh < CHUNK:
        p = p + jnp.where(rowc >= sh, pltpu.roll(p, sh, axis=0), 0.0)
        sh *= 2
    g3 = g.reshape(nc, CHUNK, LANES)
    tot = jnp.broadcast_to(jnp.sum(g3, axis=1, keepdims=True), (nc, CHUNK, LANES)).reshape(tm, LANES)
    is_bwd = (lane >= NH) & (lane < 2 * NH)
    gamma = jnp.where(is_bwd, tot - p + g, p)
    eg = jnp.exp(gamma)
    egr = jnp.exp(tot - gamma)
    m12 = lane < 2 * NH
    gcol = (jnp.where(m12, gamma, 0.0)
            + pltpu.roll(jnp.where(m12, beta, 0.0), GL_BETA, axis=1)
            + pltpu.roll(jnp.where(m12, eg, 0.0), GL_EG, axis=1)
            + pltpu.roll(jnp.where(m12, egr, 0.0), GL_EGR, axis=1))
    gcol_ref[...] = gcol
    gam_t = jnp.where(m12, gamma, 0.0).T[0:2 * SUBLANES, :]
    g_t = g.T[0:2 * SUBLANES, :]
    for c in range(nc):
        grow_ref[c] = gam_t[:, c * CHUNK:(c + 1) * CHUNK]
        tc = jnp.sum(g_t[:, c * CHUNK:(c + 1) * CHUNK], axis=-1, keepdims=True)
        gend_ref[c] = jnp.exp(jnp.broadcast_to(tc, (2 * SUBLANES, LANES)))


def inproj_call(first, last, x, mod, nw, wf, wqkv, wz, wba, cw, gp, *, seg, tm):
    t, d = x.shape
    nblk8 = t // SUBLANES
    r8 = tm // SUBLANES
    nc = tm // CHUNK
    row = lambda i, f, l: (i, 0)
    const = lambda i, f, l: (0, 0)
    chunked = lambda i, f, l: (i, 0, 0)
    single = pl.Buffered(1)
    grid_spec = pltpu.PrefetchScalarGridSpec(
        num_scalar_prefetch=2,
        grid=(t // tm,),
        in_specs=[
            pl.BlockSpec((tm, d), row),
            pl.BlockSpec((SUBLANES, d), lambda i, f, l: (jnp.maximum(i * r8 - 1, 0), 0)),
            pl.BlockSpec((SUBLANES, d), lambda i, f, l: (jnp.minimum((i + 1) * r8, nblk8 - 1), 0)),
            pl.BlockSpec((1, SUBLANES, d), lambda i, f, l: (i * tm // seg, 0, 0)),
            pl.BlockSpec((1, d), const),
            pl.BlockSpec(wf.shape, const, pipeline_mode=single),
            pl.BlockSpec(wqkv.shape, const, pipeline_mode=single),
            pl.BlockSpec(wz.shape, const, pipeline_mode=single),
            pl.BlockSpec(wba.shape, const, pipeline_mode=single),
            pl.BlockSpec((SUBLANES, QKV_DIM), const),
            pl.BlockSpec((SUBLANES, LANES), const),
        ],
        out_specs=[
            pl.BlockSpec((tm, F_DIM), row),
            pl.BlockSpec((tm, G_DIM), row),
            pl.BlockSpec((tm, G_DIM), row),
            pl.BlockSpec((tm, G_DIM), row),
            pl.BlockSpec((tm, G_DIM), row),
            pl.BlockSpec((tm, LANES), row),
            pl.BlockSpec((nc, 2 * SUBLANES, CHUNK), chunked),
            pl.BlockSpec((nc, 2 * SUBLANES, LANES), chunked),
        ],
        scratch_shapes=[pltpu.VMEM((tm + 2 * SUBLANES, QKV_DIM), F32)],
    )
    return pl.pallas_call(
        _inproj_kernel,
        grid_spec=grid_spec,
        out_shape=[
            jax.ShapeDtypeStruct((t, F_DIM), BF16),
            jax.ShapeDtypeStruct((t, G_DIM), BF16),
            jax.ShapeDtypeStruct((t, G_DIM), F32),
            jax.ShapeDtypeStruct((t, G_DIM), F32),
            jax.ShapeDtypeStruct((t, G_DIM), F32),
            jax.ShapeDtypeStruct((t, LANES), F32),
            jax.ShapeDtypeStruct((t // CHUNK, 2 * SUBLANES, CHUNK), F32),
            jax.ShapeDtypeStruct((t // CHUNK, 2 * SUBLANES, LANES), F32),
        ],
        compiler_params=_cparams(("arbitrary",)),
        name="inproj_conv",
    )(first, last, x, x, x, mod, nw, wf, wqkv, wz, wba, cw, gp)


def _gdn_kernel(rf_ref, rb_ref,
                qf_ref, kf_ref, vf_ref, gcf_ref, grf_ref, gef_ref,
                qb_ref, kb_ref, vb_ref, gcb_ref, grb_ref, geb_ref,
                of_ref, ob_ref, s_ref, wq_ref, ab_ref, u_ref):
    i = pl.program_id(0)
    tb = qf_ref.shape[0]
    nc = tb // CHUNK

    @pl.when(rf_ref[i] == 1)
    def _():
        s_ref[0:NH] = jnp.zeros((NH, HD, HD), F32)

    @pl.when(rb_ref[i] == 1)
    def _():
        s_ref[NH:2 * NH] = jnp.zeros((NH, HD, HD), F32)

    row = lax.broadcasted_iota(jnp.int32, (CHUNK, CHUNK), 0)
    col = lax.broadcasted_iota(jnp.int32, (CHUNK, CHUNK), 1)
    eye = jnp.where(row == col, 1.0, 0.0).astype(F32)
    level_masks = []
    b = 1
    while b < CHUNK:
        level_masks.append((row // (2 * b) == col // (2 * b)) & (row // b != col // b))
        b *= 2
    dirs = (
        (qf_ref, kf_ref, vf_ref, gcf_ref, grf_ref, gef_ref, of_ref, row >= col, row > col),
        (qb_ref, kb_ref, vb_ref, gcb_ref, grb_ref, geb_ref, ob_ref, row <= col, row < col),
    )

    hds = [(d, h) for d in range(2) for h in range(NH)]
    probs = [(u, d, h) for u in range(PREP_CHUNKS) for d, h in hds]
    npr = len(probs)

    def prep_step(it, carry):
        cs = [[it * PREP_CHUNKS + u for u in range(PREP_CHUNKS)],
              [nc - 1 - (it * PREP_CHUNKS + u) for u in range(PREP_CHUNKS)]]
        qs, ks, vs, cols, decs = [], [], [], [], []
        for u, d, h in probs:
            r0 = pl.multiple_of(cs[d][u] * CHUNK, CHUNK)
            q_ref, k_ref, v_ref, gc_ref, gr_ref = dirs[d][0:5]
            hd = d * NH + h
            lo = h * HD
            gc = gc_ref[pl.ds(r0, CHUNK), :]
            qs.append(q_ref[pl.ds(r0, CHUNK), lo:lo + HD])
            ks.append(k_ref[pl.ds(r0, CHUNK), lo:lo + HD])
            vs.append(v_ref[pl.ds(r0, CHUNK), lo:lo + HD])
            gam_c = gc[:, GL_GAMMA + hd:GL_GAMMA + hd + 1]
            cols.append((gc[:, GL_BETA + hd:GL_BETA + hd + 1], gc[:, GL_EG + hd:GL_EG + hd + 1],
                         gc[:, GL_EGR + hd:GL_EGR + hd + 1]))
            gam_r = gr_ref[cs[d][u]][hd:hd + 1, :]
            decs.append(jnp.exp(jnp.where(dirs[d][7], gam_c - gam_r, NEG_BIG)))
        kqs = [lax.dot_general(jnp.concatenate([ks[n], qs[n]], axis=0), ks[n],
                               (((1,), (1,)), ((), ())), preferred_element_type=F32)
               for n in range(npr)]
        a_s = [jnp.where(dirs[d][8], kqs[n][0:CHUNK] * cols[n][0] * decs[n], 0.0)
               for n, (u, d, h) in enumerate(probs)]
        for n, (u, d, h) in enumerate(probs):
            ab_ref[d, cs[d][u], h, 0:CHUNK, :] = kqs[n][CHUNK:] * decs[n]
            ab_ref[d, cs[d][u], h, CHUNK:, :] = (ks[n] * cols[n][2]).T
        ts = [eye - jnp.where(level_masks[0], a_s[n], 0.0) for n in range(npr)]
        for lm in level_masks[1:]:
            lts = [jnp.dot(jnp.where(lm, a_s[n], 0.0), ts[n], preferred_element_type=F32) for n in range(npr)]
            ts = [ts[n] - jnp.dot(ts[n], lts[n], preferred_element_type=F32) for n in range(npr)]
        uws =[jnp.dot(ts[n], jnp.concatenate([vs[n] * cols[n][0], ks[n] * (cols[n][0] * cols[n][1])], axis=1),
                       preferred_element_type=F32) for n in range(npr)]
        for n, (u, d, h) in enumerate(probs):
            u_ref[d, cs[d][u], h] = uws[n][:, 0:HD]
            wq_ref[d, cs[d][u], h, 0:CHUNK, :] = uws[n][:, HD:]
            wq_ref[d, cs[d][u], h, CHUNK:, :] = qs[n] * cols[n][1]
        return carry


    def scan_step(c, carry):
        ccs = (c, nc - 1 - c)
        sts = [s_ref[d * NH + h] for d, h in hds]
        wqs = [jnp.dot(wq_ref[d, ccs[d], h], sts[n], preferred_element_type=F32)
               for n, (d, h) in enumerate(hds)]
        vns = [u_ref[d, ccs[d], h] - wqs[n][0:CHUNK] for n, (d, h) in enumerate(hds)]
        avs = [jnp.dot(ab_ref[d, ccs[d], h], vns[n], preferred_element_type=F32)
               for n, (d, h) in enumerate(hds)]
        for n, (d, h) in enumerate(hds):
            r0 = pl.multiple_of(ccs[d] * CHUNK, CHUNK)
            dirs[d][6][pl.ds(r0, CHUNK), h * HD:(h + 1) * HD] = (
                wqs[n][CHUNK:] + avs[n][0:CHUNK]).astype(dirs[d][6].dtype)
            ge = dirs[d][5][ccs[d]]
            hd = d * NH + h
            s_ref[hd] = sts[n] * ge[hd:hd + 1, :] + avs[n][CHUNK:]
        return carry

    nit = nc // PREP_CHUNKS

    def scan_group(it):
        for u in range(PREP_CHUNKS):
            scan_step(it * PREP_CHUNKS + u, 0)

    def merged_step(it, carry):
        scan_group(it - 1)
        prep_step(it, 0)
        return carry

    prep_step(0, 0)
    lax.fori_loop(1, nit, merged_step, 0)
    scan_group(nit - 1)


def gdn_call(reset_f, reset_b, q, k, v, gcol, grow, gend, *, tb):
    t = q.shape[0]
    nb = t // tb
    nc = tb // CHUNK
    fwd2 = lambda i, a, b: (i, 0)
    bwd2 = lambda i, a, b: (nb - 1 - i, 0)
    fwd3 = lambda i, a, b: (i, 0, 0)
    bwd3 = lambda i, a, b: (nb - 1 - i, 0, 0)

    def specs(m2, m3):
        return [
            pl.BlockSpec((tb, G_DIM), m2), pl.BlockSpec((tb, G_DIM), m2), pl.BlockSpec((tb, G_DIM), m2),
            pl.BlockSpec((tb, LANES), m2),
            pl.BlockSpec((nc, 2 * SUBLANES, CHUNK), m3),
            pl.BlockSpec((nc, 2 * SUBLANES, LANES), m3),
        ]

    grid_spec = pltpu.PrefetchScalarGridSpec(
        num_scalar_prefetch=2,
        grid=(nb,),
        in_specs=specs(fwd2, fwd3) + specs(bwd2, bwd3),
        out_specs=[pl.BlockSpec((tb, G_DIM), fwd2), pl.BlockSpec((tb, G_DIM), bwd2)],
        scratch_shapes=[pltpu.VMEM((2 * NH, HD, HD), F32),
                        pltpu.VMEM((2, nc, NH, 2 * CHUNK, HD), F32),
                        pltpu.VMEM((2, nc, NH, CHUNK + HD, CHUNK), F32),
                        pltpu.VMEM((2, nc, NH, CHUNK, HD), F32)],
    )
    return pl.pallas_call(
        _gdn_kernel,
        grid_spec=grid_spec,
        out_shape=[jax.ShapeDtypeStruct((t, G_DIM), BF16), jax.ShapeDtypeStruct((t, G_DIM), BF16)],
        compiler_params=_cparams(("arbitrary",)),
        name="gdn",
    )(reset_f, reset_b, q, k, v, gcol, grow, gend, q, k, v, gcol, grow, gend)


def _fft1_kernel(x_ref, m1_ref, tc_ref, ts_ref, br_ref, bi_ref):
    s1 = x_ref.shape[1]
    a = jnp.dot(m1_ref[...], x_ref[0], preferred_element_type=F32)
    ar, ai = a[0:s1], a[s1:]
    tc, ts = tc_ref[...], ts_ref[...]
    br_ref[0] = (ar * tc + ai * ts).astype(br_ref.dtype)
    bi_ref[0] = (ai * tc - ar * ts).astype(bi_ref.dtype)


def fft1_call(x3, m1, twc, tws, *, tcol):
    nseq, s1, cols = x3.shape
    blk = pl.BlockSpec((1, s1, tcol), lambda j, b: (b, 0, j))
    tw = pl.BlockSpec((s1, tcol), lambda j, b: (0, j))
    return pl.pallas_call(
        _fft1_kernel,
        grid=(cols // tcol, nseq),
        in_specs=[blk, pl.BlockSpec(m1.shape, lambda j, b: (0, 0)), tw, tw],
        out_specs=[blk, blk],
        out_shape=[jax.ShapeDtypeStruct(x3.shape, BF16)] * 2,
        compiler_params=_cparams(("arbitrary", "arbitrary")),
        name="fft_stage1",
    )(x3, m1, twc, tws)


def _fft2_kernel(br_ref, bi_ref, mc_ref, m2_ref, o_ref):
    _, tk, s2, c = br_ref.shape
    nsplit = o_ref.shape[0]
    s2o = s2 // nsplit
    b = jnp.concatenate([br_ref[0].reshape(tk * s2, c), bi_ref[0].reshape(tk * s2, c)], axis=1)
    z = jnp.dot(b, mc_ref[...], preferred_element_type=F32)
    m2 = m2_ref[...]
    for kk in range(tk):
        zk = z[kk * s2:(kk + 1) * s2]
        x = jnp.dot(m2, jnp.concatenate([zk[:, 0:c], zk[:, c:]], axis=0),
                    preferred_element_type=F32)
        for sp in range(nsplit):
            o_ref[sp, kk] = x[sp * s2o:(sp + 1) * s2o].astype(o_ref.dtype)


def fft2_call(br4, bi4, mc, m2, *, tk, nsplit):
    nseq, s1, s2, c = br4.shape
    s2o = s2 // nsplit
    blk = pl.BlockSpec((1, tk, s2, c), lambda b, j: (b, j, 0, 0))
    return pl.pallas_call(
        _fft2_kernel,
        grid=(nseq, s1 // tk),
        in_specs=[blk, blk, pl.BlockSpec(mc.shape, lambda b, j: (0, 0)),
                  pl.BlockSpec(m2.shape, lambda b, j: (0, 0))],
        out_specs=pl.BlockSpec((nsplit, tk, s2o, c), lambda b, j: (b, j, 0, 0)),
        out_shape=jax.ShapeDtypeStruct((nseq * nsplit, s1, s2o, c), BF16),
        compiler_params=_cparams(("arbitrary", "arbitrary")),
        name="fft_stage2",
    )(br4, bi4, mc, m2)


def _dft_tables(s):
    s1 = DFT1
    s2 = s // s1
    k = np.arange(s1)
    ang1 = 2.0 * np.pi * ((k[:, None] * k[None, :]) % s1) / s1
    sc = 1.0 / math.sqrt(s)
    m1 = np.concatenate([np.cos(ang1), -np.sin(ang1)], axis=0) * sc
    n2 = np.arange(s2)
    angt = 2.0 * np.pi * ((k[:, None] * n2[None, :]) % s) / s
    twc = np.repeat(np.cos(angt), F_DIM, axis=1)
    tws = np.repeat(np.sin(angt), F_DIM, axis=1)
    ang2 = 2.0 * np.pi * ((n2[:, None] * n2[None, :]) % s2) / s2
    m2 = np.concatenate([np.cos(ang2), np.sin(ang2)], axis=1)
    return (jnp.asarray(m1, F32), jnp.asarray(np.cos(angt), F32), jnp.asarray(np.sin(angt), F32),
            jnp.asarray(m2, F32))


def _channel_dft_matrix():
    g = np.arange(GROUP_DIM)
    ang = 2.0 * np.pi * ((g[:, None] * g[None, :]) % GROUP_DIM) / GROUP_DIM
    cg = np.kron(np.eye(N_GROUPS), np.cos(ang)) / math.sqrt(GROUP_DIM)
    sg = np.kron(np.eye(N_GROUPS), np.sin(ang)) / math.sqrt(GROUP_DIM)
    return jnp.asarray(np.block([[cg, -sg], [sg, cg]]), F32)


def fourier_mix_call(f, *, nseq, s, tcol, tk, nsplit):
    s1 = DFT1
    s2 = s // s1
    m1, tcs, tss, m2 = _dft_tables(s)
    twc = jnp.broadcast_to(tcs[:, :, None], (s1, s2, F_DIM)).reshape(s1, s2 * F_DIM)
    tws = jnp.broadcast_to(tss[:, :, None], (s1, s2, F_DIM)).reshape(s1, s2 * F_DIM)
    x3 = f.reshape(nseq, s1, s2 * F_DIM)
    br, bi = fft1_call(x3, m1.astype(BF16), twc, tws, tcol=tcol)
    out = fft2_call(br.reshape(nseq, s1, s2, F_DIM), bi.reshape(nseq, s1, s2, F_DIM),
                    _channel_dft_matrix().astype(BF16), m2, tk=tk, nsplit=nsplit)
    return out.reshape(nseq * nsplit, s1, (s2 // nsplit) * F_DIM)


def _mixer_out(x_ref, mod_ref, of_ref, ob_ref, z_ref, fm_ref, gn_ref, w_ref):
    tm = x_ref.shape[0]
    o = of_ref[...].astype(F32) + ob_ref[...].astype(F32)
    z = z_ref[...].astype(F32)
    gn = gn_ref[...]
    parts = []
    fm = fm_ref[...]
    parts.append(jnp.concatenate(
        [fm[:, j * F_DIM:(j + 1) * F_DIM] for j in range(tm // DFT1)], axis=0).astype(BF16))
    for h in range(NH):
        oh = o[:, h * HD:(h + 1) * HD]
        ms = jnp.mean(oh * oh, axis=-1, keepdims=True)
        y = (oh * lax.rsqrt(ms + NORM_EPS)) * gn
        parts.append((y * _silu(z[:, h * HD:(h + 1) * HD])).astype(BF16))
    mixed = jnp.concatenate(parts, axis=1)
    proj = jnp.dot(mixed, w_ref[...], preferred_element_type=F32)
    return x_ref[...] + mod_ref[0, 2:3, :] * proj


def _swiglu_residual(x, mod_ref, nw_ref, wg_ref, wu_ref, wd_ref, nsplit):
    h = _mod_norm(x, nw_ref[...], mod_ref[0, 4:5, :], mod_ref[0, 3:4, :]).astype(BF16)
    cw = wg_ref.shape[1] // nsplit
    acc = None
    for c in range(nsplit):
        g = jnp.dot(h, wg_ref[:, c * cw:(c + 1) * cw], preferred_element_type=F32)
        u = jnp.dot(h, wu_ref[:, c * cw:(c + 1) * cw], preferred_element_type=F32)
        hid = (_silu(g) * u).astype(BF16)
        part = jnp.dot(hid, wd_ref[c * cw:(c + 1) * cw, :], preferred_element_type=F32)
        acc = part if acc is None else acc + part
    return x + mod_ref[0, 5:6, :] * acc


def _outproj_kernel(x_ref, mod_ref, of_ref, ob_ref, z_ref, fm_ref, gn_ref, w_ref, o_ref):
    o_ref[...] = _mixer_out(x_ref, mod_ref, of_ref, ob_ref, z_ref, fm_ref, gn_ref, w_ref)


def _outproj_ffn_kernel(x_ref, mod_ref, of_ref, ob_ref, z_ref, fm_ref, gn_ref, w_ref,
                        nw_ref, wg_ref, wu_ref, wd_ref, o_ref, *, nsplit):
    x_mid = _mixer_out(x_ref, mod_ref, of_ref, ob_ref, z_ref, fm_ref, gn_ref, w_ref)
    o_ref[...] = _swiglu_residual(x_mid, mod_ref, nw_ref, wg_ref, wu_ref, wd_ref, nsplit)


def _outproj_router_kernel(x_ref, mod_ref, of_ref, ob_ref, z_ref, fm_ref, gn_ref, w_ref,
                           nw_ref, wr_ref, tri_ref, o_ref, oi_ref, op_ref, cnt_ref, carry_ref):
    x_mid = _mixer_out(x_ref, mod_ref, of_ref, ob_ref, z_ref, fm_ref, gn_ref, w_ref)
    o_ref[...] = x_mid
    _route(x_mid, mod_ref, nw_ref, wr_ref, tri_ref, oi_ref, op_ref, cnt_ref, carry_ref)


def outproj_call(x, mod, o_f, o_b, z, fm2, gn, w, ffn=None, router=None, *, seg, tm):
    t, d = x.shape
    row = lambda i: (i, 0)
    const = lambda i: (0, 0)
    per_seg = seg // tm
    single = pl.Buffered(1)
    in_specs = [
        pl.BlockSpec((tm, d), row),
        pl.BlockSpec((1, SUBLANES, d), lambda i: (i // per_seg, 0, 0)),
        pl.BlockSpec((tm, G_DIM), row),
        pl.BlockSpec((tm, G_DIM), row),
        pl.BlockSpec((tm, G_DIM), row),
        pl.BlockSpec((DFT1, (tm // DFT1) * F_DIM), lambda i: (i // per_seg, i % per_seg)),
        pl.BlockSpec((1, HD), const),
        pl.BlockSpec(w.shape, const, pipeline_mode=single),
    ]
    args = [x, mod, o_f, o_b, z, fm2, gn, w]
    body = _outproj_kernel
    if ffn is not None:
        nw, wg, wu, wd = ffn
        in_specs += [pl.BlockSpec((1, d), const)] + [
            pl.BlockSpec(a.shape, const, pipeline_mode=single) for a in (wg, wu, wd)]
        args += [nw, wg, wu, wd]
        body = functools.partial(_outproj_ffn_kernel, nsplit=wg.shape[1] // MXU_COLS)
    out_specs = pl.BlockSpec((tm, d), row)
    out_shape = jax.ShapeDtypeStruct((t, d), F32)
    scratch = []
    name = "outproj" if ffn is None else "outproj_ffn"
    if router is not None:
        nw, wr, tri = router
        in_specs += [pl.BlockSpec((1, d), const), pl.BlockSpec(wr.shape, const), pl.BlockSpec(tri.shape, const)]
        args += [nw, wr, tri]
        body = _outproj_router_kernel
        out_specs = [out_specs, pl.BlockSpec((tm, LANES), row), pl.BlockSpec((tm, LANES), row),
                     pl.BlockSpec((SUBLANES, LANES), const)]
        out_shape = [out_shape, jax.ShapeDtypeStruct((t, LANES), jnp.int32),
                     jax.ShapeDtypeStruct((t, LANES), F32), jax.ShapeDtypeStruct((SUBLANES, LANES), F32)]
        scratch = [pltpu.VMEM((SUBLANES, LANES), F32)]
        name = "outproj_router"
    return pl.pallas_call(
        body,
        grid=(t // tm,),
        in_specs=in_specs,
        out_specs=out_specs,
        out_shape=out_shape,
        scratch_shapes=scratch,
        compiler_params=_cparams(("arbitrary",)),
        name=name,
    )(*args)


def _route(x, mod_ref, nw_ref, wr_ref, tri_ref, oi_ref, op_ref, cnt_ref, carry_ref):
    i = pl.program_id(0)
    tm = x.shape[0]

    @pl.when(i == 0)
    def _():
        carry_ref[...] = jnp.zeros_like(carry_ref)

    h = _mod_norm(x, nw_ref[...], mod_ref[0, 4:5, :], mod_ref[0, 3:4, :])
    logits = jnp.dot(h.astype(BF16), wr_ref[...], preferred_element_type=F32)
    lane = lax.broadcasted_iota(jnp.int32, (tm, LANES), 1)
    logits = jnp.where(lane < N_EXPERTS, logits, NEG_BIG)
    l1 = jnp.max(logits, axis=-1, keepdims=True)
    i1 = jnp.min(jnp.where(logits == l1, lane, LANES), axis=-1, keepdims=True)
    rest = jnp.where(lane == i1, NEG_BIG, logits)
    l2 = jnp.max(rest, axis=-1, keepdims=True)
    i2 = jnp.min(jnp.where(rest == l2, lane, LANES), axis=-1, keepdims=True)
    e21 = jnp.exp(l2 - l1)
    p1 = 1.0 / (1.0 + e21)
    p2 = e21 * p1
    oh1 = lane == i1
    oh2 = lane == i2
    oh = jnp.where(oh1 | oh2, 1.0, 0.0).astype(BF16)
    before = jnp.dot(tri_ref[...], oh, preferred_element_type=F32) + carry_ref[0:1, :]
    r1 = jnp.sum(jnp.where(oh1, before, 0.0), axis=-1, keepdims=True).astype(jnp.int32)
    r2 = jnp.sum(jnp.where(oh2, before, 0.0), axis=-1, keepdims=True).astype(jnp.int32)
    oi_ref[...] = jnp.where(lane == 0, i1, jnp.where(lane == 1, i2, jnp.where(lane == 2, r1, r2)))
    op_ref[...] = jnp.where(lane == 0, p1, p2)
    new_carry = carry_ref[0:1, :] + jnp.sum(oh.astype(F32), axis=0, keepdims=True)
    carry_ref[...] = jnp.broadcast_to(new_carry, carry_ref.shape)
    cnt_ref[...] = jnp.broadcast_to(new_carry, cnt_ref.shape)


def _dispatch_kernel(pos_ref, gaps_ref, x_ref, mod_ref, nw_ref, xs_ref, hbuf, zrow, sem, zsem):
    i = pl.program_id(0)
    last = pl.num_programs(0) - 1
    tm = x_ref.shape[0]
    buf = i % 2

    @pl.when(i == 0)
    def _():
        zrow[...] = jnp.zeros_like(zrow)

        def zero_copy(row):
            return pltpu.make_async_copy(zrow.at[pl.ds(0, 1), :], xs_ref.at[pl.ds(row, 1), :], zsem)

        for g in range(gaps_ref.shape[2] // 2):
            lo = gaps_ref[0, 0, 2 * g]
            n = gaps_ref[0, 0, 2 * g + 1]
            lax.fori_loop(0, n, lambda r, c: (zero_copy(lo + r).start(), c)[1], 0)
        for g in range(gaps_ref.shape[2] // 2):
            lo = gaps_ref[0, 0, 2 * g]
            n = gaps_ref[0, 0, 2 * g + 1]
            lax.fori_loop(0, n, lambda r, c: (zero_copy(lo + r).wait(), c)[1], 0)

    hbuf[buf] = _mod_norm(x_ref[...], nw_ref[...], mod_ref[0, 4:5, :], mod_ref[0, 3:4, :])

    def start(r, c):
        for slot in range(2):
            pltpu.make_async_copy(hbuf.at[buf, pl.ds(r, 1), :],
                                  xs_ref.at[pl.ds(pos_ref[0, 0, slot * tm + r], 1), :],
                                  sem.at[buf]).start(priority=slot)
        return c

    lax.fori_loop(0, tm, start, 0, unroll=8)

    def wait_rows(b):
        for _ in range(2):
            pltpu.make_async_copy(hbuf.at[b], xs_ref.at[pl.ds(0, tm), :], sem.at[b]).wait()

    @pl.when(i > 0)
    def _():
        wait_rows(1 - buf)

    @pl.when(i == last)
    def _():
        wait_rows(buf)


def dispatch_call(pos3, gaps3, x, mod, nw, *, nr, seg, tm):
    t, d = x.shape
    return pl.pallas_call(
        _dispatch_kernel,
        grid=(t // tm,),
        in_specs=[
            pl.BlockSpec((1, 1, 2 * tm), lambda i: (i, 0, 0), memory_space=pltpu.SMEM),
            pl.BlockSpec(gaps3.shape, lambda i: (0, 0, 0), memory_space=pltpu.SMEM),
            pl.BlockSpec((tm, d), lambda i: (i, 0)),
            pl.BlockSpec((1, SUBLANES, d), lambda i: (i * tm // seg, 0, 0)),
            pl.BlockSpec((1, d), lambda i: (0, 0)),
        ],
        out_specs=pl.BlockSpec(memory_space=pl.ANY),
        out_shape=jax.ShapeDtypeStruct((nr, d), F32),
        scratch_shapes=[pltpu.VMEM((2, tm, d), F32), pltpu.VMEM((SUBLANES, d), F32),
                        pltpu.SemaphoreType.DMA((2,)), pltpu.SemaphoreType.DMA(())],
        compiler_params=_cparams(("arbitrary",)),
        name="moe_dispatch",
    )(pos3, gaps3, x, mod, nw)


def _expert_kernel(te_ref, nu_ref, xs_ref, wg_ref, wu_ref, wd_ref, y_ref, acc_ref):
    j = pl.program_id(0)
    half = pl.program_id(1)

    @pl.when(j < nu_ref[0])
    def _():
        h = xs_ref[...].astype(BF16)
        part = None
        for c0 in range(0, wg_ref.shape[2], MXU_COLS):
            g = jnp.dot(h, wg_ref[0, :, c0:c0 + MXU_COLS], preferred_element_type=F32)
            u = jnp.dot(h, wu_ref[0, :, c0:c0 + MXU_COLS], preferred_element_type=F32)
            hid = (_silu(g) * u).astype(BF16)
            down = jnp.dot(hid, wd_ref[0, c0:c0 + MXU_COLS, :], preferred_element_type=F32)
            part = down if part is None else part + down

        @pl.when(half == 0)
        def _():
            acc_ref[...] = part

        @pl.when(half == 1)
        def _():
            y_ref[...] = acc_ref[...] + part

    @pl.when((j >= nu_ref[0]) & (half == 1))
    def _():
        y_ref[...] = jnp.zeros_like(y_ref)


def expert_call(tile_expert, n_used, xs, wg, wu, wd, *, tme):
    nr, d = xs.shape
    ntiles = nr // tme
    fh = wg.shape[2] // 2

    def jj(j, nu):
        return jnp.minimum(j, nu[0] - 1)

    def hh(j, hf, nu):
        return jnp.where(j < nu[0], hf, 1)

    grid_spec = pltpu.PrefetchScalarGridSpec(
        num_scalar_prefetch=2,
        grid=(ntiles, 2),
        in_specs=[
            pl.BlockSpec((tme, d), lambda j, hf, te, nu: (jj(j, nu), 0)),
            pl.BlockSpec((1, d, fh), lambda j, hf, te, nu: (te[jj(j, nu)], 0, hh(j, hf, nu))),
            pl.BlockSpec((1, d, fh), lambda j, hf, te, nu: (te[jj(j, nu)], 0, hh(j, hf, nu))),
            pl.BlockSpec((1, fh, d), lambda j, hf, te, nu: (te[jj(j, nu)], hh(j, hf, nu), 0)),
        ],
        out_specs=pl.BlockSpec((tme, d), lambda j, hf, te, nu: (j, 0)),
        scratch_shapes=[pltpu.VMEM((tme, d), F32)],
    )
    return pl.pallas_call(
        _expert_kernel,
        grid_spec=grid_spec,
        out_shape=jax.ShapeDtypeStruct((nr, d), F32),
        compiler_params=_cparams(("arbitrary", "arbitrary")),
        name="moe_experts",
    )(tile_expert, n_used, xs, wg, wu, wd)


def _combine_kernel(pos_ref, posn_ref, x_ref, mod_ref, p_ref, y_ref, o_ref, ybuf, sem):
    i = pl.program_id(0)
    last = pl.num_programs(0) - 1
    tm = x_ref.shape[0]
    buf = i % 2

    def gather(idx_ref, b):
        def start(r, c):
            for slot in range(2):
                pltpu.make_async_copy(y_ref.at[pl.ds(idx_ref[0, 0, slot * tm + r], 1), :],
                                      ybuf.at[b, slot, pl.ds(r, 1), :], sem.at[b]).start(priority=slot)
            return c

        lax.fori_loop(0, tm, start, 0, unroll=8)

    @pl.when(i == 0)
    def _():
        gather(pos_ref, buf)

    @pl.when(i < last)
    def _():
        gather(posn_ref, 1 - buf)

    for slot in range(2):
        pltpu.make_async_copy(y_ref.at[pl.ds(0, tm), :], ybuf.at[buf, slot], sem.at[buf]).wait()
    p = p_ref[...]
    f = p[:, 0:1] * ybuf[buf, 0] + p[:, 1:2] * ybuf[buf, 1]
    o_ref[...] = x_ref[...] + mod_ref[0, 5:6, :] * f


def combine_call(pos3, x, mod, p, y, *, seg, tm):
    t, d = x.shape
    nt = t // tm
    return pl.pallas_call(
        _combine_kernel,
        grid=(nt,),
        in_specs=[
            pl.BlockSpec((1, 1, 2 * tm), lambda i: (i, 0, 0), memory_space=pltpu.SMEM),
            pl.BlockSpec((1, 1, 2 * tm), lambda i: (jnp.minimum(i + 1, nt - 1), 0, 0),
                         memory_space=pltpu.SMEM),
            pl.BlockSpec((tm, d), lambda i: (i, 0)),
            pl.BlockSpec((1, SUBLANES, d), lambda i: (i * tm // seg, 0, 0)),
            pl.BlockSpec((tm, LANES), lambda i: (i, 0)),
            pl.BlockSpec(memory_space=pl.ANY),
        ],
        out_specs=pl.BlockSpec((tm, d), lambda i: (i, 0)),
        out_shape=jax.ShapeDtypeStruct((t, d), F32),
        scratch_shapes=[pltpu.VMEM((2, 2, tm, d), F32), pltpu.SemaphoreType.DMA((2,))],
        compiler_params=_cparams(("arbitrary",)),
        name="moe_combine",
    )(pos3, pos3, x, mod, p, y)


def router_operands(wr, tm):
    d = wr.shape[0]
    wr_pad = jnp.zeros((d, LANES), BF16).at[:, :N_EXPERTS].set(wr.astype(BF16))
    tri = jnp.asarray(np.tril(np.ones((tm, tm), np.float32), k=-1), BF16)
    return wr_pad, tri


def moe_block(x, routed, mod, nw, wg, wu, wd, *, seg, tmd, tme):
    t, d = x.shape
    oi, op, cnt = routed
    counts = cnt[0, :N_EXPERTS].astype(jnp.int32)
    padded = ((counts + tme - 1) // tme) * tme
    ends = jnp.cumsum(padded)
    starts = ends - padded
    nr = ((2 * t + N_EXPERTS * (tme - 1)) // tme) * tme
    ntiles = nr // tme
    n_used = (ends[-1] // tme).astype(jnp.int32).reshape(1)
    tile_start = jnp.arange(ntiles, dtype=jnp.int32) * tme
    tile_expert = jnp.minimum(jnp.sum(tile_start[:, None] >= ends[None, :], axis=1),
                              N_EXPERTS - 1).astype(jnp.int32)
    pos1 = starts[oi[:, 0]] + oi[:, 2]
    pos2 = starts[oi[:, 1]] + oi[:, 3]
    pos3 = jnp.concatenate([pos1.reshape(t // tmd, 1, tmd), pos2.reshape(t // tmd, 1, tmd)], axis=2)
    gap_lo = jnp.concatenate([starts + counts, ends[-1:]])
    gap_n = jnp.concatenate([padded - counts, nr - ends[-1:]])
    gaps3 = jnp.stack([gap_lo, gap_n], axis=1).reshape(1, 1, -1).astype(jnp.int32)
    xs = dispatch_call(pos3, gaps3, x, mod, nw, nr=nr, seg=seg, tm=tmd)
    y = expert_call(tile_expert, n_used, xs, wg, wu, wd, tme=tme)
    return combine_call(pos3, x, mod, op, y, seg=seg, tm=tmd)


def _final_kernel(x_ref, w_ref, o_ref):
    x = x_ref[...]
    ms = jnp.mean(x * x, axis=-1, keepdims=True)
    o_ref[...] = (x * lax.rsqrt(ms + NORM_EPS)) * w_ref[...]


def final_call(x, w, *, row0, nrows, tm):
    d = x.shape[1]
    off = row0 // tm
    return pl.pallas_call(
        _final_kernel,
        grid=(nrows // tm,),
        in_specs=[pl.BlockSpec((tm, d), lambda i: (i + off, 0)), pl.BlockSpec((1, d), lambda i: (0, 0))],
        out_specs=pl.BlockSpec((tm, d), lambda i: (i, 0)),
        out_shape=jax.ShapeDtypeStruct((nrows, d), F32),
        compiler_params=_cparams(("arbitrary",)),
        name="final_norm",
    )(x, w)


def _segment_flags(prompt_segs, sample_segs, seg, tile):
    per_seg = seg // tile
    nseg = prompt_segs + sample_segs
    first = np.zeros(nseg * per_seg, np.int32)
    last = np.zeros(nseg * per_seg, np.int32)
    first[0] = 1
    last[prompt_segs * per_seg - 1] = 1
    for s in range(prompt_segs, nseg):
        first[s * per_seg] = 1
        last[(s + 1) * per_seg - 1] = 1
    return first, last


def encoder_pair(x_prompt, x_sample, c_prompt, c_sample, w_ada, b_ada, norm_mix, norm_ffn, w_in, conv_w,
                 a_log, dt_bias, gdn_norm, w_out, w_ffn_gate, w_ffn_up, w_ffn_down, w_router,
                 w_exp_gate, w_exp_up, w_exp_down, norm_final, *, tm=512, tb=512, tmd=256, tme=512):
    bp, sp, d = x_prompt.shape
    bs, seg, _ = x_sample.shape
    assert bp == 1 and sp % seg == 0 and seg % tm == 0 and tm % DFT1 == 0
    depth = w_ada.shape[0]
    prompt_segs = sp // seg
    nseg = prompt_segs + bs
    t = nseg * seg
    x = jnp.concatenate([x_prompt.reshape(sp, d), x_sample.reshape(bs * seg, d)], axis=0)

    nrow = -(-(1 + bs) // SUBLANES) * SUBLANES
    c_all = jnp.zeros((nrow, d), F32).at[0:1].set(c_prompt).at[1:1 + bs].set(c_sample)
    ada = ada_call(c_all, w_ada, b_ada).reshape(depth, nrow, 6, d)
    seg_row = np.concatenate([np.zeros(prompt_segs, np.int32), 1 + np.arange(bs, dtype=np.int32)])
    mod_all = jnp.pad(ada[:, seg_row], ((0, 0), (0, 0), (0, SUBLANES - 6), (0, 0)))

    first_c, last_c = _segment_flags(prompt_segs, bs, seg, tm)
    first_g, last_g = _segment_flags(prompt_segs, bs, seg, tb)
    reset_f = jnp.asarray(first_g)
    reset_b = jnp.asarray(last_g[::-1].copy())
    first_c, last_c = jnp.asarray(first_c), jnp.asarray(last_c)

    off_f, off_qkv, off_z = F_DIM, F_DIM + QKV_DIM, F_DIM + QKV_DIM + G_DIM
    off_b = off_z + 2 * NH
    for l in range(depth):
        mod = mod_all[l]
        wl = w_in[l]
        wf = wl[:, :off_f].astype(BF16)
        wqkv = wl[:, off_f:off_qkv].astype(BF16)
        wz = wl[:, off_qkv:off_z].astype(BF16)
        wba = jnp.pad(wl[:, off_z:], ((0, 0), (0, LANES - 4 * NH))).astype(BF16)
        cw = jnp.pad(conv_w[l], ((0, SUBLANES - CONV_K), (0, 0)))
        gp = jnp.zeros((SUBLANES, LANES), F32)
        gp = gp.at[0, 2 * NH:4 * NH].set(dt_bias[l].reshape(-1)).at[1, 2 * NH:4 * NH].set(a_log[l].reshape(-1))
        f, z, q, k, v, gcol, grow, gend = inproj_call(first_c, last_c, x, mod, norm_mix[l].reshape(1, d),
                                                      wf, wqkv, wz, wba, cw, gp, seg=seg, tm=tm)
        o_f, o_b = gdn_call(reset_f, reset_b, q, k, v, gcol, grow, gend, tb=tb)

        s2p = sp // DFT1
        fm_p = fourier_mix_call(f[:sp], nseq=1, s=sp, tcol=min(2048, s2p * F_DIM),
                                tk=min(16, DFT1), nsplit=prompt_segs)
        s2s = seg // DFT1
        fm_s = fourier_mix_call(f[sp:], nseq=bs, s=seg, tcol=min(2048, s2s * F_DIM),
                                tk=min(64, DFT1), nsplit=1)
        fm2 = jnp.concatenate([fm_p, fm_s], axis=0).reshape(nseg * DFT1, (seg // DFT1) * F_DIM)

        nw = norm_ffn[l].reshape(1, d)
        i = l // 2
        gn = gdn_norm[l].reshape(1, HD)
        if l % 2 == 0:
            ffn = (nw, w_ffn_gate[i].astype(BF16), w_ffn_up[i].astype(BF16), w_ffn_down[i].astype(BF16))
            x = outproj_call(x, mod, o_f, o_b, z, fm2, gn, w_out[l].astype(BF16), ffn=ffn, seg=seg, tm=tm)
        else:
            x, *routed = outproj_call(x, mod, o_f, o_b, z, fm2, gn, w_out[l].astype(BF16),
                                      router=(nw,) + router_operands(w_router[i], tm), seg=seg, tm=tm)
            x = moe_block(x, routed, mod, nw, w_exp_gate[i].astype(BF16), w_exp_up[i].astype(BF16),
                          w_exp_down[i].astype(BF16), seg=seg, tmd=tmd, tme=tme)

    wn = norm_final.reshape(1, d)
    y_p = final_call(x, wn, row0=0, nrows=sp, tm=tm).reshape(bp, sp, d)
    y_s = final_call(x, wn, row0=sp, nrows=bs * seg, tm=tm).reshape(bs, seg, d)
    return y_p, y_s


def kernel(x_prompt, x_sample, c_prompt, c_sample, w_ada, b_ada, norm_mix, norm_ffn, w_in, conv_w, a_log,
           dt_bias, gdn_norm, w_out, w_ffn_gate, w_ffn_up, w_ffn_down, w_router, w_exp_gate, w_exp_up,
           w_exp_down, norm_final):
    return encoder_pair(x_prompt, x_sample, c_prompt, c_sample, w_ada, b_ada, norm_mix, norm_ffn, w_in,
                        conv_w, a_log, dt_bias, gdn_norm, w_out, w_ffn_gate, w_ffn_up, w_ffn_down,
                        w_router, w_exp_gate, w_exp_up, w_exp_down, norm_final)
```
